```python
import jax, jax.numpy as jnp
from jax import lax
import numpy as np

D_MODEL = 1024
BATCH = 16
SEQ = 2048
DEPTH = 1

HEAD_DIM = 64
A_HEADS = D_MODEL // (2 * HEAD_DIM)
B_HEADS = D_MODEL // (2 * HEAD_DIM)
A_WIDTH = A_HEADS * HEAD_DIM
B_WIDTH = B_HEADS * HEAD_DIM
MIX_WIDTH = A_WIDTH + B_WIDTH
IN_WIDTH = 2 * A_WIDTH + 3 * B_WIDTH
CHUNK = 128
DILATED_BRANCHES = ((128, 1), (512, 4), (2048, 16))
BAND_BLOCK = 128
ROPE_THETA = 10000.0
N_EXPERTS = 32
TOP_K = 4
D_FF = D_MODEL
SWIGLU_ALPHA = 1.702
SWIGLU_LIMIT = 7.0
NORM_EPS = 1e-5
NEG_INF = -1e30

kernel_name = "hymba_gmlp_dilated_moe_block"


def _rmsnorm(x, g):
    xf = x.astype(jnp.float32)
    y = xf * lax.rsqrt(jnp.mean(xf * xf, axis=-1, keepdims=True) + NORM_EPS)
    return (y * g.astype(jnp.float32)).astype(x.dtype)


def _rope_tables(s):
    pos = jnp.arange(s, dtype=jnp.float32)
    inv = ROPE_THETA ** (-jnp.arange(0, HEAD_DIM, 2, dtype=jnp.float32) / HEAD_DIM)
    ang = pos[:, None] * inv[None, :]
    emb = jnp.concatenate([ang, ang], axis=-1)
    return jnp.cos(emb), jnp.sin(emb)


def _apply_rope(t, cos, sin):
    tf = t.astype(jnp.float32)
    half = HEAD_DIM // 2
    rot = jnp.concatenate([-tf[..., half:], tf[..., :half]], axis=-1)
    return (tf * cos[:, None, :] + rot * sin[:, None, :]).astype(t.dtype)


def _spatial_gating(u, v, ln_g, ln_b, w_s, b_s):
    b, s, h, dh = v.shape
    u = jax.nn.gelu(u.astype(jnp.float32))
    vf = jax.nn.gelu(v.astype(jnp.float32))
    mu = jnp.mean(vf, axis=-1, keepdims=True)
    var = jnp.mean(jnp.square(vf - mu), axis=-1, keepdims=True)
    vn = (vf - mu) * lax.rsqrt(var + NORM_EPS) * ln_g.astype(jnp.float32) + ln_b.astype(jnp.float32)
    nc = s // CHUNK
    vn = vn.reshape(b, nc, CHUNK, h, dh)
    causal = jnp.tril(jnp.ones((CHUNK, CHUNK), dtype=bool))
    ws = jnp.where(causal[None], w_s.astype(jnp.float32), 0.0)
    g = jnp.einsum('hts,bnshd->bnthd', ws, vn) + b_s.astype(jnp.float32).T[:, :, None]
    return u * g.reshape(b, s, h, dh)


def _dilated_branch(q, k, v, steps, dil, scale):
    b, h, s, dh = q.shape
    L = s // dil
    nb = -(-L // BAND_BLOCK)
    Lp = nb * BAND_BLOCK

    def split(t):
        t = t.reshape(b, h, L, dil, dh).transpose(0, 1, 3, 2, 4)
        t = jnp.pad(t, ((0, 0), (0, 0), (0, 0), (0, Lp - L), (0, 0)))
        return t.reshape(b, h, dil, nb, BAND_BLOCK, dh)

    def with_prev(t):
        prev = jnp.concatenate([jnp.zeros_like(t[:, :, :, :1]), t[:, :, :, :-1]], axis=3)
        return jnp.concatenate([prev, t], axis=4)

    qb = split(q)
    kw = with_prev(split(k))
    vw = with_prev(split(v))
    sc = jnp.einsum('bhrnqd,bhrnkd->bhrnqk', qb, kw,
                    preferred_element_type=jnp.float32) * scale
    qpos = jnp.arange(BAND_BLOCK)[:, None] + BAND_BLOCK
    kpos = jnp.arange(2 * BAND_BLOCK)[None, :]
    rel = qpos - kpos
    band = (rel >= 0) & (rel <= steps)
    not_first = (jnp.arange(nb) > 0)[:, None, None]
    mask = band[None] & (not_first | (kpos >= BAND_BLOCK)[None])
    sc = jnp.where(mask, sc, NEG_INF)
    m = jnp.max(sc, axis=-1, keepdims=True)
    p = jnp.exp(sc - m)
    l = jnp.sum(p, axis=-1, keepdims=True)
    o = jnp.einsum('bhrnqk,bhrnkd->bhrnqd', p, vw.astype(jnp.float32)) / l
    lse = (m + jnp.log(l))[..., 0]
    o = o.reshape(b, h, dil, Lp, dh)[:, :, :, :L].transpose(0, 1, 3, 2, 4).reshape(b, h, s, dh)
    lse = lse.reshape(b, h, dil, Lp)[..., :L].transpose(0, 1, 3, 2).reshape(b, h, s)
    return o, lse


def _dilated_mixture(q, k, v):
    scale = HEAD_DIM ** -0.5
    outs, lses = [], []
    for window, dil in DILATED_BRANCHES:
        o, lse = _dilated_branch(q, k, v, window // dil, dil, scale)
        outs.append(o)
        lses.append(lse)
    w = jax.nn.softmax(jnp.stack(lses), axis=0)
    return jnp.einsum('nbhs,nbhsd->bhsd', w, jnp.stack(outs))


def _moe(xn, w_router, b_router, w_gate_up, b_gate_up, w_down, b_down):
    b, s, d = xn.shape
    xf = xn.reshape(-1, d)
    t = xf.shape[0]
    logits = jnp.dot(xf, w_router, preferred_element_type=jnp.float32) + b_router.astype(jnp.float32)
    top_val, top_idx = lax.top_k(logits, TOP_K)
    gates = jax.nn.softmax(top_val, axis=-1)
    flat_e = top_idx.reshape(-1)
    order = jnp.argsort(flat_e)
    e_sorted = flat_e[order]
    tok = order // TOP_K
    gate_sorted = gates.reshape(-1)[order]
    group_sizes = jnp.bincount(flat_e, length=N_EXPERTS).astype(jnp.int32)
    xs = xf[tok]
    hgu = lax.ragged_dot(xs, w_gate_up, group_sizes) + b_gate_up[e_sorted]
    gate = jnp.minimum(hgu[:, :D_FF], SWIGLU_LIMIT)
    up = jnp.clip(hgu[:, D_FF:], -SWIGLU_LIMIT, SWIGLU_LIMIT)
    act = (up + 1) * (gate * jax.nn.sigmoid(gate * SWIGLU_ALPHA))
    y = lax.ragged_dot(act, w_down, group_sizes) + b_down[e_sorted]
    y = y * gate_sorted[:, None].astype(y.dtype)
    out = jnp.zeros((t, d), y.dtype).at[tok].add(y)
    return out.reshape(b, s, d).astype(xn.dtype)


def setup_inputs(seed: int = 0) -> dict:
    key = jax.random.key(seed)
    ks = jax.random.split(key, 20)
    f32 = jnp.float32
    n = lambda k, shape, sc: jax.random.normal(k, shape, f32) * sc
    return {
        "x": jax.random.normal(ks[0], (BATCH, SEQ, D_MODEL), f32),
        "norm1_g": 1.0 + n(ks[1], (DEPTH, D_MODEL), 0.02),
        "w_in": n(ks[2], (DEPTH, D_MODEL, IN_WIDTH), D_MODEL ** -0.5),
        "a_ln_g": 1.0 + n(ks[3], (DEPTH, A_HEADS, HEAD_DIM), 0.02),
        "a_ln_b": n(ks[4], (DEPTH, A_HEADS, HEAD_DIM), 0.02),
        "a_w_s": n(ks[5], (DEPTH, A_HEADS, CHUNK, CHUNK), CHUNK ** -0.5),
        "a_b_s": 1.0 + n(ks[6], (DEPTH, A_HEADS, CHUNK), 0.02),
        "a_out_g": 1.0 + n(ks[7], (DEPTH, A_WIDTH), 0.02),
        "b_out_g": 1.0 + n(ks[8], (DEPTH, B_WIDTH), 0.02),
        "w_out": n(ks[9], (DEPTH, MIX_WIDTH, D_MODEL), MIX_WIDTH ** -0.5),
        "norm2_g": 1.0 + n(ks[10], (DEPTH, D_MODEL), 0.02),
        "w_router": n(ks[11], (DEPTH, D_MODEL, N_EXPERTS), D_MODEL ** -0.5),
        "b_router": n(ks[12], (DEPTH, N_EXPERTS), 0.01),
        "w_gate_up": n(ks[13], (DEPTH, N_EXPERTS, D_MODEL, 2 * D_FF), D_MODEL ** -0.5),
        "b_gate_up": n(ks[14], (DEPTH, N_EXPERTS, 2 * D_FF), 0.01),
        "w_down": n(ks[15], (DEPTH, N_EXPERTS, D_FF, D_MODEL), D_FF ** -0.5),
        "b_down": n(ks[16], (DEPTH, N_EXPERTS, D_MODEL), 0.01),
        "normf_g": 1.0 + n(ks[17], (D_MODEL,), 0.02),
    }


def reference(x, norm1_g, w_in, a_ln_g, a_ln_b, a_w_s, a_b_s, a_out_g, b_out_g, w_out,
              norm2_g, w_router, b_router, w_gate_up, b_gate_up, w_down, b_down, normf_g):
    b, s, _ = x.shape
    cos, sin = _rope_tables(s)
    cuts = [A_WIDTH, 2 * A_WIDTH, 2 * A_WIDTH + B_WIDTH, 2 * A_WIDTH + 2 * B_WIDTH]
    h = x
    for layer in range(DEPTH):
        xn = _rmsnorm(h, norm1_g[layer])
        proj = jnp.einsum('bsd,de->bse', xn, w_in[layer])
        a_u, a_v, q, k, v = jnp.split(proj, cuts, axis=-1)
        a_out = _spatial_gating(a_u.reshape(b, s, A_HEADS, HEAD_DIM),
                                a_v.reshape(b, s, A_HEADS, HEAD_DIM),
                                a_ln_g[layer], a_ln_b[layer], a_w_s[layer], a_b_s[layer])
        a_out = _rmsnorm(a_out.reshape(b, s, A_WIDTH), a_out_g[layer])
        q = _apply_rope(q.reshape(b, s, B_HEADS, HEAD_DIM), cos, sin).transpose(0, 2, 1, 3)
        k = _apply_rope(k.reshape(b, s, B_HEADS, HEAD_DIM), cos, sin).transpose(0, 2, 1, 3)
        v = v.reshape(b, s, B_HEADS, HEAD_DIM).transpose(0, 2, 1, 3)
        b_out = _dilated_mixture(q, k, v).transpose(0, 2, 1, 3).reshape(b, s, B_WIDTH)
        b_out = _rmsnorm(b_out, b_out_g[layer])
        mixed = jnp.concatenate([a_out, b_out], axis=-1).astype(h.dtype)
        h = h + jnp.einsum('bse,ed->bsd', mixed, w_out[layer])
        hn = _rmsnorm(h, norm2_g[layer])
        h = h + _moe(hn, w_router[layer], b_router[layer], w_gate_up[layer],
                     b_gate_up[layer], w_down[layer], b_down[layer])
    return _rmsnorm(h, normf_g)
```

```python
import functools
import math

import numpy as np
import jax
import jax.numpy as jnp
from jax import lax
from jax.experimental import pallas as pl
from jax.experimental.pallas import tpu as pltpu

F32 = jnp.float32
BF16 = jnp.bfloat16
I32 = jnp.int32

D_MODEL = 1024
HEAD_DIM = 64
A_WIDTH = 512
B_WIDTH = 512
CHUNK = 128
BAND = 128
DILATIONS = (1, 4, 16)
ROPE_THETA = 10000.0
N_EXPERTS = 32
TOP_K = 4
D_FF = 1024
SWIGLU_ALPHA = 1.702
SWIGLU_LIMIT = 7.0
NORM_EPS = 1e-5
NEG_INF = -1e30

LANES = 128
SUBLANES = 8
ROW_TILES = D_MODEL // LANES

TM_PROJ = 512
TM_MOE = 512
TM_COMB = 256
VMEM_LIMIT = 56 * 1024 * 1024


def _dot(a, b):
    return jnp.dot(a, b, preferred_element_type=F32)


def _gelu_tanh(x):
    c = math.sqrt(2.0 / math.pi)
    cdf = 0.5 * (1.0 + jnp.tanh(c * (x + 0.044715 * (x * x * x))))
    return x * cdf


def _rms(x, g):
    return x * lax.rsqrt(jnp.mean(x * x, axis=-1, keepdims=True) + NORM_EPS) * g


def _inproj_kernel(x_ref, g1_ref, w_ref, avg_ref, lng_ref, lnb_ref, ws_ref, bs_ref, aog_ref,
                   cos_ref, sin_ref, a_ref, q_ref, k_ref, v_ref, a_scr):
    tm = x_ref.shape[0]
    xn = _rms(x_ref[...], g1_ref[...]).astype(BF16)

    ug = _gelu_tanh(_dot(xn, w_ref[:, 0:A_WIDTH]))
    vg = _gelu_tanh(_dot(xn, w_ref[:, A_WIDTH:2 * A_WIDTH]))
    avg = avg_ref[...]
    mu = _dot(vg.astype(BF16), avg)
    d = vg - mu
    var = _dot((d * d).astype(BF16), avg)
    vn = (d * lax.rsqrt(var + NORM_EPS) * lng_ref[...] + lnb_ref[...]).astype(BF16)
    first_head = lax.broadcasted_iota(I32, (CHUNK, LANES), 1) < HEAD_DIM
    for c in range(tm // CHUNK):
        rows = slice(c * CHUNK, (c + 1) * CHUNK)
        for p in range(A_WIDTH // LANES):
            cols = slice(p * LANES, (p + 1) * LANES)
            slab = vn[rows, cols]
            g = jnp.where(first_head, _dot(ws_ref[2 * p], slab), _dot(ws_ref[2 * p + 1], slab))
            a_scr[rows, cols] = ug[rows, cols] * (g + bs_ref[:, cols])
    a_ref[...] = _rms(a_scr[...], aog_ref[...]).astype(BF16)

    cos = cos_ref[...]
    sin = sin_ref[...]
    first_half = (lax.broadcasted_iota(I32, (tm, LANES), 1) % HEAD_DIM) < (HEAD_DIM // 2)

    def rope(t):
        rot = jnp.where(first_half, pltpu.roll(t, LANES - HEAD_DIM // 2, 1), pltpu.roll(t, HEAD_DIM // 2, 1))
        return t * cos + rot * sin

    off = 2 * A_WIDTH
    for p in range(B_WIDTH // LANES):
        cols = slice(p * LANES, (p + 1) * LANES)
        q = _dot(xn, w_ref[:, off + p * LANES: off + (p + 1) * LANES])
        q_ref[:, cols] = (rope(q) * (HEAD_DIM ** -0.5)).astype(BF16)
        k = _dot(xn, w_ref[:, off + B_WIDTH + p * LANES: off + B_WIDTH + (p + 1) * LANES])
        k_ref[:, cols] = rope(k).astype(BF16)
    v_ref[...] = _dot(xn, w_ref[:, off + 2 * B_WIDTH: off + 3 * B_WIDTH]).astype(BF16)


def _inproj(x2, g1, w_in, avg, lng, lnb, ws, bs, aog, cos, sin, seq):
    t = x2.shape[0]
    tm = TM_PROJ
    nseq = seq // tm
    full = lambda shape: pl.BlockSpec(shape, lambda i: (0,) * len(shape))
    rows = lambda w: pl.BlockSpec((tm, w), lambda i: (i, 0))
    return pl.pallas_call(
        _inproj_kernel,
        grid=(t // tm,),
        in_specs=[rows(D_MODEL), full((1, D_MODEL)), full(w_in.shape), full(avg.shape),
                  full((1, A_WIDTH)), full((1, A_WIDTH)), full(ws.shape), full(bs.shape),
                  full((1, A_WIDTH)),
                  pl.BlockSpec((tm, LANES), lambda i: (i % nseq, 0)),
                  pl.BlockSpec((tm, LANES), lambda i: (i % nseq, 0))],
        out_specs=[rows(A_WIDTH), rows(B_WIDTH), rows(B_WIDTH), rows(B_WIDTH)],
        out_shape=[jax.ShapeDtypeStruct((t, A_WIDTH), BF16)] + [jax.ShapeDtypeStruct((t, B_WIDTH), BF16)] * 3,
        scratch_shapes=[pltpu.VMEM((tm, A_WIDTH), F32)],
        compiler_params=pltpu.CompilerParams(dimension_semantics=("arbitrary",), vmem_limit_bytes=VMEM_LIMIT),
        name="inproj",
    )(x2, g1, w_in, avg, lng, lnb, ws, bs, aog, cos, sin)


def _attn_block(qb, kw, vw, bias, first_head):
    zero = jnp.zeros_like(qb)
    q2 = jnp.concatenate([jnp.where(first_head, qb, zero), jnp.where(first_head, zero, qb)], axis=0)
    s = lax.dot_general(q2, kw, (((1,), (1,)), ((), ())), preferred_element_type=F32) + bias
    m = jnp.max(s, axis=-1, keepdims=True)
    p = jnp.exp(s - m)
    l = jnp.sum(p, axis=-1, keepdims=True)
    o = _dot(p.astype(BF16), vw) / l
    lse = jnp.broadcast_to(m + jnp.log(l), o.shape)
    return (jnp.where(first_head, o[:BAND], o[BAND:]),
            jnp.where(first_head, lse[:BAND], lse[BAND:]))


def _attn_kernel(q1, k1, v1, q4, k4, v4, q16, k16, v16, bias_band_ref, bias_first_ref, o_ref, o_scr, l_scr):
    seq = q1.shape[0]
    first_head = lax.broadcasted_iota(I32, (BAND, LANES), 1) < HEAD_DIM
    branches = ((q1, k1, v1), (q4, k4, v4), (q16, k16, v16))
    for bi, dil in enumerate(DILATIONS):
        q_r, k_r, v_r = branches[bi]
        length = seq // dil
        for r in range(dil):
            for n in range(length // BAND):
                lo = n * BAND
                if bi == 0:
                    ref_slice = lambda ref, a, b: ref[a:b, :]
                else:
                    ref_slice = lambda ref, a, b, r=r: ref[0, r, a:b, :]
                qb = ref_slice(q_r, lo, lo + BAND)
                if n == 0:
                    kw, vw, bias = ref_slice(k_r, 0, BAND), ref_slice(v_r, 0, BAND), bias_first_ref[...]
                else:
                    kw, vw = ref_slice(k_r, lo - BAND, lo + BAND), ref_slice(v_r, lo - BAND, lo + BAND)
                    bias = bias_band_ref[...]
                o, lse = _attn_block(qb, kw, vw, bias, first_head)
                if dil == 1:
                    o_scr[bi, lo:lo + BAND, :] = o
                    l_scr[bi, lo:lo + BAND, :] = lse
                else:
                    dst = pl.ds(r + dil * lo, BAND, stride=dil)
                    o_scr[bi, dst, :] = o
                    l_scr[bi, dst, :] = lse
    lses = [l_scr[i] for i in range(3)]
    m = jnp.maximum(jnp.maximum(lses[0], lses[1]), lses[2])
    es = [jnp.exp(l - m) for l in lses]
    den = es[0] + es[1] + es[2]
    o_ref[...] = (es[0] / den) * o_scr[0] + (es[1] / den) * o_scr[1] + (es[2] / den) * o_scr[2]


def _attention(q, k, v, batch, seq):
    def classes(t, dil):
        return t.reshape(batch, seq // dil, dil, B_WIDTH).transpose(0, 2, 1, 3)

    rel = (np.arange(BAND)[:, None] + BAND) - np.arange(2 * BAND)[None, :]
    band = np.where((rel >= 0) & (rel <= BAND), 0.0, NEG_INF).astype(np.float32)
    bias_band = jnp.asarray(np.concatenate([band, band], axis=0))
    bias_first = jnp.asarray(np.concatenate([band[:, BAND:], band[:, BAND:]], axis=0))
    nat = pl.BlockSpec((seq, LANES), lambda b, p: (b, p))
    cls = lambda dil: pl.BlockSpec((1, dil, seq // dil, LANES), lambda b, p: (b, 0, 0, p))
    full = lambda a: pl.BlockSpec(a.shape, lambda b, p: (0, 0))
    args = [q, k, v] + [classes(t, 4) for t in (q, k, v)] + [classes(t, 16) for t in (q, k, v)]
    return pl.pallas_call(
        _attn_kernel,
        grid=(batch, B_WIDTH // LANES),
        in_specs=[nat] * 3 + [cls(4)] * 3 + [cls(16)] * 3 + [full(bias_band), full(bias_first)],
        out_specs=pl.BlockSpec((seq, LANES), lambda b, p: (b, p)),
        out_shape=jax.ShapeDtypeStruct((batch * seq, B_WIDTH), F32),
        scratch_shapes=[pltpu.VMEM((3, seq, LANES), F32), pltpu.VMEM((3, seq, LANES), F32)],
        compiler_params=pltpu.CompilerParams(dimension_semantics=("arbitrary", "arbitrary"),
                                             vmem_limit_bytes=VMEM_LIMIT),
        name="dilated_attention",
    )(*args, bias_band, bias_first)


def _mix_kernel(a_ref, bm_ref, x_ref, bog_ref, wout_ref, n2g_ref, wr_ref, br_ref,
                h_ref, hn_ref, idx_ref, gate_ref):
    tm = x_ref.shape[0]
    bn = _rms(bm_ref[...], bog_ref[...]).astype(BF16)
    mixed = jnp.concatenate([a_ref[...], bn], axis=1)
    h = x_ref[...] + _dot(mixed, wout_ref[...])
    h_ref[...] = h
    hn = _rms(h, n2g_ref[...])
    hn_ref[...] = hn
    logits = _dot(hn.astype(BF16), wr_ref[...]) + br_ref[...]
    lane = lax.broadcasted_iota(I32, (tm, LANES), 1)
    vals, idxs = [], []
    for _ in range(TOP_K):
        m = jnp.max(logits, axis=-1, keepdims=True)
        am = jnp.min(jnp.where(logits == m, lane, LANES), axis=-1, keepdims=True)
        vals.append(m)
        idxs.append(am)
        logits = jnp.where(lane == am, -jnp.inf, logits)
    es = [jnp.exp(v - vals[0]) for v in vals]
    den = es[0] + es[1] + es[2] + es[3]
    idx_out = jnp.zeros((tm, LANES), I32)
    gate_out = jnp.zeros((tm, LANES), F32)
    for kk in range(TOP_K):
        idx_out = jnp.where(lane == kk, idxs[kk], idx_out)
        gate_out = jnp.where(lane == kk, es[kk] / den, gate_out)
    idx_ref[...] = idx_out
    gate_ref[...] = gate_out


def _mix(a_out, b_mix, x2, bog, w_out, n2g, w_r, b_r):
    t = x2.shape[0]
    tm = TM_PROJ
    full = lambda shape: pl.BlockSpec(shape, lambda i: (0,) * len(shape))
    rows = lambda w: pl.BlockSpec((tm, w), lambda i: (i, 0))
    return pl.pallas_call(
        _mix_kernel,
        grid=(t // tm,),
        in_specs=[rows(A_WIDTH), rows(B_WIDTH), rows(D_MODEL), full((1, B_WIDTH)), full(w_out.shape),
                  full((1, D_MODEL)), full(w_r.shape), full((1, LANES))],
        out_specs=[rows(D_MODEL), rows(D_MODEL), rows(LANES), rows(LANES)],
        out_shape=[jax.ShapeDtypeStruct((t, D_MODEL), F32), jax.ShapeDtypeStruct((t, D_MODEL), F32),
                   jax.ShapeDtypeStruct((t, LANES), I32), jax.ShapeDtypeStruct((t, LANES), F32)],
        compiler_params=pltpu.CompilerParams(dimension_semantics=("arbitrary",), vmem_limit_bytes=VMEM_LIMIT),
        name="mix_router",
    )(a_out, b_mix, x2, bog, w_out, n2g, w_r, b_r)


def _row_copy_in(hn_hbm, xbuf, sem, slot, src_row, r):
    return pltpu.make_async_copy(hn_hbm.at[pl.ds(pl.multiple_of(src_row, SUBLANES), SUBLANES), :],
                                 xbuf.at[slot, pl.ds(pl.multiple_of(r * SUBLANES, SUBLANES), SUBLANES), :],
                                 sem.at[slot])


def _row_copy_out(ybuf, yk_hbm, sem, slot, dst_row, r):
    return pltpu.make_async_copy(ybuf.at[slot, pl.ds(pl.multiple_of(r * SUBLANES, SUBLANES), SUBLANES), :],
                                 yk_hbm.at[pl.ds(pl.multiple_of(dst_row, SUBLANES), SUBLANES), :],
                                 sem.at[slot])


def _moe_kernel(te_ref, nt_ref, src_ref, src_next_ref, dst_ref, hn_hbm, wgu_ref, bgu_ref, wd_ref, bd_ref,
                yk_hbm, xbuf, ybuf, gsem, ssem):
    del te_ref
    tm = TM_MOE
    tile_rows = tm * ROW_TILES
    i = pl.program_id(0)
    nt = nt_ref[0]
    slot = i % 2

    def gather(idx_ref, s):
        def body(r, c):
            _row_copy_in(hn_hbm, xbuf, gsem, s, idx_ref[0, 0, r], r).start()
            return c
        lax.fori_loop(0, tm, body, 0)

    def wait_gather(s):
        pltpu.make_async_copy(hn_hbm.at[pl.ds(0, tile_rows), :], xbuf.at[s], gsem.at[s]).wait()

    def wait_scatter(s):
        pltpu.make_async_copy(ybuf.at[s], yk_hbm.at[pl.ds(0, tile_rows), :], ssem.at[s]).wait()

    @pl.when(i == 0)
    def _():
        gather(src_ref, 0)
        ybuf[1] = jnp.zeros((tile_rows, LANES), F32)
        spare = pltpu.make_async_copy(ybuf.at[1], yk_hbm.at[pl.ds(yk_hbm.shape[0] - tile_rows, tile_rows), :],
                                      ssem.at[1])
        spare.start()
        spare.wait()

    @pl.when(i + 1 < nt)
    def _():
        gather(src_next_ref, 1 - slot)

    @pl.when(i < nt)
    def _():
        wait_gather(slot)

        @pl.when(i >= 2)
        def _():
            wait_scatter(slot)

        xs = xbuf.at[slot]
        x = jnp.concatenate([xs[pl.ds(j, tm, stride=ROW_TILES), :] for j in range(ROW_TILES)], axis=1).astype(BF16)
        y = jnp.zeros((tm, D_MODEL), F32) + bd_ref[0]
        half = D_FF // 2
        for c in range(2):
            gate = _dot(x, wgu_ref[0, :, c * half:(c + 1) * half]) + bgu_ref[0, :, c * half:(c + 1) * half]
            up = (_dot(x, wgu_ref[0, :, D_FF + c * half:D_FF + (c + 1) * half])
                  + bgu_ref[0, :, D_FF + c * half:D_FF + (c + 1) * half])
            gate = jnp.minimum(gate, SWIGLU_LIMIT)
            up = jnp.clip(up, -SWIGLU_LIMIT, SWIGLU_LIMIT)
            act = (up + 1.0) * (gate * jax.nn.sigmoid(gate * SWIGLU_ALPHA))
            y = y + _dot(act.astype(BF16), wd_ref[0, c * half:(c + 1) * half, :])
        ys = ybuf.at[slot]
        for j in range(ROW_TILES):
            ys[pl.ds(j, tm, stride=ROW_TILES), :] = y[:, j * LANES:(j + 1) * LANES]

        def body(r, c):
            _row_copy_out(ybuf, yk_hbm, ssem, slot, dst_ref[0, 0, r], r).start()
            return c
        lax.fori_loop(0, tm, body, 0)

    @pl.when(i == pl.num_programs(0) - 1)
    def _():
        wait_scatter((nt - 1) % 2)

        @pl.when(nt >= 2)
        def _():
            wait_scatter(nt % 2)


def _experts(tile_expert, n_tiles, src_rows, dst_rows, hn_tiles, wgu, bgu, wd, bd, out_rows):
    nt_max = src_rows.shape[0]
    tm = TM_MOE
    smem = lambda f: pl.BlockSpec((1, 1, tm), f, memory_space=pltpu.SMEM)
    grid_spec = pltpu.PrefetchScalarGridSpec(
        num_scalar_prefetch=2,
        grid=(nt_max,),
        in_specs=[smem(lambda i, te, nt: (i, 0, 0)),
                  smem(lambda i, te, nt: (jnp.minimum(i + 1, nt_max - 1), 0, 0)),
                  smem(lambda i, te, nt: (i, 0, 0)),
                  pl.BlockSpec(memory_space=pl.ANY),
                  pl.BlockSpec((1, D_MODEL, 2 * D_FF), lambda i, te, nt: (te[i], 0, 0)),
                  pl.BlockSpec((1, 1, 2 * D_FF), lambda i, te, nt: (te[i], 0, 0)),
                  pl.BlockSpec((1, D_FF, D_MODEL), lambda i, te, nt: (te[i], 0, 0)),
                  pl.BlockSpec((1, 1, D_MODEL), lambda i, te, nt: (te[i], 0, 0))],
        out_specs=pl.BlockSpec(memory_space=pl.ANY),
        scratch_shapes=[pltpu.VMEM((2, tm * ROW_TILES, LANES), F32), pltpu.VMEM((2, tm * ROW_TILES, LANES), F32),
                        pltpu.SemaphoreType.DMA((2,)), pltpu.SemaphoreType.DMA((2,))],
    )
    return pl.pallas_call(
        _moe_kernel,
        grid_spec=grid_spec,
        out_shape=jax.ShapeDtypeStruct((out_rows * ROW_TILES, LANES), F32),
        compiler_params=pltpu.CompilerParams(dimension_semantics=("arbitrary",), vmem_limit_bytes=VMEM_LIMIT),
        name="experts",
    )(tile_expert, n_tiles, src_rows, src_rows, dst_rows, hn_tiles, wgu, bgu, wd, bd)


def _combine_kernel(yk_ref, gate_ref, h_ref, g_ref, o_ref, acc_scr):
    tm = h_ref.shape[0]
    gates = gate_ref[...]
    gk = [jnp.broadcast_to(gates[:, kk:kk + 1], (tm, LANES)) for kk in range(TOP_K)]
    ss = jnp.zeros((tm, 1), F32)
    for j in range(ROW_TILES):
        cols = slice(j * LANES, (j + 1) * LANES)
        moe = gk[0] * yk_ref[pl.ds(j, tm, stride=TOP_K * ROW_TILES), :]
        for kk in range(1, TOP_K):
            moe = moe + gk[kk] * yk_ref[pl.ds(kk * ROW_TILES + j, tm, stride=TOP_K * ROW_TILES), :]
        acc = h_ref[:, cols] + moe
        acc_scr[:, cols] = acc
        ss = ss + jnp.sum(acc * acc, axis=-1, keepdims=True)
    inv = lax.rsqrt(ss / D_MODEL + NORM_EPS)
    o_ref[...] = acc_scr[...] * inv * g_ref[...]


def _combine(yk, gates, h, g):
    t = h.shape[0]
    tm = TM_COMB
    return pl.pallas_call(
        _combine_kernel,
        grid=(t // tm,),
        in_specs=[pl.BlockSpec((tm * TOP_K * ROW_TILES, LANES), lambda i: (i, 0)),
                  pl.BlockSpec((tm, LANES), lambda i: (i, 0)),
                  pl.BlockSpec((tm, D_MODEL), lambda i: (i, 0)),
                  pl.BlockSpec((1, D_MODEL), lambda i: (0, 0))],
        out_specs=pl.BlockSpec((tm, D_MODEL), lambda i: (i, 0)),
        out_shape=jax.ShapeDtypeStruct((t, D_MODEL), F32),
        scratch_shapes=[pltpu.VMEM((tm, D_MODEL), F32)],
        compiler_params=pltpu.CompilerParams(dimension_semantics=("arbitrary",), vmem_limit_bytes=VMEM_LIMIT),
        name="combine",
    )(yk, gates, h, g)


def _routing_plan(top_idx, n_tokens):
    tm = TM_MOE
    n_rows = n_tokens * TOP_K
    nt_max = n_rows // tm + N_EXPERTS
    flat_e = top_idx.reshape(-1)
    order = jnp.argsort(flat_e, stable=True).astype(I32)
    counts = jnp.sum((flat_e[:, None] == jnp.arange(N_EXPERTS, dtype=I32)[None, :]).astype(I32), axis=0)
    tiles_e = (counts + tm - 1) // tm
    tile_end = jnp.cumsum(tiles_e)
    tile_start = tile_end - tiles_e
    row_start = jnp.cumsum(counts) - counts
    n_tiles = tile_end[-1]
    tile_ids = jnp.arange(nt_max, dtype=I32)
    te = jnp.minimum(jnp.sum((tile_ids[:, None] >= tile_end[None, :]).astype(I32), axis=1), N_EXPERTS - 1)
    last_e = jnp.take(te, jnp.maximum(n_tiles - 1, 0))
    te = jnp.where(tile_ids < n_tiles, te, last_e).astype(I32)
    local = (tile_ids - jnp.take(tile_start, te))[:, None] * tm + jnp.arange(tm, dtype=I32)[None, :]
    valid = (local < jnp.take(counts, te)[:, None]) & (tile_ids < n_tiles)[:, None]
    sorted_pos = jnp.clip(jnp.take(row_start, te)[:, None] + local, 0, n_rows - 1)
    flat = jnp.take(order, sorted_pos)
    src = jnp.where(valid, flat // TOP_K, 0) * ROW_TILES
    dump = n_rows + jnp.arange(tm, dtype=I32)[None, :]
    dst = jnp.where(valid, flat, dump) * ROW_TILES
    return (te, n_tiles.reshape(1).astype(I32), src.reshape(nt_max, 1, tm).astype(I32),
            dst.reshape(nt_max, 1, tm).astype(I32))


def kernel(x, norm1_g, w_in, a_ln_g, a_ln_b, a_w_s, a_b_s, a_out_g, b_out_g, w_out, norm2_g, w_router,
           b_router, w_gate_up, b_gate_up, w_down, b_down, normf_g):
    batch, seq, _ = x.shape
    t = batch * seq
    assert seq % (TM_PROJ) == 0 and t % TM_MOE == 0 and seq // DILATIONS[-1] == BAND
    h = x.reshape(t, D_MODEL)

    pos = jnp.arange(seq, dtype=F32)
    inv = ROPE_THETA ** (-jnp.arange(0, HEAD_DIM, 2, dtype=F32) / HEAD_DIM)
    ang = pos[:, None] * inv[None, :]
    cos = jnp.tile(jnp.cos(ang), (1, 2 * LANES // HEAD_DIM))
    sin = jnp.tile(jnp.concatenate([-jnp.sin(ang), jnp.sin(ang)], axis=1), (1, LANES // HEAD_DIM))
    head_of_lane = np.arange(A_WIDTH) // HEAD_DIM
    avg = jnp.asarray((head_of_lane[:, None] == head_of_lane[None, :]).astype(np.float32) / HEAD_DIM, dtype=BF16)
    row2 = lambda v: v.reshape(1, -1).astype(F32)

    for layer in range(norm1_g.shape[0]):
        causal = np.tril(np.ones((CHUNK, CHUNK), dtype=bool))
        ws = jnp.where(causal[None], a_w_s[layer], 0.0).astype(BF16)
        bs = jnp.repeat(a_b_s[layer].astype(F32).T, HEAD_DIM, axis=1)
        a_out, q, k, v = _inproj(h, row2(norm1_g[layer]), w_in[layer].astype(BF16), avg,
                                 row2(a_ln_g[layer]), row2(a_ln_b[layer]), ws, bs, row2(a_out_g[layer]),
                                 cos, sin, seq)
        b_mix = _attention(q, k, v, batch, seq)
        w_r = jnp.pad(w_router[layer], ((0, 0), (0, LANES - N_EXPERTS))).astype(BF16)
        b_r = jnp.concatenate([b_router[layer].astype(F32), jnp.full((LANES - N_EXPERTS,), NEG_INF, F32)])
        h_mid, hn, idx, gates = _mix(a_out, b_mix, h, row2(b_out_g[layer]), w_out[layer].astype(BF16),
                                     row2(norm2_g[layer]), w_r, b_r.reshape(1, LANES))
        te, n_tiles, src, dst = _routing_plan(idx[:, :TOP_K], t)
        yk = _experts(te, n_tiles, src, dst, hn.reshape(t * ROW_TILES, LANES),
                      w_gate_up[layer].astype(BF16), b_gate_up[layer].reshape(N_EXPERTS, 1, 2 * D_FF),
                      w_down[layer].astype(BF16), b_down[layer].reshape(N_EXPERTS, 1, D_MODEL),
                      t * TOP_K + TM_MOE)
        last = layer == norm1_g.shape[0] - 1
        assert last, "the combine kernel fuses the final norm; depth > 1 is not supported"
        h = _combine(yk, gates, h_mid, row2(normf_g))
    return h.reshape(batch, seq, D_MODEL)
```

```python
import functools
import math

import numpy as np
import jax
import jax.numpy as jnp
from jax import lax
from jax.experimental import pallas as pl
from jax.experimental.pallas import tpu as pltpu

F32 = jnp.float32
BF16 = jnp.bfloat16
I32 = jnp.int32

D_MODEL = 1024
HEAD_DIM = 64
A_WIDTH = 512
B_WIDTH = 512
CHUNK = 128
BAND = 128
DILATIONS = (1, 4, 16)
ROPE_THETA = 10000.0
N_EXPERTS = 32
TOP_K = 4
D_FF = 1024
SWIGLU_ALPHA = 1.702
SWIGLU_LIMIT = 7.0
NORM_EPS = 1e-5
NEG_INF = -1e30

LANES = 128
SUBLANES = 8
ROW_TILES = D_MODEL // LANES

TM_PROJ = 512
TM_MOE = 512
TM_COMB = 256
DMA_UNROLL = 16
VMEM_LIMIT = 56 * 1024 * 1024


def _dot(a, b):
    return jnp.dot(a, b, preferred_element_type=F32)


def _gelu_tanh(x):
    c = math.sqrt(2.0 / math.pi)
    cdf = 0.5 * (1.0 + jnp.tanh(c * (x + 0.044715 * (x * x * x))))
    return x * cdf


def _rms(x, g):
    return x * lax.rsqrt(jnp.mean(x * x, axis=-1, keepdims=True) + NORM_EPS) * g


def _inproj_kernel(x_ref, g1_ref, w_ref, avg_ref, lng_ref, lnb_ref, ws_ref, bs_ref, aog_ref,
                   cos_ref, sin_ref, a_ref, q_ref, k_ref, v_ref, q4_ref, k4_ref, v4_ref,
                   q16_ref, k16_ref, v16_ref, a_scr, qkv_scr):
    tm = x_ref.shape[0]
    xn = _rms(x_ref[...], g1_ref[...]).astype(BF16)

    ug = _gelu_tanh(_dot(xn, w_ref[:, 0:A_WIDTH]))
    vg = _gelu_tanh(_dot(xn, w_ref[:, A_WIDTH:2 * A_WIDTH]))
    avg = avg_ref[...]
    mu = _dot(vg.astype(BF16), avg)
    d = vg - mu
    var = _dot((d * d).astype(BF16), avg)
    vn = (d * lax.rsqrt(var + NORM_EPS) * lng_ref[...] + lnb_ref[...]).astype(BF16)
    first_head = lax.broadcasted_iota(I32, (CHUNK, LANES), 1) < HEAD_DIM
    for c in range(tm // CHUNK):
        rows = slice(c * CHUNK, (c + 1) * CHUNK)
        for p in range(A_WIDTH // LANES):
            cols = slice(p * LANES, (p + 1) * LANES)
            slab = vn[rows, cols]
            g = jnp.where(first_head, _dot(ws_ref[2 * p], slab), _dot(ws_ref[2 * p + 1], slab))
            a_scr[rows, cols] = ug[rows, cols] * (g + bs_ref[:, cols])
    a_ref[...] = _rms(a_scr[...], aog_ref[...]).astype(BF16)

    cos = cos_ref[...]
    sin = sin_ref[...]
    first_half = (lax.broadcasted_iota(I32, (tm, LANES), 1) % HEAD_DIM) < (HEAD_DIM // 2)

    def rope(t):
        rot = jnp.where(first_half, pltpu.roll(t, LANES - HEAD_DIM // 2, 1), pltpu.roll(t, HEAD_DIM // 2, 1))
        return t * cos + rot * sin

    off = 2 * A_WIDTH
    nlt = B_WIDTH // LANES
    for p in range(nlt):
        cols = slice(p * LANES, (p + 1) * LANES)
        q = _dot(xn, w_ref[:, off + p * LANES: off + (p + 1) * LANES])
        qkv_scr[p] = rope(q) * (HEAD_DIM ** -0.5)
        k = _dot(xn, w_ref[:, off + B_WIDTH + p * LANES: off + B_WIDTH + (p + 1) * LANES])
        qkv_scr[nlt + p] = rope(k)
        qkv_scr[2 * nlt + p] = _dot(xn, w_ref[:, off + 2 * B_WIDTH + p * LANES: off + 2 * B_WIDTH + (p + 1) * LANES])

    for i, (nat_ref, c4_ref, c16_ref) in enumerate(((q_ref, q4_ref, q16_ref), (k_ref, k4_ref, k16_ref),
                                                    (v_ref, v4_ref, v16_ref))):
        for p in range(nlt):
            cols = slice(p * LANES, (p + 1) * LANES)
            nat_ref[:, cols] = qkv_scr[i * nlt + p].astype(BF16)
            for dil, c_ref in ((DILATIONS[1], c4_ref), (DILATIONS[2], c16_ref)):
                for r in range(dil):
                    c_ref[0, r, :, cols] = qkv_scr[i * nlt + p, pl.ds(r, tm // dil, stride=dil), :].astype(BF16)


def _inproj(x2, g1, w_in, avg, lng, lnb, ws, bs, aog, cos, sin, seq):
    t = x2.shape[0]
    tm = TM_PROJ
    nseq = seq // tm
    full = lambda shape: pl.BlockSpec(shape, lambda i: (0,) * len(shape))
    rows = lambda w: pl.BlockSpec((tm, w), lambda i: (i, 0))
    classes = lambda dil: pl.BlockSpec((1, dil, tm // dil, B_WIDTH), lambda i: (i // nseq, 0, i % nseq, 0))
    class_shape = lambda dil: jax.ShapeDtypeStruct((t // seq, dil, seq // dil, B_WIDTH), BF16)
    return pl.pallas_call(
        _inproj_kernel,
        grid=(t // tm,),
        in_specs=[rows(D_MODEL), full((1, D_MODEL)), full(w_in.shape), full(avg.shape),
                  full((1, A_WIDTH)), full((1, A_WIDTH)), full(ws.shape), full(bs.shape),
                  full((1, A_WIDTH)),
                  pl.BlockSpec((tm, LANES), lambda i: (i % nseq, 0)),
                  pl.BlockSpec((tm, LANES), lambda i: (i % nseq, 0))],
        out_specs=[rows(A_WIDTH)] + [rows(B_WIDTH)] * 3 + [classes(DILATIONS[1])] * 3 + [classes(DILATIONS[2])] * 3,
        out_shape=([jax.ShapeDtypeStruct((t, A_WIDTH), BF16)] + [jax.ShapeDtypeStruct((t, B_WIDTH), BF16)] * 3
                   + [class_shape(DILATIONS[1])] * 3 + [class_shape(DILATIONS[2])] * 3),
        scratch_shapes=[pltpu.VMEM((tm, A_WIDTH), F32), pltpu.VMEM((3 * B_WIDTH // LANES, tm, LANES), F32)],
        compiler_params=pltpu.CompilerParams(dimension_semantics=("arbitrary",), vmem_limit_bytes=VMEM_LIMIT),
        name="inproj",
    )(x2, g1, w_in, avg, lng, lnb, ws, bs, aog, cos, sin)


def _attn_block(qb, kw, vw, bias, first_head):
    zero = jnp.zeros_like(qb)
    q2 = jnp.concatenate([jnp.where(first_head, qb, zero), jnp.where(first_head, zero, qb)], axis=0)
    s = lax.dot_general(q2, kw, (((1,), (1,)), ((), ())), preferred_element_type=F32) + bias
    m = jnp.max(s, axis=-1, keepdims=True)
    p = jnp.exp(s - m)
    l = jnp.sum(p, axis=-1, keepdims=True)
    o = _dot(p.astype(BF16), vw) / l
    lse = jnp.broadcast_to(m + jnp.log(l), o.shape)
    return (jnp.where(first_head, o[:BAND], o[BAND:]),
            jnp.where(first_head, lse[:BAND], lse[BAND:]))


def _attn_kernel(q1, k1, v1, q4, k4, v4, q16, k16, v16, bias_band_ref, bias_first_ref, o_ref, o_scr, l_scr):
    seq = q1.shape[0]
    first_head = lax.broadcasted_iota(I32, (BAND, LANES), 1) < HEAD_DIM
    branches = ((q1, k1, v1), (q4, k4, v4), (q16, k16, v16))
    for bi, dil in enumerate(DILATIONS):
        q_r, k_r, v_r = branches[bi]
        length = seq // dil
        for r in range(dil):
            for n in range(length // BAND):
                lo = n * BAND
                if bi == 0:
                    ref_slice = lambda ref, a, b: ref[a:b, :]
                else:
                    ref_slice = lambda ref, a, b, r=r: ref[0, r, a:b, :]
                qb = ref_slice(q_r, lo, lo + BAND)
                if n == 0:
                    kw, vw, bias = ref_slice(k_r, 0, BAND), ref_slice(v_r, 0, BAND), bias_first_ref[...]
                else:
                    kw, vw = ref_slice(k_r, lo - BAND, lo + BAND), ref_slice(v_r, lo - BAND, lo + BAND)
                    bias = bias_band_ref[...]
                o, lse = _attn_block(qb, kw, vw, bias, first_head)
                if dil == 1:
                    o_scr[bi, lo:lo + BAND, :] = o
                    l_scr[bi, lo:lo + BAND, :] = lse
                else:
                    dst = pl.ds(r + dil * lo, BAND, stride=dil)
                    o_scr[bi, dst, :] = o
                    l_scr[bi, dst, :] = lse
    lses = [l_scr[i] for i in range(3)]
    m = jnp.maximum(jnp.maximum(lses[0], lses[1]), lses[2])
    es = [jnp.exp(l - m) for l in lses]
    den = es[0] + es[1] + es[2]
    o_ref[...] = (es[0] / den) * o_scr[0] + (es[1] / den) * o_scr[1] + (es[2] / den) * o_scr[2]


def _attention(qkv, batch, seq):
    rel = (np.arange(BAND)[:, None] + BAND) - np.arange(2 * BAND)[None, :]
    band = np.where((rel >= 0) & (rel <= BAND), 0.0, NEG_INF).astype(np.float32)
    bias_band = jnp.asarray(np.concatenate([band, band], axis=0))
    bias_first = jnp.asarray(np.concatenate([band[:, BAND:], band[:, BAND:]], axis=0))
    nat = pl.BlockSpec((seq, LANES), lambda b, p: (b, p))
    cls = lambda dil: pl.BlockSpec((1, dil, seq // dil, LANES), lambda b, p: (b, 0, 0, p))
    full = lambda a: pl.BlockSpec(a.shape, lambda b, p: (0, 0))
    return pl.pallas_call(
        _attn_kernel,
        grid=(batch, B_WIDTH // LANES),
        in_specs=[nat] * 3 + [cls(4)] * 3 + [cls(16)] * 3 + [full(bias_band), full(bias_first)],
        out_specs=pl.BlockSpec((seq, LANES), lambda b, p: (b, p)),
        out_shape=jax.ShapeDtypeStruct((batch * seq, B_WIDTH), F32),
        scratch_shapes=[pltpu.VMEM((3, seq, LANES), F32), pltpu.VMEM((3, seq, LANES), F32)],
        compiler_params=pltpu.CompilerParams(dimension_semantics=("arbitrary", "arbitrary"),
                                             vmem_limit_bytes=VMEM_LIMIT),
        name="dilated_attention",
    )(*qkv, bias_band, bias_first)


def _mix_kernel(a_ref, bm_ref, x_ref, bog_ref, wout_ref, n2g_ref, wr_ref, br_ref,
                h_ref, hn_ref, idx_ref, gate_ref):
    tm = x_ref.shape[0]
    bn = _rms(bm_ref[...], bog_ref[...]).astype(BF16)
    mixed = jnp.concatenate([a_ref[...], bn], axis=1)
    h = x_ref[...] + _dot(mixed, wout_ref[...])
    h_ref[...] = h
    hn = _rms(h, n2g_ref[...])
    for j in range(ROW_TILES):
        hn_ref[pl.ds(j, tm, stride=ROW_TILES), :] = hn[:, j * LANES:(j + 1) * LANES]
    logits = _dot(hn.astype(BF16), wr_ref[...]) + br_ref[...]
    lane = lax.broadcasted_iota(I32, (tm, LANES), 1)
    vals, idxs = [], []
    for _ in range(TOP_K):
        m = jnp.max(logits, axis=-1, keepdims=True)
        am = jnp.min(jnp.where(logits == m, lane, LANES), axis=-1, keepdims=True)
        vals.append(m)
        idxs.append(am)
        logits = jnp.where(lane == am, -jnp.inf, logits)
    es = [jnp.exp(v - vals[0]) for v in vals]
    den = es[0] + es[1] + es[2] + es[3]
    idx_out = jnp.zeros((tm, LANES), I32)
    gate_out = jnp.zeros((tm, LANES), F32)
    for kk in range(TOP_K):
        idx_out = jnp.where(lane == kk, idxs[kk], idx_out)
        gate_out = jnp.where(lane == kk, es[kk] / den, gate_out)
    idx_ref[...] = idx_out
    gate_ref[...] = gate_out


def _mix(a_out, b_mix, x2, bog, w_out, n2g, w_r, b_r):
    t = x2.shape[0]
    tm = TM_PROJ
    full = lambda shape: pl.BlockSpec(shape, lambda i: (0,) * len(shape))
    rows = lambda w: pl.BlockSpec((tm, w), lambda i: (i, 0))
    return pl.pallas_call(
        _mix_kernel,
        grid=(t // tm,),
        in_specs=[rows(A_WIDTH), rows(B_WIDTH), rows(D_MODEL), full((1, B_WIDTH)), full(w_out.shape),
                  full((1, D_MODEL)), full(w_r.shape), full((1, LANES))],
        out_specs=[rows(D_MODEL), pl.BlockSpec((tm * ROW_TILES, LANES), lambda i: (i, 0)), rows(LANES), rows(LANES)],
        out_shape=[jax.ShapeDtypeStruct((t, D_MODEL), F32), jax.ShapeDtypeStruct((t * ROW_TILES, LANES), F32),
                   jax.ShapeDtypeStruct((t, LANES), I32), jax.ShapeDtypeStruct((t, LANES), F32)],
        compiler_params=pltpu.CompilerParams(dimension_semantics=("arbitrary",), vmem_limit_bytes=VMEM_LIMIT),
        name="mix_router",
    )(a_out, b_mix, x2, bog, w_out, n2g, w_r, b_r)


def _row_copy_in(hn_hbm, xbuf, sem, slot, src_row, r):
    return pltpu.make_async_copy(hn_hbm.at[pl.ds(pl.multiple_of(src_row, SUBLANES), SUBLANES), :],
                                 xbuf.at[slot, pl.ds(pl.multiple_of(r * SUBLANES, SUBLANES), SUBLANES), :],
                                 sem.at[slot])


def _row_copy_out(ybuf, yk_hbm, sem, slot, dst_row, r):
    return pltpu.make_async_copy(ybuf.at[slot, pl.ds(pl.multiple_of(r * SUBLANES, SUBLANES), SUBLANES), :],
                                 yk_hbm.at[pl.ds(pl.multiple_of(dst_row, SUBLANES), SUBLANES), :],
                                 sem.at[slot])


def _moe_kernel(te_ref, nt_ref, src_ref, src_next_ref, dst_ref, hn_hbm, wgu_ref, bgu_ref, wd_ref, bd_ref,
                yk_hbm, xbuf, ybuf, wgu_bf, wd_bf, gsem, ssem):
    tm = TM_MOE
    tile_rows = tm * ROW_TILES
    i = pl.program_id(0)
    nt = nt_ref[0]
    slot = i % 2

    def gather(idx_ref, s):
        def body(g, c):
            for u in range(DMA_UNROLL):
                r = g * DMA_UNROLL + u
                _row_copy_in(hn_hbm, xbuf, gsem, s, idx_ref[0, 0, r], r).start()
            return c
        lax.fori_loop(0, tm // DMA_UNROLL, body, 0)

    def wait_gather(s):
        pltpu.make_async_copy(hn_hbm.at[pl.ds(0, tile_rows), :], xbuf.at[s], gsem.at[s]).wait()

    def wait_scatter(s):
        pltpu.make_async_copy(ybuf.at[s], yk_hbm.at[pl.ds(0, tile_rows), :], ssem.at[s]).wait()

    @pl.when(i == 0)
    def _():
        gather(src_ref, 0)
        ybuf[1] = jnp.zeros((tile_rows, LANES), F32)
        spare = pltpu.make_async_copy(ybuf.at[1], yk_hbm.at[pl.ds(yk_hbm.shape[0] - tile_rows, tile_rows), :],
                                      ssem.at[1])
        spare.start()
        spare.wait()

    @pl.when(i + 1 < nt)
    def _():
        gather(src_next_ref, 1 - slot)

    @pl.when(i < nt)
    def _():
        wait_gather(slot)

        @pl.when(i >= 2)
        def _():
            wait_scatter(slot)

        @pl.when((i == 0) | (te_ref[i] != te_ref[jnp.maximum(i - 1, 0)]))
        def _():
            wgu_bf[...] = wgu_ref[0].astype(BF16)
            wd_bf[...] = wd_ref[0].astype(BF16)

        xs = xbuf.at[slot]
        x = jnp.concatenate([xs[pl.ds(j, tm, stride=ROW_TILES), :] for j in range(ROW_TILES)], axis=1).astype(BF16)
        y = jnp.zeros((tm, D_MODEL), F32) + bd_ref[0]
        half = D_FF // 2
        for c in range(2):
            gate = _dot(x, wgu_bf[:, c * half:(c + 1) * half]) + bgu_ref[0, :, c * half:(c + 1) * half]
            up = (_dot(x, wgu_bf[:, D_FF + c * half:D_FF + (c + 1) * half])
                  + bgu_ref[0, :, D_FF + c * half:D_FF + (c + 1) * half])
            gate = jnp.minimum(gate, SWIGLU_LIMIT)
            up = jnp.clip(up, -SWIGLU_LIMIT, SWIGLU_LIMIT)
            act = (up + 1.0) * (gate * jax.nn.sigmoid(gate * SWIGLU_ALPHA))
            y = y + _dot(act.astype(BF16), wd_bf[c * half:(c + 1) * half, :])
        ys = ybuf.at[slot]
        for j in range(ROW_TILES):
            ys[pl.ds(j, tm, stride=ROW_TILES), :] = y[:, j * LANES:(j + 1) * LANES]

        def body(g, c):
            for u in range(DMA_UNROLL):
                r = g * DMA_UNROLL + u
                _row_copy_out(ybuf, yk_hbm, ssem, slot, dst_ref[0, 0, r], r).start()
            return c
        lax.fori_loop(0, tm // DMA_UNROLL, body, 0)

    @pl.when(i == pl.num_programs(0) - 1)
    def _():
        wait_scatter((nt - 1) % 2)

        @pl.when(nt >= 2)
        def _():
            wait_scatter(nt % 2)


def _experts(tile_expert, n_tiles, src_rows, dst_rows, hn_tiles, wgu, bgu, wd, bd, out_rows):
    nt_max = src_rows.shape[0]
    tm = TM_MOE
    smem = lambda f: pl.BlockSpec((1, 1, tm), f, memory_space=pltpu.SMEM)
    grid_spec = pltpu.PrefetchScalarGridSpec(
        num_scalar_prefetch=2,
        grid=(nt_max,),
        in_specs=[smem(lambda i, te, nt: (i, 0, 0)),
                  smem(lambda i, te, nt: (jnp.minimum(i + 1, nt_max - 1), 0, 0)),
                  smem(lambda i, te, nt: (i, 0, 0)),
                  pl.BlockSpec(memory_space=pl.ANY),
                  pl.BlockSpec((1, D_MODEL, 2 * D_FF), lambda i, te, nt: (te[i], 0, 0)),
                  pl.BlockSpec((1, 1, 2 * D_FF), lambda i, te, nt: (te[i], 0, 0)),
                  pl.BlockSpec((1, D_FF, D_MODEL), lambda i, te, nt: (te[i], 0, 0)),
                  pl.BlockSpec((1, 1, D_MODEL), lambda i, te, nt: (te[i], 0, 0))],
        out_specs=pl.BlockSpec(memory_space=pl.ANY),
        scratch_shapes=[pltpu.VMEM((2, tm * ROW_TILES, LANES), F32), pltpu.VMEM((2, tm * ROW_TILES, LANES), F32),
                        pltpu.VMEM((D_MODEL, 2 * D_FF), BF16), pltpu.VMEM((D_FF, D_MODEL), BF16),
                        pltpu.SemaphoreType.DMA((2,)), pltpu.SemaphoreType.DMA((2,))],
    )
    return pl.pallas_call(
        _moe_kernel,
        grid_spec=grid_spec,
        out_shape=jax.ShapeDtypeStruct((out_rows * ROW_TILES, LANES), F32),
        compiler_params=pltpu.CompilerParams(dimension_semantics=("arbitrary",), vmem_limit_bytes=VMEM_LIMIT),
        name="experts",
    )(tile_expert, n_tiles, src_rows, src_rows, dst_rows, hn_tiles, wgu, bgu, wd, bd)


def _combine_kernel(yk_ref, gate_ref, h_ref, g_ref, o_ref, acc_scr):
    tm = h_ref.shape[0]
    gates = gate_ref[...]
    gk = [jnp.broadcast_to(gates[:, kk:kk + 1], (tm, LANES)) for kk in range(TOP_K)]
    ss = jnp.zeros((tm, 1), F32)
    for j in range(ROW_TILES):
        cols = slice(j * LANES, (j + 1) * LANES)
        moe = gk[0] * yk_ref[pl.ds(j, tm, stride=TOP_K * ROW_TILES), :]
        for kk in range(1, TOP_K):
            moe = moe + gk[kk] * yk_ref[pl.ds(kk * ROW_TILES + j, tm, stride=TOP_K * ROW_TILES), :]
        acc = h_ref[:, cols] + moe
        acc_scr[:, cols] = acc
        ss = ss + jnp.sum(acc * acc, axis=-1, keepdims=True)
    inv = lax.rsqrt(ss / D_MODEL + NORM_EPS)
    o_ref[...] = acc_scr[...] * inv * g_ref[...]


def _combine(yk, gates, h, g):
    t = h.shape[0]
    tm = TM_COMB
    return pl.pallas_call(
        _combine_kernel,
        grid=(t // tm,),
        in_specs=[pl.BlockSpec((tm * TOP_K * ROW_TILES, LANES), lambda i: (i, 0)),
                  pl.BlockSpec((tm, LANES), lambda i: (i, 0)),
                  pl.BlockSpec((tm, D_MODEL), lambda i: (i, 0)),
                  pl.BlockSpec((1, D_MODEL), lambda i: (0, 0))],
        out_specs=pl.BlockSpec((tm, D_MODEL), lambda i: (i, 0)),
        out_shape=jax.ShapeDtypeStruct((t, D_MODEL), F32),
        scratch_shapes=[pltpu.VMEM((tm, D_MODEL), F32)],
        compiler_params=pltpu.CompilerParams(dimension_semantics=("arbitrary",), vmem_limit_bytes=VMEM_LIMIT),
        name="combine",
    )(yk, gates, h, g)


def _routing_plan(top_idx, n_tokens):
    tm = TM_MOE
    n_rows = n_tokens * TOP_K
    nt_max = n_rows // tm + N_EXPERTS
    flat_e = top_idx.reshape(-1)
    order = jnp.argsort(flat_e, stable=True).astype(I32)
    counts = jnp.sum((flat_e[:, None] == jnp.arange(N_EXPERTS, dtype=I32)[None, :]).astype(I32), axis=0)
    tiles_e = (counts + tm - 1) // tm
    tile_end = jnp.cumsum(tiles_e)
    tile_start = tile_end - tiles_e
    row_start = jnp.cumsum(counts) - counts
    n_tiles = tile_end[-1]
    tile_ids = jnp.arange(nt_max, dtype=I32)
    te = jnp.minimum(jnp.sum((tile_ids[:, None] >= tile_end[None, :]).astype(I32), axis=1), N_EXPERTS - 1)
    last_e = jnp.take(te, jnp.maximum(n_tiles - 1, 0))
    te = jnp.where(tile_ids < n_tiles, te, last_e).astype(I32)
    local = (tile_ids - jnp.take(tile_start, te))[:, None] * tm + jnp.arange(tm, dtype=I32)[None, :]
    valid = (local < jnp.take(counts, te)[:, None]) & (tile_ids < n_tiles)[:, None]
    sorted_pos = jnp.clip(jnp.take(row_start, te)[:, None] + local, 0, n_rows - 1)
    flat = jnp.take(order, sorted_pos)
    src = jnp.where(valid, flat // TOP_K, 0) * ROW_TILES
    dump = n_rows + jnp.arange(tm, dtype=I32)[None, :]
    dst = jnp.where(valid, flat, dump) * ROW_TILES
    return (te, n_tiles.reshape(1).astype(I32), src.reshape(nt_max, 1, tm).astype(I32),
            dst.reshape(nt_max, 1, tm).astype(I32))


def kernel(x, norm1_g, w_in, a_ln_g, a_ln_b, a_w_s, a_b_s, a_out_g, b_out_g, w_out, norm2_g, w_router,
           b_router, w_gate_up, b_gate_up, w_down, b_down, normf_g):
    batch, seq, _ = x.shape
    t = batch * seq
    assert seq % (TM_PROJ) == 0 and t % TM_MOE == 0 and seq // DILATIONS[-1] == BAND
    h = x.reshape(t, D_MODEL)

    pos = jnp.arange(seq, dtype=F32)
    inv = ROPE_THETA ** (-jnp.arange(0, HEAD_DIM, 2, dtype=F32) / HEAD_DIM)
    ang = pos[:, None] * inv[None, :]
    cos = jnp.tile(jnp.cos(ang), (1, 2 * LANES // HEAD_DIM))
    sin = jnp.tile(jnp.concatenate([-jnp.sin(ang), jnp.sin(ang)], axis=1), (1, LANES // HEAD_DIM))
    head_of_lane = np.arange(A_WIDTH) // HEAD_DIM
    avg = jnp.asarray((head_of_lane[:, None] == head_of_lane[None, :]).astype(np.float32) / HEAD_DIM, dtype=BF16)
    row2 = lambda v: v.reshape(1, -1).astype(F32)

    for layer in range(norm1_g.shape[0]):
        causal = np.tril(np.ones((CHUNK, CHUNK), dtype=bool))
        ws = jnp.where(causal[None], a_w_s[layer], 0.0).astype(BF16)
        bs = jnp.repeat(a_b_s[layer].astype(F32).T, HEAD_DIM, axis=1)
        a_out, *qkv = _inproj(h, row2(norm1_g[layer]), w_in[layer].astype(BF16), avg,
                              row2(a_ln_g[layer]), row2(a_ln_b[layer]), ws, bs, row2(a_out_g[layer]),
                              cos, sin, seq)
        b_mix = _attention(qkv, batch, seq)
        w_r = jnp.pad(w_router[layer], ((0, 0), (0, LANES - N_EXPERTS))).astype(BF16)
        b_r = jnp.concatenate([b_router[layer].astype(F32), jnp.full((LANES - N_EXPERTS,), NEG_INF, F32)])
        h_mid, hn, idx, gates = _mix(a_out, b_mix, h, row2(b_out_g[layer]), w_out[layer].astype(BF16),
                                     row2(norm2_g[layer]), w_r, b_r.reshape(1, LANES))
        te, n_tiles, src, dst = _routing_plan(idx[:, :TOP_K], t)
        yk = _experts(te, n_tiles, src, dst, hn,
                      w_gate_up[layer], b_gate_up[layer].reshape(N_EXPERTS, 1, 2 * D_FF),
                      w_down[layer], b_down[layer].reshape(N_EXPERTS, 1, D_MODEL),
                      t * TOP_K + TM_MOE)
        last = layer == norm1_g.shape[0] - 1
        assert last, "the combine kernel fuses the final norm; depth > 1 is not supported"
        h = _combine(yk, gates, h_mid, row2(normf_g))
    return h.reshape(batch, seq, D_MODEL)
```

```python
import functools
import math

import numpy as np
import jax
import jax.numpy as jnp
from jax import lax
from jax.experimental import pallas as pl
from jax.experimental.pallas import tpu as pltpu

F32 = jnp.float32
BF16 = jnp.bfloat16
I32 = jnp.int32

D_MODEL = 1024
HEAD_DIM = 64
A_WIDTH = 512
B_WIDTH = 512
CHUNK = 128
BAND = 128
DILATIONS = (1, 4, 16)
ROPE_THETA = 10000.0
N_EXPERTS = 32
TOP_K = 4
D_FF = 1024
SWIGLU_ALPHA = 1.702
SWIGLU_LIMIT = 7.0
NORM_EPS = 1e-5
NEG_INF = -1e30

LANES = 128
SUBLANES = 8
ROW_TILES = D_MODEL // LANES

TM_PROJ = 512
TM_MOE = 512
TM_COMB = 256
DMA_UNROLL = 16
MOE_FF_CHUNKS = 2
MOE_DMA_GROUP = TM_MOE // (3 * MOE_FF_CHUNKS + 2)
VMEM_LIMIT = 56 * 1024 * 1024


def _dot(a, b):
    return jnp.dot(a, b, preferred_element_type=F32)


def _gelu_tanh(x):
    c = math.sqrt(2.0 / math.pi)
    cdf = 0.5 * (1.0 + jnp.tanh(c * (x + 0.044715 * (x * x * x))))
    return x * cdf


def _rms(x, g):
    return x * lax.rsqrt(jnp.mean(x * x, axis=-1, keepdims=True) + NORM_EPS) * g


def _inproj_kernel(x_ref, g1_ref, w_ref, avg_ref, lng_ref, lnb_ref, ws_ref, bs_ref, aog_ref,
                   cos_ref, sin_ref, a_ref, q_ref, k_ref, v_ref, q4_ref, k4_ref, v4_ref,
                   q16_ref, k16_ref, v16_ref, a_scr, qkv_scr):
    tm = x_ref.shape[0]
    xn = _rms(x_ref[...], g1_ref[...]).astype(BF16)

    ug = _gelu_tanh(_dot(xn, w_ref[:, 0:A_WIDTH]))
    vg = _gelu_tanh(_dot(xn, w_ref[:, A_WIDTH:2 * A_WIDTH]))
    avg = avg_ref[...]
    mu = _dot(vg.astype(BF16), avg)
    d = vg - mu
    var = _dot((d * d).astype(BF16), avg)
    vn = (d * lax.rsqrt(var + NORM_EPS) * lng_ref[...] + lnb_ref[...]).astype(BF16)
    first_head = lax.broadcasted_iota(I32, (CHUNK, LANES), 1) < HEAD_DIM
    for c in range(tm // CHUNK):
        rows = slice(c * CHUNK, (c + 1) * CHUNK)
        for p in range(A_WIDTH // LANES):
            cols = slice(p * LANES, (p + 1) * LANES)
            slab = vn[rows, cols]
            g = jnp.where(first_head, _dot(ws_ref[2 * p], slab), _dot(ws_ref[2 * p + 1], slab))
            a_scr[rows, cols] = ug[rows, cols] * (g + bs_ref[:, cols])
    a_ref[...] = _rms(a_scr[...], aog_ref[...]).astype(BF16)

    cos = cos_ref[...]
    sin = sin_ref[...]
    first_half = (lax.broadcasted_iota(I32, (tm, LANES), 1) % HEAD_DIM) < (HEAD_DIM // 2)

    def rope(t):
        rot = jnp.where(first_half, pltpu.roll(t, LANES - HEAD_DIM // 2, 1), pltpu.roll(t, HEAD_DIM // 2, 1))
        return t * cos + rot * sin

    off = 2 * A_WIDTH
    nlt = B_WIDTH // LANES
    for p in range(nlt):
        cols = slice(p * LANES, (p + 1) * LANES)
        q = _dot(xn, w_ref[:, off + p * LANES: off + (p + 1) * LANES])
        qkv_scr[p] = rope(q) * (HEAD_DIM ** -0.5)
        k = _dot(xn, w_ref[:, off + B_WIDTH + p * LANES: off + B_WIDTH + (p + 1) * LANES])
        qkv_scr[nlt + p] = rope(k)
        qkv_scr[2 * nlt + p] = _dot(xn, w_ref[:, off + 2 * B_WIDTH + p * LANES: off + 2 * B_WIDTH + (p + 1) * LANES])

    for i, (nat_ref, c4_ref, c16_ref) in enumerate(((q_ref, q4_ref, q16_ref), (k_ref, k4_ref, k16_ref),
                                                    (v_ref, v4_ref, v16_ref))):
        for p in range(nlt):
            cols = slice(p * LANES, (p + 1) * LANES)
            nat_ref[:, cols] = qkv_scr[i * nlt + p].astype(BF16)
            for dil, c_ref in ((DILATIONS[1], c4_ref), (DILATIONS[2], c16_ref)):
                for r in range(dil):
                    c_ref[0, r, :, cols] = qkv_scr[i * nlt + p, pl.ds(r, tm // dil, stride=dil), :].astype(BF16)


def _inproj(x2, g1, w_in, avg, lng, lnb, ws, bs, aog, cos, sin, seq):
    t = x2.shape[0]
    tm = TM_PROJ
    nseq = seq // tm
    full = lambda shape: pl.BlockSpec(shape, lambda i: (0,) * len(shape))
    rows = lambda w: pl.BlockSpec((tm, w), lambda i: (i, 0))
    classes = lambda dil: pl.BlockSpec((1, dil, tm // dil, B_WIDTH), lambda i: (i // nseq, 0, i % nseq, 0))
    class_shape = lambda dil: jax.ShapeDtypeStruct((t // seq, dil, seq // dil, B_WIDTH), BF16)
    return pl.pallas_call(
        _inproj_kernel,
        grid=(t // tm,),
        in_specs=[rows(D_MODEL), full((1, D_MODEL)), full(w_in.shape), full(avg.shape),
                  full((1, A_WIDTH)), full((1, A_WIDTH)), full(ws.shape), full(bs.shape),
                  full((1, A_WIDTH)),
                  pl.BlockSpec((tm, LANES), lambda i: (i % nseq, 0)),
                  pl.BlockSpec((tm, LANES), lambda i: (i % nseq, 0))],
        out_specs=[rows(A_WIDTH)] + [rows(B_WIDTH)] * 3 + [classes(DILATIONS[1])] * 3 + [classes(DILATIONS[2])] * 3,
        out_shape=([jax.ShapeDtypeStruct((t, A_WIDTH), BF16)] + [jax.ShapeDtypeStruct((t, B_WIDTH), BF16)] * 3
                   + [class_shape(DILATIONS[1])] * 3 + [class_shape(DILATIONS[2])] * 3),
        scratch_shapes=[pltpu.VMEM((tm, A_WIDTH), F32), pltpu.VMEM((3 * B_WIDTH // LANES, tm, LANES), F32)],
        compiler_params=pltpu.CompilerParams(dimension_semantics=("arbitrary",), vmem_limit_bytes=VMEM_LIMIT),
        name="inproj",
    )(x2, g1, w_in, avg, lng, lnb, ws, bs, aog, cos, sin)


def _attn_block(qb, kw, vw, bias, first_head):
    zero = jnp.zeros_like(qb)
    q2 = jnp.concatenate([jnp.where(first_head, qb, zero), jnp.where(first_head, zero, qb)], axis=0)
    s = lax.dot_general(q2, kw, (((1,), (1,)), ((), ())), preferred_element_type=F32) + bias
    m = jnp.max(s, axis=-1, keepdims=True)
    p = jnp.exp(s - m)
    l = jnp.sum(p, axis=-1, keepdims=True)
    o = _dot(p.astype(BF16), vw) / l
    lse = jnp.broadcast_to(m + jnp.log(l), o.shape)
    return (jnp.where(first_head, o[:BAND], o[BAND:]),
            jnp.where(first_head, lse[:BAND], lse[BAND:]))


def _attn_kernel(q1, k1, v1, q4, k4, v4, q16, k16, v16, bias_band_ref, bias_first_ref, o_ref, o_scr, l_scr):
    seq = q1.shape[0]
    first_head = lax.broadcasted_iota(I32, (BAND, LANES), 1) < HEAD_DIM
    branches = ((q1, k1, v1), (q4, k4, v4), (q16, k16, v16))
    for bi, dil in enumerate(DILATIONS):
        q_r, k_r, v_r = branches[bi]
        length = seq // dil
        for r in range(dil):
            for n in range(length // BAND):
                lo = n * BAND
                if bi == 0:
                    ref_slice = lambda ref, a, b: ref[a:b, :]
                else:
                    ref_slice = lambda ref, a, b, r=r: ref[0, r, a:b, :]
                qb = ref_slice(q_r, lo, lo + BAND)
                if n == 0:
                    kw, vw, bias = ref_slice(k_r, 0, BAND), ref_slice(v_r, 0, BAND), bias_first_ref[...]
                else:
                    kw, vw = ref_slice(k_r, lo - BAND, lo + BAND), ref_slice(v_r, lo - BAND, lo + BAND)
                    bias = bias_band_ref[...]
                o, lse = _attn_block(qb, kw, vw, bias, first_head)
                if dil == 1:
                    o_scr[bi, lo:lo + BAND, :] = o
                    l_scr[bi, lo:lo + BAND, :] = lse
                else:
                    dst = pl.ds(r + dil * lo, BAND, stride=dil)
                    o_scr[bi, dst, :] = o
                    l_scr[bi, dst, :] = lse
    lses = [l_scr[i] for i in range(3)]
    m = jnp.maximum(jnp.maximum(lses[0], lses[1]), lses[2])
    es = [jnp.exp(l - m) for l in lses]
    den = es[0] + es[1] + es[2]
    o_ref[...] = (es[0] / den) * o_scr[0] + (es[1] / den) * o_scr[1] + (es[2] / den) * o_scr[2]


def _attention(qkv, batch, seq):
    rel = (np.arange(BAND)[:, None] + BAND) - np.arange(2 * BAND)[None, :]
    band = np.where((rel >= 0) & (rel <= BAND), 0.0, NEG_INF).astype(np.float32)
    bias_band = jnp.asarray(np.concatenate([band, band], axis=0))
    bias_first = jnp.asarray(np.concatenate([band[:, BAND:], band[:, BAND:]], axis=0))
    nat = pl.BlockSpec((seq, LANES), lambda b, p: (b, p))
    cls = lambda dil: pl.BlockSpec((1, dil, seq // dil, LANES), lambda b, p: (b, 0, 0, p))
    full = lambda a: pl.BlockSpec(a.shape, lambda b, p: (0, 0))
    return pl.pallas_call(
        _attn_kernel,
        grid=(batch, B_WIDTH // LANES),
        in_specs=[nat] * 3 + [cls(4)] * 3 + [cls(16)] * 3 + [full(bias_band), full(bias_first)],
        out_specs=pl.BlockSpec((seq, LANES), lambda b, p: (b, p)),
        out_shape=jax.ShapeDtypeStruct((batch * seq, B_WIDTH), F32),
        scratch_shapes=[pltpu.VMEM((3, seq, LANES), F32), pltpu.VMEM((3, seq, LANES), F32)],
        compiler_params=pltpu.CompilerParams(dimension_semantics=("arbitrary", "arbitrary"),
                                             vmem_limit_bytes=VMEM_LIMIT),
        name="dilated_attention",
    )(*qkv, bias_band, bias_first)


def _mix_kernel(a_ref, bm_ref, x_ref, bog_ref, wout_ref, n2g_ref, wr_ref, br_ref,
                h_ref, hn_ref, idx_ref, gate_ref):
    tm = x_ref.shape[0]
    bn = _rms(bm_ref[...], bog_ref[...]).astype(BF16)
    mixed = jnp.concatenate([a_ref[...], bn], axis=1)
    h = x_ref[...] + _dot(mixed, wout_ref[...])
    h_ref[...] = h
    hn = _rms(h, n2g_ref[...])
    for j in range(ROW_TILES):
        hn_ref[pl.ds(j, tm, stride=ROW_TILES), :] = hn[:, j * LANES:(j + 1) * LANES]
    logits = _dot(hn.astype(BF16), wr_ref[...]) + br_ref[...]
    lane = lax.broadcasted_iota(I32, (tm, LANES), 1)
    vals, idxs = [], []
    for _ in range(TOP_K):
        m = jnp.max(logits, axis=-1, keepdims=True)
        am = jnp.min(jnp.where(logits == m, lane, LANES), axis=-1, keepdims=True)
        vals.append(m)
        idxs.append(am)
        logits = jnp.where(lane == am, -jnp.inf, logits)
    es = [jnp.exp(v - vals[0]) for v in vals]
    den = es[0] + es[1] + es[2] + es[3]
    idx_out = jnp.zeros((tm, LANES), I32)
    gate_out = jnp.zeros((tm, LANES), F32)
    for kk in range(TOP_K):
        idx_out = jnp.where(lane == kk, idxs[kk], idx_out)
        gate_out = jnp.where(lane == kk, es[kk] / den, gate_out)
    idx_ref[...] = idx_out
    gate_ref[...] = gate_out


def _mix(a_out, b_mix, x2, bog, w_out, n2g, w_r, b_r):
    t = x2.shape[0]
    tm = TM_PROJ
    full = lambda shape: pl.BlockSpec(shape, lambda i: (0,) * len(shape))
    rows = lambda w: pl.BlockSpec((tm, w), lambda i: (i, 0))
    return pl.pallas_call(
        _mix_kernel,
        grid=(t // tm,),
        in_specs=[rows(A_WIDTH), rows(B_WIDTH), rows(D_MODEL), full((1, B_WIDTH)), full(w_out.shape),
                  full((1, D_MODEL)), full(w_r.shape), full((1, LANES))],
        out_specs=[rows(D_MODEL), pl.BlockSpec((tm * ROW_TILES, LANES), lambda i: (i, 0)), rows(LANES), rows(LANES)],
        out_shape=[jax.ShapeDtypeStruct((t, D_MODEL), F32), jax.ShapeDtypeStruct((t * ROW_TILES, LANES), F32),
                   jax.ShapeDtypeStruct((t, LANES), I32), jax.ShapeDtypeStruct((t, LANES), F32)],
        compiler_params=pltpu.CompilerParams(dimension_semantics=("arbitrary",), vmem_limit_bytes=VMEM_LIMIT),
        name="mix_router",
    )(a_out, b_mix, x2, bog, w_out, n2g, w_r, b_r)


def _row_in(hn_hbm, buf, sem, src_row, r):
    return pltpu.make_async_copy(hn_hbm.at[pl.ds(pl.multiple_of(src_row, SUBLANES), SUBLANES), :],
                                 buf.at[pl.ds(pl.multiple_of(r * SUBLANES, SUBLANES), SUBLANES), :], sem)


def _row_out(buf, yk_hbm, sem, dst_row, r):
    return pltpu.make_async_copy(buf.at[pl.ds(pl.multiple_of(r * SUBLANES, SUBLANES), SUBLANES), :],
                                 yk_hbm.at[pl.ds(pl.multiple_of(dst_row, SUBLANES), SUBLANES), :], sem)


def _moe_kernel(te_ref, nt_ref, src_first_ref, src_next_ref, dst_prev_ref, hn_hbm, wgu_ref, bgu_ref, wd_ref,
                bd_ref, yk_hbm, xbuf0, xbuf1, ybuf0, ybuf1, wgu_bf, wd_bf, act_scr, gsem, ssem):
    tm = TM_MOE
    tile_rows = tm * ROW_TILES
    i = pl.program_id(0)
    nt = nt_ref[0]
    xbufs = (xbuf0, xbuf1)
    ybufs = (ybuf0, ybuf1)

    def wait_gather(p):
        pltpu.make_async_copy(hn_hbm.at[pl.ds(0, tile_rows), :], xbufs[p], gsem.at[p]).wait()

    def wait_scatter(p):
        pltpu.make_async_copy(ybufs[p], yk_hbm.at[pl.ds(0, tile_rows), :], ssem.at[p]).wait()

    def issue_loop(start_row):
        def body(g, c):
            for u in range(DMA_UNROLL):
                start_row(g * DMA_UNROLL + u, u % 2)
            return c
        lax.fori_loop(0, tm // DMA_UNROLL, body, 0)

    @pl.when(i == 0)
    def _():
        issue_loop(lambda r, pr: _row_in(hn_hbm, xbuf0, gsem.at[0], src_first_ref[0, 0, r], r).start(priority=pr))
        ybuf1[...] = jnp.zeros((tile_rows, LANES), F32)
        spare = pltpu.make_async_copy(ybuf1, yk_hbm.at[pl.ds(yk_hbm.shape[0] - 2 * tile_rows, tile_rows), :],
                                      ssem.at[1])
        spare.start()
        spare.wait()

    for p in (0, 1):
        q = 1 - p

        @pl.when((i < nt) & (i % 2 == p))
        def _(p=p, q=q):
            wait_gather(p)

            @pl.when((i == 0) | (te_ref[i] != te_ref[jnp.maximum(i - 1, 0)]))
            def _():
                wgu_bf[...] = wgu_ref[0].astype(BF16)
                wd_bf[...] = wd_ref[0].astype(BF16)

            def dma_group(g):
                for r in range(g * MOE_DMA_GROUP, (g + 1) * MOE_DMA_GROUP):
                    _row_in(hn_hbm, xbufs[q], gsem.at[q], src_next_ref[0, 0, r], r).start(priority=r % 2)
                    _row_out(ybufs[q], yk_hbm, ssem.at[q], dst_prev_ref[0, 0, r], r).start(priority=(r + 1) % 2)

            xs = xbufs[p]
            x = jnp.concatenate([xs[pl.ds(j, tm, stride=ROW_TILES), :] for j in range(ROW_TILES)],
                                axis=1).astype(BF16)
            width = D_FF // MOE_FF_CHUNKS
            dma_group(0)
            group = 1
            for c in range(MOE_FF_CHUNKS):
                lo, hi = c * width, (c + 1) * width
                dma_group(group)
                gate = _dot(x, wgu_bf[:, lo:hi]) + bgu_ref[0, :, lo:hi]
                dma_group(group + 1)
                up = _dot(x, wgu_bf[:, D_FF + lo:D_FF + hi]) + bgu_ref[0, :, D_FF + lo:D_FF + hi]
                dma_group(group + 2)
                gate = jnp.minimum(gate, SWIGLU_LIMIT)
                up = jnp.clip(up, -SWIGLU_LIMIT, SWIGLU_LIMIT)
                act_scr[:, lo:hi] = ((up + 1.0) * (gate * jax.nn.sigmoid(gate * SWIGLU_ALPHA))).astype(BF16)
                group += 3
            dma_group(group)
            assert (group + 1) * MOE_DMA_GROUP == tm
            y = _dot(act_scr[...], wd_bf[...]) + bd_ref[0]

            @pl.when(i >= 1)
            def _():
                wait_scatter(p)
            ys = ybufs[p]
            for j in range(ROW_TILES):
                ys[pl.ds(j, tm, stride=ROW_TILES), :] = y[:, j * LANES:(j + 1) * LANES]

    for p in (0, 1):
        q = 1 - p

        @pl.when((i == nt) & (i % 2 == p))
        def _(p=p, q=q):
            wait_gather(p)
            issue_loop(lambda r, pr: _row_out(ybufs[q], yk_hbm, ssem.at[q], dst_prev_ref[0, 0, r], r)
                       .start(priority=pr))
            wait_scatter(q)
            wait_scatter(p)


def _experts(tile_expert, n_tiles, src_rows, dst_prev_rows, hn_tiles, wgu, bgu, wd, bd, out_rows):
    nt_max = src_rows.shape[0]
    tm = TM_MOE
    smem = lambda f: pl.BlockSpec((1, 1, tm), f, memory_space=pltpu.SMEM)
    row_buf = pltpu.VMEM((tm * ROW_TILES, LANES), F32)
    grid_spec = pltpu.PrefetchScalarGridSpec(
        num_scalar_prefetch=2,
        grid=(nt_max,),
        in_specs=[smem(lambda i, te, nt: (0, 0, 0)),
                  smem(lambda i, te, nt: (jnp.minimum(i + 1, nt_max - 1), 0, 0)),
                  smem(lambda i, te, nt: (i, 0, 0)),
                  pl.BlockSpec(memory_space=pl.ANY),
                  pl.BlockSpec((1, D_MODEL, 2 * D_FF), lambda i, te, nt: (te[i], 0, 0)),
                  pl.BlockSpec((1, 1, 2 * D_FF), lambda i, te, nt: (te[i], 0, 0)),
                  pl.BlockSpec((1, D_FF, D_MODEL), lambda i, te, nt: (te[i], 0, 0)),
                  pl.BlockSpec((1, 1, D_MODEL), lambda i, te, nt: (te[i], 0, 0))],
        out_specs=pl.BlockSpec(memory_space=pl.ANY),
        scratch_shapes=[row_buf, row_buf, row_buf, row_buf,
                        pltpu.VMEM((D_MODEL, 2 * D_FF), BF16), pltpu.VMEM((D_FF, D_MODEL), BF16),
                        pltpu.VMEM((tm, D_FF), BF16),
                        pltpu.SemaphoreType.DMA((2,)), pltpu.SemaphoreType.DMA((2,))],
    )
    return pl.pallas_call(
        _moe_kernel,
        grid_spec=grid_spec,
        out_shape=jax.ShapeDtypeStruct((out_rows * ROW_TILES, LANES), F32),
        compiler_params=pltpu.CompilerParams(dimension_semantics=("arbitrary",), vmem_limit_bytes=VMEM_LIMIT),
        name="experts",
    )(tile_expert, n_tiles, src_rows, src_rows, dst_prev_rows, hn_tiles, wgu, bgu, wd, bd)


def _combine_kernel(yk_ref, gate_ref, h_ref, g_ref, o_ref, acc_scr):
    tm = h_ref.shape[0]
    gates = gate_ref[...]
    gk = [jnp.broadcast_to(gates[:, kk:kk + 1], (tm, LANES)) for kk in range(TOP_K)]
    ss = jnp.zeros((tm, 1), F32)
    for j in range(ROW_TILES):
        cols = slice(j * LANES, (j + 1) * LANES)
        moe = gk[0] * yk_ref[pl.ds(j, tm, stride=TOP_K * ROW_TILES), :]
        for kk in range(1, TOP_K):
            moe = moe + gk[kk] * yk_ref[pl.ds(kk * ROW_TILES + j, tm, stride=TOP_K * ROW_TILES), :]
        acc = h_ref[:, cols] + moe
        acc_scr[:, cols] = acc
        ss = ss + jnp.sum(acc * acc, axis=-1, keepdims=True)
    inv = lax.rsqrt(ss / D_MODEL + NORM_EPS)
    o_ref[...] = acc_scr[...] * inv * g_ref[...]


def _combine(yk, gates, h, g):
    t = h.shape[0]
    tm = TM_COMB
    return pl.pallas_call(
        _combine_kernel,
        grid=(t // tm,),
        in_specs=[pl.BlockSpec((tm * TOP_K * ROW_TILES, LANES), lambda i: (i, 0)),
                  pl.BlockSpec((tm, LANES), lambda i: (i, 0)),
                  pl.BlockSpec((tm, D_MODEL), lambda i: (i, 0)),
                  pl.BlockSpec((1, D_MODEL), lambda i: (0, 0))],
        out_specs=pl.BlockSpec((tm, D_MODEL), lambda i: (i, 0)),
        out_shape=jax.ShapeDtypeStruct((t, D_MODEL), F32),
        scratch_shapes=[pltpu.VMEM((tm, D_MODEL), F32)],
        compiler_params=pltpu.CompilerParams(dimension_semantics=("arbitrary",), vmem_limit_bytes=VMEM_LIMIT),
        name="combine",
    )(yk, gates, h, g)


def _routing_plan(top_idx, n_tokens):
    tm = TM_MOE
    n_rows = n_tokens * TOP_K
    nt_max = n_rows // tm + N_EXPERTS + 1
    flat_e = top_idx.reshape(-1)
    order = jnp.argsort(flat_e, stable=True).astype(I32)
    counts = jnp.sum((flat_e[:, None] == jnp.arange(N_EXPERTS, dtype=I32)[None, :]).astype(I32), axis=0)
    tiles_e = (counts + tm - 1) // tm
    tile_end = jnp.cumsum(tiles_e)
    tile_start = tile_end - tiles_e
    row_start = jnp.cumsum(counts) - counts
    n_tiles = tile_end[-1]
    tile_ids = jnp.arange(nt_max, dtype=I32)
    te = jnp.minimum(jnp.sum((tile_ids[:, None] >= tile_end[None, :]).astype(I32), axis=1), N_EXPERTS - 1)
    last_e = jnp.take(te, jnp.maximum(n_tiles - 1, 0))
    te = jnp.where(tile_ids < n_tiles, te, last_e).astype(I32)
    lane = jnp.arange(tm, dtype=I32)[None, :]
    local = (tile_ids - jnp.take(tile_start, te))[:, None] * tm + lane
    valid = (local < jnp.take(counts, te)[:, None]) & (tile_ids < n_tiles)[:, None]
    sorted_pos = jnp.clip(jnp.take(row_start, te)[:, None] + local, 0, n_rows - 1)
    flat = jnp.take(order, sorted_pos)
    src = jnp.where(valid, flat // TOP_K, 0) * ROW_TILES
    spare = n_rows + (tile_ids % 2)[:, None] * tm + lane
    dst = jnp.where(valid, flat, spare) * ROW_TILES
    zero_tile = (n_rows + tm + lane) * ROW_TILES
    dst_prev = jnp.concatenate([zero_tile, dst[:-1]], axis=0)
    return (te, n_tiles.reshape(1).astype(I32), src.reshape(nt_max, 1, tm).astype(I32),
            dst_prev.reshape(nt_max, 1, tm).astype(I32))


def kernel(x, norm1_g, w_in, a_ln_g, a_ln_b, a_w_s, a_b_s, a_out_g, b_out_g, w_out, norm2_g, w_router,
           b_router, w_gate_up, b_gate_up, w_down, b_down, normf_g):
    batch, seq, _ = x.shape
    t = batch * seq
    assert seq % (TM_PROJ) == 0 and t % TM_MOE == 0 and seq // DILATIONS[-1] == BAND
    h = x.reshape(t, D_MODEL)

    pos = jnp.arange(seq, dtype=F32)
    inv = ROPE_THETA ** (-jnp.arange(0, HEAD_DIM, 2, dtype=F32) / HEAD_DIM)
    ang = pos[:, None] * inv[None, :]
    cos = jnp.tile(jnp.cos(ang), (1, 2 * LANES // HEAD_DIM))
    sin = jnp.tile(jnp.concatenate([-jnp.sin(ang), jnp.sin(ang)], axis=1), (1, LANES // HEAD_DIM))
    head_of_lane = np.arange(A_WIDTH) // HEAD_DIM
    avg = jnp.asarray((head_of_lane[:, None] == head_of_lane[None, :]).astype(np.float32) / HEAD_DIM, dtype=BF16)
    row2 = lambda v: v.reshape(1, -1).astype(F32)

    for layer in range(norm1_g.shape[0]):
        causal = np.tril(np.ones((CHUNK, CHUNK), dtype=bool))
        ws = jnp.where(causal[None], a_w_s[layer], 0.0).astype(BF16)
        bs = jnp.repeat(a_b_s[layer].astype(F32).T, HEAD_DIM, axis=1)
        a_out, *qkv = _inproj(h, row2(norm1_g[layer]), w_in[layer].astype(BF16), avg,
                              row2(a_ln_g[layer]), row2(a_ln_b[layer]), ws, bs, row2(a_out_g[layer]),
                              cos, sin, seq)
        b_mix = _attention(qkv, batch, seq)
        w_r = jnp.pad(w_router[layer], ((0, 0), (0, LANES - N_EXPERTS))).astype(BF16)
        b_r = jnp.concatenate([b_router[layer].astype(F32), jnp.full((LANES - N_EXPERTS,), NEG_INF, F32)])
        h_mid, hn, idx, gates = _mix(a_out, b_mix, h, row2(b_out_g[layer]), w_out[layer].astype(BF16),
                                     row2(norm2_g[layer]), w_r, b_r.reshape(1, LANES))
        te, n_tiles, src, dst = _routing_plan(idx[:, :TOP_K], t)
        yk = _experts(te, n_tiles, src, dst, hn,
                      w_gate_up[layer], b_gate_up[layer].reshape(N_EXPERTS, 1, 2 * D_FF),
                      w_down[layer], b_down[layer].reshape(N_EXPERTS, 1, D_MODEL),
                      t * TOP_K + 2 * TM_MOE)
        last = layer == norm1_g.shape[0] - 1
        assert last, "the combine kernel fuses the final norm; depth > 1 is not supported"
        h = _combine(yk, gates, h_mid, row2(normf_g))
    return h.reshape(batch, seq, D_MODEL)
```

```python
import functools
import math

import numpy as np
import jax
import jax.numpy as jnp
from jax import lax
from jax.experimental import pallas as pl
from jax.experimental.pallas import tpu as pltpu

F32 = jnp.float32
BF16 = jnp.bfloat16
I32 = jnp.int32

D_MODEL = 1024
HEAD_DIM = 64
A_WIDTH = 512
B_WIDTH = 512
CHUNK = 128
BAND = 128
DILATIONS = (1, 4, 16)
ROPE_THETA = 10000.0
N_EXPERTS = 32
TOP_K = 4
D_FF = 1024
SWIGLU_ALPHA = 1.702
SWIGLU_LIMIT = 7.0
NORM_EPS = 1e-5
NEG_INF = -1e30

LANES = 128
SUBLANES = 8
ROW_TILES = D_MODEL // LANES

TM_PROJ = 512
TM_MOE = 512
TM_COMB = 256
DMA_UNROLL = 16
MOE_FF_CHUNKS = 2
MOE_DMA_GROUP = TM_MOE // (3 * MOE_FF_CHUNKS + 2)
VMEM_LIMIT = 56 * 1024 * 1024


def _dot(a, b):
    return jnp.dot(a, b, preferred_element_type=F32)


def _gelu_tanh(x):
    c = math.sqrt(2.0 / math.pi)
    cdf = 0.5 * (1.0 + jnp.tanh(c * (x + 0.044715 * (x * x * x))))
    return x * cdf


def _rms(x, g):
    return x * lax.rsqrt(jnp.mean(x * x, axis=-1, keepdims=True) + NORM_EPS) * g


def _inproj_kernel(x_ref, g1_ref, w_ref, avg_ref, lng_ref, lnb_ref, ws_ref, bs_ref, aog_ref,
                   cos_ref, sin_ref, a_ref, q_ref, k_ref, v_ref, q4_ref, k4_ref, v4_ref,
                   q16_ref, k16_ref, v16_ref, a_scr, qkv_scr):
    tm = x_ref.shape[0]
    xn = _rms(x_ref[...], g1_ref[...]).astype(BF16)

    ug = _gelu_tanh(_dot(xn, w_ref[:, 0:A_WIDTH]))
    vg = _gelu_tanh(_dot(xn, w_ref[:, A_WIDTH:2 * A_WIDTH]))
    avg = avg_ref[...]
    mu = _dot(vg.astype(BF16), avg)
    d = vg - mu
    var = _dot((d * d).astype(BF16), avg)
    vn = (d * lax.rsqrt(var + NORM_EPS) * lng_ref[...] + lnb_ref[...]).astype(BF16)
    first_head = lax.broadcasted_iota(I32, (CHUNK, LANES), 1) < HEAD_DIM
    for c in range(tm // CHUNK):
        rows = slice(c * CHUNK, (c + 1) * CHUNK)
        for p in range(A_WIDTH // LANES):
            cols = slice(p * LANES, (p + 1) * LANES)
            slab = vn[rows, cols]
            g = jnp.where(first_head, _dot(ws_ref[2 * p], slab), _dot(ws_ref[2 * p + 1], slab))
            a_scr[rows, cols] = ug[rows, cols] * (g + bs_ref[:, cols])
    a_ref[...] = _rms(a_scr[...], aog_ref[...]).astype(BF16)

    cos = cos_ref[...]
    sin = sin_ref[...]
    first_half = (lax.broadcasted_iota(I32, (tm, LANES), 1) % HEAD_DIM) < (HEAD_DIM // 2)

    def rope(t):
        rot = jnp.where(first_half, pltpu.roll(t, LANES - HEAD_DIM // 2, 1), pltpu.roll(t, HEAD_DIM // 2, 1))
        return t * cos + rot * sin

    off = 2 * A_WIDTH
    nlt = B_WIDTH // LANES
    for p in range(nlt):
        cols = slice(p * LANES, (p + 1) * LANES)
        q = _dot(xn, w_ref[:, off + p * LANES: off + (p + 1) * LANES])
        qkv_scr[p] = rope(q) * (HEAD_DIM ** -0.5)
        k = _dot(xn, w_ref[:, off + B_WIDTH + p * LANES: off + B_WIDTH + (p + 1) * LANES])
        qkv_scr[nlt + p] = rope(k)
        qkv_scr[2 * nlt + p] = _dot(xn, w_ref[:, off + 2 * B_WIDTH + p * LANES: off + 2 * B_WIDTH + (p + 1) * LANES])

    for i, (nat_ref, c4_ref, c16_ref) in enumerate(((q_ref, q4_ref, q16_ref), (k_ref, k4_ref, k16_ref),
                                                    (v_ref, v4_ref, v16_ref))):
        for p in range(nlt):
            cols = slice(p * LANES, (p + 1) * LANES)
            nat_ref[:, cols] = qkv_scr[i * nlt + p].astype(BF16)
            for dil, c_ref in ((DILATIONS[1], c4_ref), (DILATIONS[2], c16_ref)):
                for r in range(dil):
                    c_ref[0, r, :, cols] = qkv_scr[i * nlt + p, pl.ds(r, tm // dil, stride=dil), :].astype(BF16)


def _inproj(x2, g1, w_in, avg, lng, lnb, ws, bs, aog, cos, sin, seq):
    t = x2.shape[0]
    tm = TM_PROJ
    nseq = seq // tm
    full = lambda shape: pl.BlockSpec(shape, lambda i: (0,) * len(shape))
    rows = lambda w: pl.BlockSpec((tm, w), lambda i: (i, 0))
    classes = lambda dil: pl.BlockSpec((1, dil, tm // dil, B_WIDTH), lambda i: (i // nseq, 0, i % nseq, 0))
    class_shape = lambda dil: jax.ShapeDtypeStruct((t // seq, dil, seq // dil, B_WIDTH), BF16)
    return pl.pallas_call(
        _inproj_kernel,
        grid=(t // tm,),
        in_specs=[rows(D_MODEL), full((1, D_MODEL)), full(w_in.shape), full(avg.shape),
                  full((1, A_WIDTH)), full((1, A_WIDTH)), full(ws.shape), full(bs.shape),
                  full((1, A_WIDTH)),
                  pl.BlockSpec((tm, LANES), lambda i: (i % nseq, 0)),
                  pl.BlockSpec((tm, LANES), lambda i: (i % nseq, 0))],
        out_specs=[rows(A_WIDTH)] + [rows(B_WIDTH)] * 3 + [classes(DILATIONS[1])] * 3 + [classes(DILATIONS[2])] * 3,
        out_shape=([jax.ShapeDtypeStruct((t, A_WIDTH), BF16)] + [jax.ShapeDtypeStruct((t, B_WIDTH), BF16)] * 3
                   + [class_shape(DILATIONS[1])] * 3 + [class_shape(DILATIONS[2])] * 3),
        scratch_shapes=[pltpu.VMEM((tm, A_WIDTH), F32), pltpu.VMEM((3 * B_WIDTH // LANES, tm, LANES), F32)],
        compiler_params=pltpu.CompilerParams(dimension_semantics=("arbitrary",), vmem_limit_bytes=VMEM_LIMIT),
        name="inproj",
    )(x2, g1, w_in, avg, lng, lnb, ws, bs, aog, cos, sin)


def _attn_block(qb, kw, vw, bias, first_head):
    zero = jnp.zeros_like(qb)
    q2 = jnp.concatenate([jnp.where(first_head, qb, zero), jnp.where(first_head, zero, qb)], axis=0)
    s = lax.dot_general(q2, kw, (((1,), (1,)), ((), ())), preferred_element_type=F32) + bias
    m = jnp.max(s, axis=-1, keepdims=True)
    p = jnp.exp(s - m)
    l = jnp.sum(p, axis=-1, keepdims=True)
    o = _dot(p.astype(BF16), vw) / l
    lse = jnp.broadcast_to(m + jnp.log(l), o.shape)
    return (jnp.where(first_head, o[:BAND], o[BAND:]),
            jnp.where(first_head, lse[:BAND], lse[BAND:]))


def _attn_kernel(q1, k1, v1, q4, k4, v4, q16, k16, v16, bias_band_ref, bias_first_ref, o_ref, o_scr, l_scr):
    seq = q1.shape[0]
    first_head = lax.broadcasted_iota(I32, (BAND, LANES), 1) < HEAD_DIM
    branches = ((q1, k1, v1), (q4, k4, v4), (q16, k16, v16))
    for bi, dil in enumerate(DILATIONS):
        q_r, k_r, v_r = branches[bi]
        length = seq // dil
        for r in range(dil):
            for n in range(length // BAND):
                lo = n * BAND
                if bi == 0:
                    ref_slice = lambda ref, a, b: ref[a:b, :]
                else:
                    ref_slice = lambda ref, a, b, r=r: ref[0, r, a:b, :]
                qb = ref_slice(q_r, lo, lo + BAND)
                if n == 0:
                    kw, vw, bias = ref_slice(k_r, 0, BAND), ref_slice(v_r, 0, BAND), bias_first_ref[...]
                else:
                    kw, vw = ref_slice(k_r, lo - BAND, lo + BAND), ref_slice(v_r, lo - BAND, lo + BAND)
                    bias = bias_band_ref[...]
                o, lse = _attn_block(qb, kw, vw, bias, first_head)
                if dil == 1:
                    o_scr[bi, lo:lo + BAND, :] = o
                    l_scr[bi, lo:lo + BAND, :] = lse
                else:
                    dst = pl.ds(r + dil * lo, BAND, stride=dil)
                    o_scr[bi, dst, :] = o
                    l_scr[bi, dst, :] = lse
    lses = [l_scr[i] for i in range(3)]
    m = jnp.maximum(jnp.maximum(lses[0], lses[1]), lses[2])
    es = [jnp.exp(l - m) for l in lses]
    den = es[0] + es[1] + es[2]
    o_ref[...] = (es[0] / den) * o_scr[0] + (es[1] / den) * o_scr[1] + (es[2] / den) * o_scr[2]


def _attention(qkv, batch, seq):
    rel = (np.arange(BAND)[:, None] + BAND) - np.arange(2 * BAND)[None, :]
    band = np.where((rel >= 0) & (rel <= BAND), 0.0, NEG_INF).astype(np.float32)
    bias_band = jnp.asarray(np.concatenate([band, band], axis=0))
    bias_first = jnp.asarray(np.concatenate([band[:, BAND:], band[:, BAND:]], axis=0))
    nat = pl.BlockSpec((seq, LANES), lambda b, p: (b, p))
    cls = lambda dil: pl.BlockSpec((1, dil, seq // dil, LANES), lambda b, p: (b, 0, 0, p))
    full = lambda a: pl.BlockSpec(a.shape, lambda b, p: (0, 0))
    return pl.pallas_call(
        _attn_kernel,
        grid=(batch, B_WIDTH // LANES),
        in_specs=[nat] * 3 + [cls(4)] * 3 + [cls(16)] * 3 + [full(bias_band), full(bias_first)],
        out_specs=pl.BlockSpec((seq, LANES), lambda b, p: (b, p)),
        out_shape=jax.ShapeDtypeStruct((batch * seq, B_WIDTH), F32),
        scratch_shapes=[pltpu.VMEM((3, seq, LANES), F32), pltpu.VMEM((3, seq, LANES), F32)],
        compiler_params=pltpu.CompilerParams(dimension_semantics=("arbitrary", "arbitrary"),
                                             vmem_limit_bytes=VMEM_LIMIT),
        name="dilated_attention",
    )(*qkv, bias_band, bias_first)


def _mix_kernel(a_ref, bm_ref, x_ref, bog_ref, wout_ref, n2g_ref, wr_ref, br_ref,
                h_ref, hn_ref, idx_ref, gate_ref):
    tm = x_ref.shape[0]
    bn = _rms(bm_ref[...], bog_ref[...]).astype(BF16)
    mixed = jnp.concatenate([a_ref[...], bn], axis=1)
    h = x_ref[...] + _dot(mixed, wout_ref[...])
    h_ref[...] = h
    hn = _rms(h, n2g_ref[...])
    for j in range(ROW_TILES):
        hn_ref[pl.ds(j, tm, stride=ROW_TILES), :] = hn[:, j * LANES:(j + 1) * LANES]
    logits = _dot(hn.astype(BF16), wr_ref[...]) + br_ref[...]
    lane = lax.broadcasted_iota(I32, (tm, LANES), 1)
    vals, idxs = [], []
    for _ in range(TOP_K):
        m = jnp.max(logits, axis=-1, keepdims=True)
        am = jnp.min(jnp.where(logits == m, lane, LANES), axis=-1, keepdims=True)
        vals.append(m)
        idxs.append(am)
        logits = jnp.where(lane == am, -jnp.inf, logits)
    es = [jnp.exp(v - vals[0]) for v in vals]
    den = es[0] + es[1] + es[2] + es[3]
    idx_out = jnp.zeros((tm, LANES), I32)
    gate_out = jnp.zeros((tm, LANES), F32)
    for kk in range(TOP_K):
        idx_out = jnp.where(lane == kk, idxs[kk], idx_out)
        gate_out = jnp.where(lane == kk, es[kk] / den, gate_out)
    idx_ref[...] = idx_out
    gate_ref[...] = gate_out


def _mix(a_out, b_mix, x2, bog, w_out, n2g, w_r, b_r):
    t = x2.shape[0]
    tm = TM_PROJ
    full = lambda shape: pl.BlockSpec(shape, lambda i: (0,) * len(shape))
    rows = lambda w: pl.BlockSpec((tm, w), lambda i: (i, 0))
    return pl.pallas_call(
        _mix_kernel,
        grid=(t // tm,),
        in_specs=[rows(A_WIDTH), rows(B_WIDTH), rows(D_MODEL), full((1, B_WIDTH)), full(w_out.shape),
                  full((1, D_MODEL)), full(w_r.shape), full((1, LANES))],
        out_specs=[rows(D_MODEL), pl.BlockSpec((tm * ROW_TILES, LANES), lambda i: (i, 0)), rows(LANES), rows(LANES)],
        out_shape=[jax.ShapeDtypeStruct((t, D_MODEL), F32), jax.ShapeDtypeStruct((t * ROW_TILES, LANES), F32),
                   jax.ShapeDtypeStruct((t, LANES), I32), jax.ShapeDtypeStruct((t, LANES), F32)],
        compiler_params=pltpu.CompilerParams(dimension_semantics=("arbitrary",), vmem_limit_bytes=VMEM_LIMIT),
        name="mix_router",
    )(a_out, b_mix, x2, bog, w_out, n2g, w_r, b_r)


def _row_in(hn_hbm, buf, sem, src_row, r):
    return pltpu.make_async_copy(hn_hbm.at[pl.ds(pl.multiple_of(src_row, SUBLANES), SUBLANES), :],
                                 buf.at[pl.ds(pl.multiple_of(r * SUBLANES, SUBLANES), SUBLANES), :], sem)


def _row_out(buf, yk_hbm, sem, dst_row, r):
    return pltpu.make_async_copy(buf.at[pl.ds(pl.multiple_of(r * SUBLANES, SUBLANES), SUBLANES), :],
                                 yk_hbm.at[pl.ds(pl.multiple_of(dst_row, SUBLANES), SUBLANES), :], sem)


def _moe_kernel(te_ref, nt_ref, src_first_ref, src_next_ref, dst_prev_ref, hn_hbm, wgu_ref, bgu_ref, wd_ref,
                bd_ref, yk_hbm, xbuf0, xbuf1, ybuf0, ybuf1, wgu_bf, wd_bf, act_scr, gsem, ssem):
    tm = TM_MOE
    tile_rows = tm * ROW_TILES
    i = pl.program_id(0)
    nt = nt_ref[0]
    xbufs = (xbuf0, xbuf1)
    ybufs = (ybuf0, ybuf1)

    def wait_gather(p):
        pltpu.make_async_copy(hn_hbm.at[pl.ds(0, tile_rows), :], xbufs[p], gsem.at[p]).wait()

    def wait_scatter(p):
        pltpu.make_async_copy(ybufs[p], yk_hbm.at[pl.ds(0, tile_rows), :], ssem.at[p]).wait()

    def issue_loop(start_row):
        def body(g, c):
            for u in range(DMA_UNROLL):
                start_row(g * DMA_UNROLL + u, u % 2)
            return c
        lax.fori_loop(0, tm // DMA_UNROLL, body, 0)

    @pl.when(i == 0)
    def _():
        issue_loop(lambda r, pr: _row_in(hn_hbm, xbuf0, gsem.at[0], src_first_ref[0, 0, r], r).start(priority=pr))
        ybuf1[...] = jnp.zeros((tile_rows, LANES), F32)
        spare = pltpu.make_async_copy(ybuf1, yk_hbm.at[pl.ds(yk_hbm.shape[0] - 2 * tile_rows, tile_rows), :],
                                      ssem.at[1])
        spare.start()
        spare.wait()

    for p in (0, 1):
        q = 1 - p

        @pl.when((i < nt) & (i % 2 == p))
        def _(p=p, q=q):
            wait_gather(p)

            @pl.when((i == 0) | (te_ref[i] != te_ref[jnp.maximum(i - 1, 0)]))
            def _():
                wgu_bf[...] = wgu_ref[0].astype(BF16)
                wd_bf[...] = wd_ref[0].astype(BF16)

            def dma_group(g):
                for r in range(g * MOE_DMA_GROUP, (g + 1) * MOE_DMA_GROUP):
                    _row_in(hn_hbm, xbufs[q], gsem.at[q], src_next_ref[0, 0, r], r).start(priority=r % 2)
                    _row_out(ybufs[q], yk_hbm, ssem.at[q], dst_prev_ref[0, 0, r], r).start(priority=(r + 1) % 2)

            xs = xbufs[p]
            x = jnp.concatenate([xs[pl.ds(j, tm, stride=ROW_TILES), :] for j in range(ROW_TILES)],
                                axis=1).astype(BF16)
            width = D_FF // MOE_FF_CHUNKS
            dma_group(0)
            group = 1
            for c in range(MOE_FF_CHUNKS):
                lo, hi = c * width, (c + 1) * width
                dma_group(group)
                gate = _dot(x, wgu_bf[:, lo:hi]) + bgu_ref[0, :, lo:hi]
                dma_group(group + 1)
                up = _dot(x, wgu_bf[:, D_FF + lo:D_FF + hi]) + bgu_ref[0, :, D_FF + lo:D_FF + hi]
                dma_group(group + 2)
                gate = jnp.minimum(gate, SWIGLU_LIMIT)
                up = jnp.clip(up, -SWIGLU_LIMIT, SWIGLU_LIMIT)
                act_scr[:, lo:hi] = ((up + 1.0) * (gate * jax.nn.sigmoid(gate * SWIGLU_ALPHA))).astype(BF16)
                group += 3
            dma_group(group)
            assert (group + 1) * MOE_DMA_GROUP == tm
            y = _dot(act_scr[...], wd_bf[...]) + bd_ref[0]

            @pl.when(i >= 1)
            def _():
                wait_scatter(p)
            ys = ybufs[p]
            for j in range(ROW_TILES):
                ys[pl.ds(j, tm, stride=ROW_TILES), :] = y[:, j * LANES:(j + 1) * LANES]

    for p in (0, 1):
        q = 1 - p

        @pl.when((i == nt) & (i % 2 == p))
        def _(p=p, q=q):
            wait_gather(p)
            issue_loop(lambda r, pr: _row_out(ybufs[q], yk_hbm, ssem.at[q], dst_prev_ref[0, 0, r], r)
                       .start(priority=pr))
            wait_scatter(q)
            wait_scatter(p)


def _experts(tile_expert, n_tiles, src_rows, dst_prev_rows, hn_tiles, wgu, bgu, wd, bd, out_rows):
    nt_max = src_rows.shape[0]
    tm = TM_MOE
    smem = lambda f: pl.BlockSpec((1, 1, tm), f, memory_space=pltpu.SMEM)
    row_buf = pltpu.VMEM((tm * ROW_TILES, LANES), F32)
    grid_spec = pltpu.PrefetchScalarGridSpec(
        num_scalar_prefetch=2,
        grid=(nt_max,),
        in_specs=[smem(lambda i, te, nt: (0, 0, 0)),
                  smem(lambda i, te, nt: (jnp.minimum(i + 1, nt_max - 1), 0, 0)),
                  smem(lambda i, te, nt: (i, 0, 0)),
                  pl.BlockSpec(memory_space=pl.ANY),
                  pl.BlockSpec((1, D_MODEL, 2 * D_FF), lambda i, te, nt: (te[i], 0, 0)),
                  pl.BlockSpec((1, 1, 2 * D_FF), lambda i, te, nt: (te[i], 0, 0)),
                  pl.BlockSpec((1, D_FF, D_MODEL), lambda i, te, nt: (te[i], 0, 0)),
                  pl.BlockSpec((1, 1, D_MODEL), lambda i, te, nt: (te[i], 0, 0))],
        out_specs=pl.BlockSpec(memory_space=pl.ANY),
        scratch_shapes=[row_buf, row_buf, row_buf, row_buf,
                        pltpu.VMEM((D_MODEL, 2 * D_FF), BF16), pltpu.VMEM((D_FF, D_MODEL), BF16),
                        pltpu.VMEM((tm, D_FF), BF16),
                        pltpu.SemaphoreType.DMA((2,)), pltpu.SemaphoreType.DMA((2,))],
    )
    return pl.pallas_call(
        _moe_kernel,
        grid_spec=grid_spec,
        out_shape=jax.ShapeDtypeStruct((out_rows * ROW_TILES, LANES), F32),
        compiler_params=pltpu.CompilerParams(dimension_semantics=("arbitrary",), vmem_limit_bytes=VMEM_LIMIT),
        name="experts",
    )(tile_expert, n_tiles, src_rows, src_rows, dst_prev_rows, hn_tiles, wgu, bgu, wd, bd)


def _combine_kernel(yk0_ref, yk1_ref, yk2_ref, yk3_ref, gate_ref, h_ref, g_ref, o_ref, acc_scr):
    tm = h_ref.shape[0]
    yk_refs = (yk0_ref, yk1_ref, yk2_ref, yk3_ref)
    gates = gate_ref[...]
    gk = [jnp.broadcast_to(gates[:, kk:kk + 1], (tm, LANES)) for kk in range(TOP_K)]
    ss = jnp.zeros((tm, 1), F32)
    for j in range(ROW_TILES):
        cols = slice(j * LANES, (j + 1) * LANES)
        moe = gk[0] * yk_refs[0][pl.ds(j, tm, stride=ROW_TILES), :]
        for kk in range(1, TOP_K):
            moe = moe + gk[kk] * yk_refs[kk][pl.ds(j, tm, stride=ROW_TILES), :]
        acc = h_ref[:, cols] + moe
        acc_scr[:, cols] = acc
        ss = ss + jnp.sum(acc * acc, axis=-1, keepdims=True)
    inv = lax.rsqrt(ss / D_MODEL + NORM_EPS)
    o_ref[...] = acc_scr[...] * inv * g_ref[...]


def _combine(yk, gates, h, g):
    t = h.shape[0]
    tm = TM_COMB
    return pl.pallas_call(
        _combine_kernel,
        grid=(t // tm,),
        in_specs=[pl.BlockSpec((tm * ROW_TILES, LANES), functools.partial(lambda kk, i: (kk * (t // tm) + i, 0), kk))
                  for kk in range(TOP_K)] + [
                  pl.BlockSpec((tm, LANES), lambda i: (i, 0)),
                  pl.BlockSpec((tm, D_MODEL), lambda i: (i, 0)),
                  pl.BlockSpec((1, D_MODEL), lambda i: (0, 0))],
        out_specs=pl.BlockSpec((tm, D_MODEL), lambda i: (i, 0)),
        out_shape=jax.ShapeDtypeStruct((t, D_MODEL), F32),
        scratch_shapes=[pltpu.VMEM((tm, D_MODEL), F32)],
        compiler_params=pltpu.CompilerParams(dimension_semantics=("arbitrary",), vmem_limit_bytes=VMEM_LIMIT),
        name="combine",
    )(yk, yk, yk, yk, gates, h, g)


def _routing_plan(top_idx, n_tokens):
    tm = TM_MOE
    n_rows = n_tokens * TOP_K
    nt_max = n_rows // tm + N_EXPERTS + 1
    flat_e = top_idx.reshape(-1)
    order = jnp.argsort(flat_e, stable=True).astype(I32)
    counts = jnp.sum((flat_e[:, None] == jnp.arange(N_EXPERTS, dtype=I32)[None, :]).astype(I32), axis=0)
    tiles_e = (counts + tm - 1) // tm
    tile_end = jnp.cumsum(tiles_e)
    tile_start = tile_end - tiles_e
    row_start = jnp.cumsum(counts) - counts
    n_tiles = tile_end[-1]
    tile_ids = jnp.arange(nt_max, dtype=I32)
    te = jnp.minimum(jnp.sum((tile_ids[:, None] >= tile_end[None, :]).astype(I32), axis=1), N_EXPERTS - 1)
    last_e = jnp.take(te, jnp.maximum(n_tiles - 1, 0))
    te = jnp.where(tile_ids < n_tiles, te, last_e).astype(I32)
    lane = jnp.arange(tm, dtype=I32)[None, :]
    local = (tile_ids - jnp.take(tile_start, te))[:, None] * tm + lane
    valid = (local < jnp.take(counts, te)[:, None]) & (tile_ids < n_tiles)[:, None]
    sorted_pos = jnp.clip(jnp.take(row_start, te)[:, None] + local, 0, n_rows - 1)
    flat = jnp.take(order, sorted_pos)
    src = jnp.where(valid, flat // TOP_K, 0) * ROW_TILES
    spare = n_rows + (tile_ids % 2)[:, None] * tm + lane
    slot_major = (flat % TOP_K) * n_tokens + flat // TOP_K
    dst = jnp.where(valid, slot_major, spare) * ROW_TILES
    zero_tile = (n_rows + tm + lane) * ROW_TILES
    dst_prev = jnp.concatenate([zero_tile, dst[:-1]], axis=0)
    return (te, n_tiles.reshape(1).astype(I32), src.reshape(nt_max, 1, tm).astype(I32),
            dst_prev.reshape(nt_max, 1, tm).astype(I32))


def kernel(x, norm1_g, w_in, a_ln_g, a_ln_b, a_w_s, a_b_s, a_out_g, b_out_g, w_out, norm2_g, w_router,
           b_router, w_gate_up, b_gate_up, w_down, b_down, normf_g):
    batch, seq, _ = x.shape
    t = batch * seq
    assert seq % (TM_PROJ) == 0 and t % TM_MOE == 0 and seq // DILATIONS[-1] == BAND
    h = x.reshape(t, D_MODEL)

    pos = jnp.arange(seq, dtype=F32)
    inv = ROPE_THETA ** (-jnp.arange(0, HEAD_DIM, 2, dtype=F32) / HEAD_DIM)
    ang = pos[:, None] * inv[None, :]
    cos = jnp.tile(jnp.cos(ang), (1, 2 * LANES // HEAD_DIM))
    sin = jnp.tile(jnp.concatenate([-jnp.sin(ang), jnp.sin(ang)], axis=1), (1, LANES // HEAD_DIM))
    head_of_lane = np.arange(A_WIDTH) // HEAD_DIM
    avg = jnp.asarray((head_of_lane[:, None] == head_of_lane[None, :]).astype(np.float32) / HEAD_DIM, dtype=BF16)
    row2 = lambda v: v.reshape(1, -1).astype(F32)

    for layer in range(norm1_g.shape[0]):
        causal = np.tril(np.ones((CHUNK, CHUNK), dtype=bool))
        ws = jnp.where(causal[None], a_w_s[layer], 0.0).astype(BF16)
        bs = jnp.repeat(a_b_s[layer].astype(F32).T, HEAD_DIM, axis=1)
        a_out, *qkv = _inproj(h, row2(norm1_g[layer]), w_in[layer].astype(BF16), avg,
                              row2(a_ln_g[layer]), row2(a_ln_b[layer]), ws, bs, row2(a_out_g[layer]),
                              cos, sin, seq)
        b_mix = _attention(qkv, batch, seq)
        w_r = jnp.pad(w_router[layer], ((0, 0), (0, LANES - N_EXPERTS))).astype(BF16)
        b_r = jnp.concatenate([b_router[layer].astype(F32), jnp.full((LANES - N_EXPERTS,), NEG_INF, F32)])
        h_mid, hn, idx, gates = _mix(a_out, b_mix, h, row2(b_out_g[layer]), w_out[layer].astype(BF16),
                                     row2(norm2_g[layer]), w_r, b_r.reshape(1, LANES))
        te, n_tiles, src, dst = _routing_plan(idx[:, :TOP_K], t)
        yk = _experts(te, n_tiles, src, dst, hn,
                      w_gate_up[layer], b_gate_up[layer].reshape(N_EXPERTS, 1, 2 * D_FF),
                      w_down[layer], b_down[layer].reshape(N_EXPERTS, 1, D_MODEL),
                      t * TOP_K + 2 * TM_MOE)
        last = layer == norm1_g.shape[0] - 1
        assert last, "the combine kernel fuses the final norm; depth > 1 is not supported"
        h = _combine(yk, gates, h_mid, row2(normf_g))
    return h.reshape(batch, seq, D_MODEL)
```

```python
import functools
import math

import numpy as np
import jax
import jax.numpy as jnp
from jax import lax
from jax.experimental import pallas as pl
from jax.experimental.pallas import tpu as pltpu

F32 = jnp.float32
BF16 = jnp.bfloat16
I32 = jnp.int32

D_MODEL = 1024
HEAD_DIM = 64
A_WIDTH = 512
B_WIDTH = 512
CHUNK = 128
BAND = 128
DILATIONS = (1, 4, 16)
ROPE_THETA = 10000.0
N_EXPERTS = 32
TOP_K = 4
D_FF = 1024
SWIGLU_ALPHA = 1.702
SWIGLU_LIMIT = 7.0
NORM_EPS = 1e-5
NEG_INF = -1e30

LANES = 128
SUBLANES = 8
ROW_TILES = D_MODEL // LANES

TM_PROJ = 512
PROJ_SUBTILES = 1
TM_MOE = 512
TM_COMB = 512
DMA_UNROLL = 16
MOE_FF_CHUNKS = 2
MOE_DMA_GROUP = TM_MOE // (3 * MOE_FF_CHUNKS + 2)
VMEM_LIMIT = 56 * 1024 * 1024


def _dot(a, b):
    return jnp.dot(a, b, preferred_element_type=F32)


def _gelu_tanh(x):
    c = math.sqrt(2.0 / math.pi)
    cdf = 0.5 * (1.0 + jnp.tanh(c * (x + 0.044715 * (x * x * x))))
    return x * cdf


def _rms(x, g):
    return x * lax.rsqrt(jnp.mean(x * x, axis=-1, keepdims=True) + NORM_EPS) * g


def _inproj_kernel(x_ref, g1_ref, w_ref, avg_ref, lng_ref, lnb_ref, ws_ref, bs_ref, aog_ref,
                   cos_ref, sin_ref, a_ref, q_ref, k_ref, v_ref, q4_ref, k4_ref, v4_ref,
                   q16_ref, k16_ref, v16_ref, a_scr, qkv_scr, cls_scr):
    tm = x_ref.shape[0]
    n = tm // PROJ_SUBTILES
    nlt = B_WIDTH // LANES
    d4, d16 = DILATIONS[1], DILATIONS[2]
    first_head = lax.broadcasted_iota(I32, (CHUNK, LANES), 1) < HEAD_DIM
    first_half = (lax.broadcasted_iota(I32, (n, LANES), 1) % HEAD_DIM) < (HEAD_DIM // 2)
    avg = avg_ref[...]

    for sub in range(PROJ_SUBTILES):
        r0 = sub * n
        rows_n = slice(r0, r0 + n)
        xn = _rms(x_ref[rows_n, :], g1_ref[...]).astype(BF16)

        ug = _gelu_tanh(_dot(xn, w_ref[:, 0:A_WIDTH]))
        vg = _gelu_tanh(_dot(xn, w_ref[:, A_WIDTH:2 * A_WIDTH]))
        mu = _dot(vg.astype(BF16), avg)
        d = vg - mu
        var = _dot((d * d).astype(BF16), avg)
        vn = (d * lax.rsqrt(var + NORM_EPS) * lng_ref[...] + lnb_ref[...]).astype(BF16)
        for c in range(n // CHUNK):
            rows = slice(c * CHUNK, (c + 1) * CHUNK)
            for p in range(A_WIDTH // LANES):
                cols = slice(p * LANES, (p + 1) * LANES)
                slab = vn[rows, cols]
                g = jnp.where(first_head, _dot(ws_ref[2 * p], slab), _dot(ws_ref[2 * p + 1], slab))
                a_scr[r0 + c * CHUNK:r0 + (c + 1) * CHUNK, cols] = ug[rows, cols] * (g + bs_ref[:, cols])
        a_ref[rows_n, :] = _rms(a_scr[rows_n, :], aog_ref[...]).astype(BF16)

        cos = cos_ref[rows_n, :]
        sin = sin_ref[rows_n, :]

        def rope(t):
            rot = jnp.where(first_half, pltpu.roll(t, LANES - HEAD_DIM // 2, 1), pltpu.roll(t, HEAD_DIM // 2, 1))
            return t * cos + rot * sin

        off = 2 * A_WIDTH
        for p in range(nlt):
            q = _dot(xn, w_ref[:, off + p * LANES: off + (p + 1) * LANES])
            qkv_scr[p, rows_n, :] = rope(q) * (HEAD_DIM ** -0.5)
            k = _dot(xn, w_ref[:, off + B_WIDTH + p * LANES: off + B_WIDTH + (p + 1) * LANES])
            qkv_scr[nlt + p, rows_n, :] = rope(k)
            qkv_scr[2 * nlt + p, rows_n, :] = _dot(xn, w_ref[:, off + 2 * B_WIDTH + p * LANES:
                                                             off + 2 * B_WIDTH + (p + 1) * LANES])

        rows4 = slice(r0 // d4, (r0 + n) // d4)
        rows16 = slice(r0 // d16, (r0 + n) // d16)
        for i, (nat_ref, c4_ref, c16_ref) in enumerate(((q_ref, q4_ref, q16_ref), (k_ref, k4_ref, k16_ref),
                                                        (v_ref, v4_ref, v16_ref))):
            for p in range(nlt):
                cols = slice(p * LANES, (p + 1) * LANES)
                nat_ref[rows_n, cols] = qkv_scr[i * nlt + p, rows_n, :].astype(BF16)
                for r4 in range(d4):
                    cls = qkv_scr[i * nlt + p, pl.ds(r0 + r4, n // d4, stride=d4), :]
                    c4_ref[0, r4, rows4, cols] = cls.astype(BF16)
                    cls_scr[i * nlt + p, r4, rows4, :] = cls
                for r4 in range(d4):
                    for a in range(d16 // d4):
                        c16_ref[0, r4 + d4 * a, rows16, cols] = cls_scr[
                            i * nlt + p, r4, pl.ds(r0 // d4 + a, n // d16, stride=d16 // d4), :].astype(BF16)


def _inproj(x2, g1, w_in, avg, lng, lnb, ws, bs, aog, cos, sin, seq):
    t = x2.shape[0]
    tm = TM_PROJ
    nseq = seq // tm
    full = lambda shape: pl.BlockSpec(shape, lambda i: (0,) * len(shape))
    rows = lambda w: pl.BlockSpec((tm, w), lambda i: (i, 0))
    classes = lambda dil: pl.BlockSpec((1, dil, tm // dil, B_WIDTH), lambda i: (i // nseq, 0, i % nseq, 0))
    class_shape = lambda dil: jax.ShapeDtypeStruct((t // seq, dil, seq // dil, B_WIDTH), BF16)
    return pl.pallas_call(
        _inproj_kernel,
        grid=(t // tm,),
        in_specs=[rows(D_MODEL), full((1, D_MODEL)), full(w_in.shape), full(avg.shape),
                  full((1, A_WIDTH)), full((1, A_WIDTH)), full(ws.shape), full(bs.shape),
                  full((1, A_WIDTH)),
                  pl.BlockSpec((tm, LANES), lambda i: (i % nseq, 0)),
                  pl.BlockSpec((tm, LANES), lambda i: (i % nseq, 0))],
        out_specs=[rows(A_WIDTH)] + [rows(B_WIDTH)] * 3 + [classes(DILATIONS[1])] * 3 + [classes(DILATIONS[2])] * 3,
        out_shape=([jax.ShapeDtypeStruct((t, A_WIDTH), BF16)] + [jax.ShapeDtypeStruct((t, B_WIDTH), BF16)] * 3
                   + [class_shape(DILATIONS[1])] * 3 + [class_shape(DILATIONS[2])] * 3),
        scratch_shapes=[pltpu.VMEM((tm, A_WIDTH), F32), pltpu.VMEM((3 * B_WIDTH // LANES, tm, LANES), F32),
                        pltpu.VMEM((3 * B_WIDTH // LANES, DILATIONS[1], tm // DILATIONS[1], LANES), F32)],
        compiler_params=pltpu.CompilerParams(dimension_semantics=("arbitrary",), vmem_limit_bytes=VMEM_LIMIT),
        name="inproj",
    )(x2, g1, w_in, avg, lng, lnb, ws, bs, aog, cos, sin)


def _attn_block(qb, kw, vw, bias, first_head):
    zero = jnp.zeros_like(qb)
    q2 = jnp.concatenate([jnp.where(first_head, qb, zero), jnp.where(first_head, zero, qb)], axis=0)
    s = lax.dot_general(q2, kw, (((1,), (1,)), ((), ())), preferred_element_type=F32) + bias
    m = jnp.max(s, axis=-1, keepdims=True)
    p = jnp.exp(s - m)
    l = jnp.sum(p, axis=-1, keepdims=True)
    o = _dot(p.astype(BF16), vw) / l
    lse = jnp.broadcast_to(m + jnp.log(l), o.shape)
    return (jnp.where(first_head, o[:BAND], o[BAND:]),
            jnp.where(first_head, lse[:BAND], lse[BAND:]))


def _attn_kernel(q1, k1, v1, q4, k4, v4, q16, k16, v16, bias_band_ref, bias_first_ref, o_ref, o_scr, l_scr):
    seq = q1.shape[0]
    first_head = lax.broadcasted_iota(I32, (BAND, LANES), 1) < HEAD_DIM
    branches = ((q1, k1, v1), (q4, k4, v4), (q16, k16, v16))
    for bi, dil in enumerate(DILATIONS):
        q_r, k_r, v_r = branches[bi]
        length = seq // dil
        for r in range(dil):
            for n in range(length // BAND):
                lo = n * BAND
                if bi == 0:
                    ref_slice = lambda ref, a, b: ref[a:b, :]
                else:
                    ref_slice = lambda ref, a, b, r=r: ref[0, r, a:b, :]
                qb = ref_slice(q_r, lo, lo + BAND)
                if n == 0:
                    kw, vw, bias = ref_slice(k_r, 0, BAND), ref_slice(v_r, 0, BAND), bias_first_ref[...]
                else:
                    kw, vw = ref_slice(k_r, lo - BAND, lo + BAND), ref_slice(v_r, lo - BAND, lo + BAND)
                    bias = bias_band_ref[...]
                o, lse = _attn_block(qb, kw, vw, bias, first_head)
                if dil == 1:
                    o_scr[bi, lo:lo + BAND, :] = o
                    l_scr[bi, lo:lo + BAND, :] = lse
                else:
                    dst = pl.ds(r + dil * lo, BAND, stride=dil)
                    o_scr[bi, dst, :] = o
                    l_scr[bi, dst, :] = lse
    lses = [l_scr[i] for i in range(3)]
    m = jnp.maximum(jnp.maximum(lses[0], lses[1]), lses[2])
    es = [jnp.exp(l - m) for l in lses]
    den = es[0] + es[1] + es[2]
    o_ref[...] = (es[0] / den) * o_scr[0] + (es[1] / den) * o_scr[1] + (es[2] / den) * o_scr[2]


def _attention(qkv, batch, seq):
    rel = (np.arange(BAND)[:, None] + BAND) - np.arange(2 * BAND)[None, :]
    band = np.where((rel >= 0) & (rel <= BAND), 0.0, NEG_INF).astype(np.float32)
    bias_band = jnp.asarray(np.concatenate([band, band], axis=0))
    bias_first = jnp.asarray(np.concatenate([band[:, BAND:], band[:, BAND:]], axis=0))
    nat = pl.BlockSpec((seq, LANES), lambda b, p: (b, p))
    cls = lambda dil: pl.BlockSpec((1, dil, seq // dil, LANES), lambda b, p: (b, 0, 0, p))
    full = lambda a: pl.BlockSpec(a.shape, lambda b, p: (0, 0))
    return pl.pallas_call(
        _attn_kernel,
        grid=(batch, B_WIDTH // LANES),
        in_specs=[nat] * 3 + [cls(4)] * 3 + [cls(16)] * 3 + [full(bias_band), full(bias_first)],
        out_specs=pl.BlockSpec((seq, LANES), lambda b, p: (b, p)),
        out_shape=jax.ShapeDtypeStruct((batch * seq, B_WIDTH), F32),
        scratch_shapes=[pltpu.VMEM((3, seq, LANES), F32), pltpu.VMEM((3, seq, LANES), F32)],
        compiler_params=pltpu.CompilerParams(dimension_semantics=("arbitrary", "arbitrary"),
                                             vmem_limit_bytes=VMEM_LIMIT),
        name="dilated_attention",
    )(*qkv, bias_band, bias_first)


def _mix_kernel(a_ref, bm_ref, x_ref, bog_ref, wout_ref, n2g_ref, wr_ref, br_ref,
                h_ref, hn_ref, idx_ref, gate_ref):
    tm = x_ref.shape[0]
    bn = _rms(bm_ref[...], bog_ref[...]).astype(BF16)
    mixed = jnp.concatenate([a_ref[...], bn], axis=1)
    h = x_ref[...] + _dot(mixed, wout_ref[...])
    h_ref[...] = h
    hn = _rms(h, n2g_ref[...])
    for j in range(ROW_TILES):
        hn_ref[pl.ds(j, tm, stride=ROW_TILES), :] = hn[:, j * LANES:(j + 1) * LANES]
    logits = _dot(hn.astype(BF16), wr_ref[...]) + br_ref[...]
    lane = lax.broadcasted_iota(I32, (tm, LANES), 1)
    vals, idxs = [], []
    for _ in range(TOP_K):
        m = jnp.max(logits, axis=-1, keepdims=True)
        am = jnp.min(jnp.where(logits == m, lane, LANES), axis=-1, keepdims=True)
        vals.append(m)
        idxs.append(am)
        logits = jnp.where(lane == am, -jnp.inf, logits)
    es = [jnp.exp(v - vals[0]) for v in vals]
    den = es[0] + es[1] + es[2] + es[3]
    idx_out = jnp.zeros((tm, LANES), I32)
    gate_out = jnp.zeros((tm, LANES), F32)
    for kk in range(TOP_K):
        idx_out = jnp.where(lane == kk, idxs[kk], idx_out)
        gate_out = jnp.where(lane == kk, es[kk] / den, gate_out)
    idx_ref[...] = idx_out
    gate_ref[...] = gate_out


def _mix(a_out, b_mix, x2, bog, w_out, n2g, w_r, b_r):
    t = x2.shape[0]
    tm = TM_PROJ
    full = lambda shape: pl.BlockSpec(shape, lambda i: (0,) * len(shape))
    rows = lambda w: pl.BlockSpec((tm, w), lambda i: (i, 0))
    return pl.pallas_call(
        _mix_kernel,
        grid=(t // tm,),
        in_specs=[rows(A_WIDTH), rows(B_WIDTH), rows(D_MODEL), full((1, B_WIDTH)), full(w_out.shape),
                  full((1, D_MODEL)), full(w_r.shape), full((1, LANES))],
        out_specs=[rows(D_MODEL), pl.BlockSpec((tm * ROW_TILES, LANES), lambda i: (i, 0)), rows(LANES), rows(LANES)],
        out_shape=[jax.ShapeDtypeStruct((t, D_MODEL), F32), jax.ShapeDtypeStruct((t * ROW_TILES, LANES), F32),
                   jax.ShapeDtypeStruct((t, LANES), I32), jax.ShapeDtypeStruct((t, LANES), F32)],
        compiler_params=pltpu.CompilerParams(dimension_semantics=("arbitrary",), vmem_limit_bytes=VMEM_LIMIT),
        name="mix_router",
    )(a_out, b_mix, x2, bog, w_out, n2g, w_r, b_r)


def _row_in(hn_hbm, buf, sem, src_row, r):
    return pltpu.make_async_copy(hn_hbm.at[pl.ds(pl.multiple_of(src_row, SUBLANES), SUBLANES), :],
                                 buf.at[pl.ds(pl.multiple_of(r * SUBLANES, SUBLANES), SUBLANES), :], sem)


def _row_out(buf, yk_hbm, sem, dst_row, r):
    return pltpu.make_async_copy(buf.at[pl.ds(pl.multiple_of(r * SUBLANES, SUBLANES), SUBLANES), :],
                                 yk_hbm.at[pl.ds(pl.multiple_of(dst_row, SUBLANES), SUBLANES), :], sem)


def _moe_kernel(te_ref, nt_ref, src_first_ref, src_next_ref, dst_prev_ref, hn_hbm, wgu_ref, bgu_ref, wd_ref,
                bd_ref, yk_hbm, xbuf0, xbuf1, ybuf0, ybuf1, wgu_bf, wd_bf, act_scr, gsem, ssem):
    tm = TM_MOE
    tile_rows = tm * ROW_TILES
    i = pl.program_id(0)
    nt = nt_ref[0]
    xbufs = (xbuf0, xbuf1)
    ybufs = (ybuf0, ybuf1)

    def wait_gather(p):
        pltpu.make_async_copy(hn_hbm.at[pl.ds(0, tile_rows), :], xbufs[p], gsem.at[p]).wait()

    def wait_scatter(p):
        pltpu.make_async_copy(ybufs[p], yk_hbm.at[pl.ds(0, tile_rows), :], ssem.at[p]).wait()

    def issue_loop(start_row):
        def body(g, c):
            for u in range(DMA_UNROLL):
                start_row(g * DMA_UNROLL + u, u % 2)
            return c
        lax.fori_loop(0, tm // DMA_UNROLL, body, 0)

    @pl.when(i == 0)
    def _():
        issue_loop(lambda r, pr: _row_in(hn_hbm, xbuf0, gsem.at[0], src_first_ref[0, 0, r], r).start(priority=pr))
        ybuf1[...] = jnp.zeros((tile_rows, LANES), F32)
        spare = pltpu.make_async_copy(ybuf1, yk_hbm.at[pl.ds(yk_hbm.shape[0] - 2 * tile_rows, tile_rows), :],
                                      ssem.at[1])
        spare.start()
        spare.wait()

    for p in (0, 1):
        q = 1 - p

        @pl.when((i < nt) & (i % 2 == p))
        def _(p=p, q=q):
            wait_gather(p)

            @pl.when((i == 0) | (te_ref[i] != te_ref[jnp.maximum(i - 1, 0)]))
            def _():
                wgu_bf[...] = wgu_ref[0].astype(BF16)
                wd_bf[...] = wd_ref[0].astype(BF16)

            def dma_group(g):
                for r in range(g * MOE_DMA_GROUP, (g + 1) * MOE_DMA_GROUP):
                    _row_in(hn_hbm, xbufs[q], gsem.at[q], src_next_ref[0, 0, r], r).start(priority=r % 2)
                    _row_out(ybufs[q], yk_hbm, ssem.at[q], dst_prev_ref[0, 0, r], r).start(priority=(r + 1) % 2)

            xs = xbufs[p]
            x = jnp.concatenate([xs[pl.ds(j, tm, stride=ROW_TILES), :] for j in range(ROW_TILES)],
                                axis=1).astype(BF16)
            width = D_FF // MOE_FF_CHUNKS
            dma_group(0)
            group = 1
            for c in range(MOE_FF_CHUNKS):
                lo, hi = c * width, (c + 1) * width
                dma_group(group)
                gate = _dot(x, wgu_bf[:, lo:hi]) + bgu_ref[0, :, lo:hi]
                dma_group(group + 1)
                up = _dot(x, wgu_bf[:, D_FF + lo:D_FF + hi]) + bgu_ref[0, :, D_FF + lo:D_FF + hi]
                dma_group(group + 2)
                gate = jnp.minimum(gate, SWIGLU_LIMIT)
                up = jnp.clip(up, -SWIGLU_LIMIT, SWIGLU_LIMIT)
                act_scr[:, lo:hi] = ((up + 1.0) * (gate * jax.nn.sigmoid(gate * SWIGLU_ALPHA))).astype(BF16)
                group += 3
            dma_group(group)
            assert (group + 1) * MOE_DMA_GROUP == tm
            y = _dot(act_scr[...], wd_bf[...]) + bd_ref[0]

            @pl.when(i >= 1)
            def _():
                wait_scatter(p)
            ys = ybufs[p]
            for j in range(ROW_TILES):
                ys[pl.ds(j, tm, stride=ROW_TILES), :] = y[:, j * LANES:(j + 1) * LANES]

    for p in (0, 1):
        q = 1 - p

        @pl.when((i == nt) & (i % 2 == p))
        def _(p=p, q=q):
            wait_gather(p)
            issue_loop(lambda r, pr: _row_out(ybufs[q], yk_hbm, ssem.at[q], dst_prev_ref[0, 0, r], r)
                       .start(priority=pr))
            wait_scatter(q)
            wait_scatter(p)


def _experts(tile_expert, n_tiles, src_rows, dst_prev_rows, hn_tiles, wgu, bgu, wd, bd, out_rows):
    nt_max = src_rows.shape[0]
    tm = TM_MOE
    smem = lambda f: pl.BlockSpec((1, 1, tm), f, memory_space=pltpu.SMEM)
    row_buf = pltpu.VMEM((tm * ROW_TILES, LANES), F32)
    grid_spec = pltpu.PrefetchScalarGridSpec(
        num_scalar_prefetch=2,
        grid=(nt_max,),
        in_specs=[smem(lambda i, te, nt: (0, 0, 0)),
                  smem(lambda i, te, nt: (jnp.minimum(i + 1, nt_max - 1), 0, 0)),
                  smem(lambda i, te, nt: (i, 0, 0)),
                  pl.BlockSpec(memory_space=pl.ANY),
                  pl.BlockSpec((1, D_MODEL, 2 * D_FF), lambda i, te, nt: (te[i], 0, 0)),
                  pl.BlockSpec((1, 1, 2 * D_FF), lambda i, te, nt: (te[i], 0, 0)),
                  pl.BlockSpec((1, D_FF, D_MODEL), lambda i, te, nt: (te[i], 0, 0)),
                  pl.BlockSpec((1, 1, D_MODEL), lambda i, te, nt: (te[i], 0, 0))],
        out_specs=pl.BlockSpec(memory_space=pl.ANY),
        scratch_shapes=[row_buf, row_buf, row_buf, row_buf,
                        pltpu.VMEM((D_MODEL, 2 * D_FF), BF16), pltpu.VMEM((D_FF, D_MODEL), BF16),
                        pltpu.VMEM((tm, D_FF), BF16),
                        pltpu.SemaphoreType.DMA((2,)), pltpu.SemaphoreType.DMA((2,))],
    )
    return pl.pallas_call(
        _moe_kernel,
        grid_spec=grid_spec,
        out_shape=jax.ShapeDtypeStruct((out_rows * ROW_TILES, LANES), F32),
        compiler_params=pltpu.CompilerParams(dimension_semantics=("arbitrary",), vmem_limit_bytes=VMEM_LIMIT),
        name="experts",
    )(tile_expert, n_tiles, src_rows, src_rows, dst_prev_rows, hn_tiles, wgu, bgu, wd, bd)


def _combine_kernel(yk0_ref, yk1_ref, yk2_ref, yk3_ref, gate_ref, h_ref, g_ref, o_ref, acc_scr):
    tm = h_ref.shape[0]
    yk_refs = (yk0_ref, yk1_ref, yk2_ref, yk3_ref)
    gates = gate_ref[...]
    gk = [jnp.broadcast_to(gates[:, kk:kk + 1], (tm, LANES)) for kk in range(TOP_K)]
    ss = jnp.zeros((tm, 1), F32)
    for j in range(ROW_TILES):
        cols = slice(j * LANES, (j + 1) * LANES)
        moe = gk[0] * yk_refs[0][pl.ds(j, tm, stride=ROW_TILES), :]
        for kk in range(1, TOP_K):
            moe = moe + gk[kk] * yk_refs[kk][pl.ds(j, tm, stride=ROW_TILES), :]
        acc = h_ref[:, cols] + moe
        acc_scr[:, cols] = acc
        ss = ss + jnp.sum(acc * acc, axis=-1, keepdims=True)
    inv = lax.rsqrt(ss / D_MODEL + NORM_EPS)
    o_ref[...] = acc_scr[...] * inv * g_ref[...]


def _combine(yk, gates, h, g):
    t = h.shape[0]
    tm = TM_COMB
    return pl.pallas_call(
        _combine_kernel,
        grid=(t // tm,),
        in_specs=[pl.BlockSpec((tm * ROW_TILES, LANES), functools.partial(lambda kk, i: (kk * (t // tm) + i, 0), kk))
                  for kk in range(TOP_K)] + [
                  pl.BlockSpec((tm, LANES), lambda i: (i, 0)),
                  pl.BlockSpec((tm, D_MODEL), lambda i: (i, 0)),
                  pl.BlockSpec((1, D_MODEL), lambda i: (0, 0))],
        out_specs=pl.BlockSpec((tm, D_MODEL), lambda i: (i, 0)),
        out_shape=jax.ShapeDtypeStruct((t, D_MODEL), F32),
        scratch_shapes=[pltpu.VMEM((tm, D_MODEL), F32)],
        compiler_params=pltpu.CompilerParams(dimension_semantics=("arbitrary",), vmem_limit_bytes=VMEM_LIMIT),
        name="combine",
    )(yk, yk, yk, yk, gates, h, g)


def _routing_plan(top_idx, n_tokens):
    tm = TM_MOE
    n_rows = n_tokens * TOP_K
    nt_max = n_rows // tm + N_EXPERTS + 1
    flat_e = top_idx.reshape(-1)
    order = jnp.sort(flat_e * n_rows + jnp.arange(n_rows, dtype=I32)) % n_rows
    counts = jnp.sum((flat_e[:, None] == jnp.arange(N_EXPERTS, dtype=I32)[None, :]).astype(I32), axis=0)
    tiles_e = (counts + tm - 1) // tm
    tile_end = jnp.cumsum(tiles_e)
    tile_start = tile_end - tiles_e
    row_start = jnp.cumsum(counts) - counts
    n_tiles = tile_end[-1]
    tile_ids = jnp.arange(nt_max, dtype=I32)
    te = jnp.minimum(jnp.sum((tile_ids[:, None] >= tile_end[None, :]).astype(I32), axis=1), N_EXPERTS - 1)
    last_e = jnp.take(te, jnp.maximum(n_tiles - 1, 0))
    te = jnp.where(tile_ids < n_tiles, te, last_e).astype(I32)
    lane = jnp.arange(tm, dtype=I32)[None, :]
    local = (tile_ids - jnp.take(tile_start, te))[:, None] * tm + lane
    valid = (local < jnp.take(counts, te)[:, None]) & (tile_ids < n_tiles)[:, None]
    sorted_pos = jnp.clip(jnp.take(row_start, te)[:, None] + local, 0, n_rows - 1)
    flat = jnp.take(order, sorted_pos)
    src = jnp.where(valid, flat // TOP_K, 0) * ROW_TILES
    spare = n_rows + (tile_ids % 2)[:, None] * tm + lane
    slot_major = (flat % TOP_K) * n_tokens + flat // TOP_K
    dst = jnp.where(valid, slot_major, spare) * ROW_TILES
    zero_tile = (n_rows + tm + lane) * ROW_TILES
    dst_prev = jnp.concatenate([zero_tile, dst[:-1]], axis=0)
    return (te, n_tiles.reshape(1).astype(I32), src.reshape(nt_max, 1, tm).astype(I32),
            dst_prev.reshape(nt_max, 1, tm).astype(I32))


def kernel(x, norm1_g, w_in, a_ln_g, a_ln_b, a_w_s, a_b_s, a_out_g, b_out_g, w_out, norm2_g, w_router,
           b_router, w_gate_up, b_gate_up, w_down, b_down, normf_g):
    batch, seq, _ = x.shape
    t = batch * seq
    assert seq % (TM_PROJ) == 0 and t % TM_MOE == 0 and seq // DILATIONS[-1] == BAND
    h = x.reshape(t, D_MODEL)

    pos = jnp.arange(seq, dtype=F32)
    inv = ROPE_THETA ** (-jnp.arange(0, HEAD_DIM, 2, dtype=F32) / HEAD_DIM)
    ang = pos[:, None] * inv[None, :]
    cos = jnp.tile(jnp.cos(ang), (1, 2 * LANES // HEAD_DIM))
    sin = jnp.tile(jnp.concatenate([-jnp.sin(ang), jnp.sin(ang)], axis=1), (1, LANES // HEAD_DIM))
    head_of_lane = np.arange(A_WIDTH) // HEAD_DIM
    avg = jnp.asarray((head_of_lane[:, None] == head_of_lane[None, :]).astype(np.float32) / HEAD_DIM, dtype=BF16)
    row2 = lambda v: v.reshape(1, -1).astype(F32)

    for layer in range(norm1_g.shape[0]):
        causal = np.tril(np.ones((CHUNK, CHUNK), dtype=bool))
        ws = jnp.where(causal[None], a_w_s[layer], 0.0).astype(BF16)
        bs = jnp.repeat(a_b_s[layer].astype(F32).T, HEAD_DIM, axis=1)
        a_out, *qkv = _inproj(h, row2(norm1_g[layer]), w_in[layer].astype(BF16), avg,
                              row2(a_ln_g[layer]), row2(a_ln_b[layer]), ws, bs, row2(a_out_g[layer]),
                              cos, sin, seq)
        b_mix = _attention(qkv, batch, seq)
        w_r = jnp.pad(w_router[layer], ((0, 0), (0, LANES - N_EXPERTS))).astype(BF16)
        b_r = jnp.concatenate([b_router[layer].astype(F32), jnp.full((LANES - N_EXPERTS,), NEG_INF, F32)])
        h_mid, hn, idx, gates = _mix(a_out, b_mix, h, row2(b_out_g[layer]), w_out[layer].astype(BF16),
                                     row2(norm2_g[layer]), w_r, b_r.reshape(1, LANES))
        te, n_tiles, src, dst = _routing_plan(idx[:, :TOP_K], t)
        yk = _experts(te, n_tiles, src, dst, hn,
                      w_gate_up[layer], b_gate_up[layer].reshape(N_EXPERTS, 1, 2 * D_FF),
                      w_down[layer], b_down[layer].reshape(N_EXPERTS, 1, D_MODEL),
                      t * TOP_K + 2 * TM_MOE)
        last = layer == norm1_g.shape[0] - 1
        assert last, "the combine kernel fuses the final norm; depth > 1 is not supported"
        h = _combine(yk, gates, h_mid, row2(normf_g))
    return h.reshape(batch, seq, D_MODEL)
```

```python
import functools
import math

import numpy as np
import jax
import jax.numpy as jnp
from jax import lax
from jax.experimental import pallas as pl
from jax.experimental.pallas import tpu as pltpu

F32 = jnp.float32
BF16 = jnp.bfloat16
I32 = jnp.int32

D_MODEL = 1024
HEAD_DIM = 64
A_WIDTH = 512
B_WIDTH = 512
CHUNK = 128
BAND = 128
DILATIONS = (1, 4, 16)
ROPE_THETA = 10000.0
N_EXPERTS = 32
TOP_K = 4
D_FF = 1024
SWIGLU_ALPHA = 1.702
SWIGLU_LIMIT = 7.0
NORM_EPS = 1e-5
NEG_INF = -1e30

LANES = 128
SUBLANES = 8
ROW_TILES = D_MODEL // LANES

TM_PROJ = 512
PROJ_SUBTILES = 1
TM_MOE = 512
TM_COMB = 512
DMA_UNROLL = 16
MOE_FF_CHUNKS = 2
MOE_DMA_GROUP = TM_MOE // (3 * MOE_FF_CHUNKS + 2)
VMEM_LIMIT = 56 * 1024 * 1024


def _dot(a, b):
    return jnp.dot(a, b, preferred_element_type=F32)


def _gelu_tanh(x):
    c = math.sqrt(2.0 / math.pi)
    cdf = 0.5 * (1.0 + jnp.tanh(c * (x + 0.044715 * (x * x * x))))
    return x * cdf


def _rms(x, g):
    return x * lax.rsqrt(jnp.mean(x * x, axis=-1, keepdims=True) + NORM_EPS) * g


def _inproj_kernel(x_ref, g1_ref, w_ref, avg_ref, lng_ref, lnb_ref, ws_ref, bs_ref, aog_ref,
                   cos_ref, sin_ref, a_ref, q_ref, k_ref, v_ref, q4_ref, k4_ref, v4_ref,
                   q16_ref, k16_ref, v16_ref, a_scr, qkv_scr, cls_scr):
    tm = x_ref.shape[0]
    n = tm // PROJ_SUBTILES
    nlt = B_WIDTH // LANES
    d4, d16 = DILATIONS[1], DILATIONS[2]
    first_head = lax.broadcasted_iota(I32, (CHUNK, LANES), 1) < HEAD_DIM
    first_half = (lax.broadcasted_iota(I32, (n, LANES), 1) % HEAD_DIM) < (HEAD_DIM // 2)
    avg = avg_ref[...]

    for sub in range(PROJ_SUBTILES):
        r0 = sub * n
        rows_n = slice(r0, r0 + n)
        xn = _rms(x_ref[rows_n, :], g1_ref[...]).astype(BF16)

        ug = _gelu_tanh(_dot(xn, w_ref[:, 0:A_WIDTH]))
        vg = _gelu_tanh(_dot(xn, w_ref[:, A_WIDTH:2 * A_WIDTH]))
        mu = _dot(vg.astype(BF16), avg)
        d = vg - mu
        var = _dot((d * d).astype(BF16), avg)
        vn = (d * lax.rsqrt(var + NORM_EPS) * lng_ref[...] + lnb_ref[...]).astype(BF16)
        for c in range(n // CHUNK):
            rows = slice(c * CHUNK, (c + 1) * CHUNK)
            for p in range(A_WIDTH // LANES):
                cols = slice(p * LANES, (p + 1) * LANES)
                slab = vn[rows, cols]
                g = jnp.where(first_head, _dot(ws_ref[2 * p], slab), _dot(ws_ref[2 * p + 1], slab))
                a_scr[r0 + c * CHUNK:r0 + (c + 1) * CHUNK, cols] = ug[rows, cols] * (g + bs_ref[:, cols])
        a_ref[rows_n, :] = _rms(a_scr[rows_n, :], aog_ref[...]).astype(BF16)

        cos = cos_ref[rows_n, :]
        sin = sin_ref[rows_n, :]

        def rope(t):
            rot = jnp.where(first_half, pltpu.roll(t, LANES - HEAD_DIM // 2, 1), pltpu.roll(t, HEAD_DIM // 2, 1))
            return t * cos + rot * sin

        off = 2 * A_WIDTH
        for p in range(nlt):
            q = _dot(xn, w_ref[:, off + p * LANES: off + (p + 1) * LANES])
            qkv_scr[p, rows_n, :] = rope(q) * (HEAD_DIM ** -0.5)
            k = _dot(xn, w_ref[:, off + B_WIDTH + p * LANES: off + B_WIDTH + (p + 1) * LANES])
            qkv_scr[nlt + p, rows_n, :] = rope(k)
            qkv_scr[2 * nlt + p, rows_n, :] = _dot(xn, w_ref[:, off + 2 * B_WIDTH + p * LANES:
                                                             off + 2 * B_WIDTH + (p + 1) * LANES])

        rows4 = slice(r0 // d4, (r0 + n) // d4)
        rows16 = slice(r0 // d16, (r0 + n) // d16)
        for i, (nat_ref, c4_ref, c16_ref) in enumerate(((q_ref, q4_ref, q16_ref), (k_ref, k4_ref, k16_ref),
                                                        (v_ref, v4_ref, v16_ref))):
            for p in range(nlt):
                cols = slice(p * LANES, (p + 1) * LANES)
                nat_ref[rows_n, cols] = qkv_scr[i * nlt + p, rows_n, :].astype(BF16)
                for r4 in range(d4):
                    cls = qkv_scr[i * nlt + p, pl.ds(r0 + r4, n // d4, stride=d4), :]
                    c4_ref[0, r4, rows4, cols] = cls.astype(BF16)
                    cls_scr[i * nlt + p, r4, rows4, :] = cls
                for r4 in range(d4):
                    for a in range(d16 // d4):
                        c16_ref[0, r4 + d4 * a, rows16, cols] = cls_scr[
                            i * nlt + p, r4, pl.ds(r0 // d4 + a, n // d16, stride=d16 // d4), :].astype(BF16)


def _inproj(x2, g1, w_in, avg, lng, lnb, ws, bs, aog, cos, sin, seq):
    t = x2.shape[0]
    tm = TM_PROJ
    nseq = seq // tm
    full = lambda shape: pl.BlockSpec(shape, lambda i: (0,) * len(shape))
    rows = lambda w: pl.BlockSpec((tm, w), lambda i: (i, 0))
    classes = lambda dil: pl.BlockSpec((1, dil, tm // dil, B_WIDTH), lambda i: (i // nseq, 0, i % nseq, 0))
    class_shape = lambda dil: jax.ShapeDtypeStruct((t // seq, dil, seq // dil, B_WIDTH), BF16)
    return pl.pallas_call(
        _inproj_kernel,
        grid=(t // tm,),
        in_specs=[rows(D_MODEL), full((1, D_MODEL)), full(w_in.shape), full(avg.shape),
                  full((1, A_WIDTH)), full((1, A_WIDTH)), full(ws.shape), full(bs.shape),
                  full((1, A_WIDTH)),
                  pl.BlockSpec((tm, LANES), lambda i: (i % nseq, 0)),
                  pl.BlockSpec((tm, LANES), lambda i: (i % nseq, 0))],
        out_specs=[rows(A_WIDTH)] + [rows(B_WIDTH)] * 3 + [classes(DILATIONS[1])] * 3 + [classes(DILATIONS[2])] * 3,
        out_shape=([jax.ShapeDtypeStruct((t, A_WIDTH), BF16)] + [jax.ShapeDtypeStruct((t, B_WIDTH), BF16)] * 3
                   + [class_shape(DILATIONS[1])] * 3 + [class_shape(DILATIONS[2])] * 3),
        scratch_shapes=[pltpu.VMEM((tm, A_WIDTH), F32), pltpu.VMEM((3 * B_WIDTH // LANES, tm, LANES), F32),
                        pltpu.VMEM((3 * B_WIDTH // LANES, DILATIONS[1], tm // DILATIONS[1], LANES), F32)],
        compiler_params=pltpu.CompilerParams(dimension_semantics=("arbitrary",), vmem_limit_bytes=VMEM_LIMIT),
        name="inproj",
    )(x2, g1, w_in, avg, lng, lnb, ws, bs, aog, cos, sin)


def _attn_block(qb, kw, vw, bias, first_head):
    zero = jnp.zeros_like(qb)
    q2 = jnp.concatenate([jnp.where(first_head, qb, zero), jnp.where(first_head, zero, qb)], axis=0)
    s = lax.dot_general(q2, kw, (((1,), (1,)), ((), ())), preferred_element_type=F32) + bias
    m = jnp.max(s, axis=-1, keepdims=True)
    p = jnp.exp(s - m)
    l = jnp.sum(p, axis=-1, keepdims=True)
    o = _dot(p.astype(BF16), vw) / l
    lse = jnp.broadcast_to(m + jnp.log(l), o.shape)
    return (jnp.where(first_head, o[:BAND], o[BAND:]),
            jnp.where(first_head, lse[:BAND], lse[BAND:]))


def _attn_kernel(q1, k1, v1, q4, k4, v4, q16, k16, v16, bias_band_ref, bias_first_ref, o_ref, o_scr, l_scr):
    seq = q1.shape[0]
    first_head = lax.broadcasted_iota(I32, (BAND, LANES), 1) < HEAD_DIM
    branches = ((q1, k1, v1), (q4, k4, v4), (q16, k16, v16))
    for bi, dil in enumerate(DILATIONS):
        q_r, k_r, v_r = branches[bi]
        length = seq // dil
        for r in range(dil):
            for n in range(length // BAND):
                lo = n * BAND
                if bi == 0:
                    ref_slice = lambda ref, a, b: ref[a:b, :]
                else:
                    ref_slice = lambda ref, a, b, r=r: ref[0, r, a:b, :]
                qb = ref_slice(q_r, lo, lo + BAND)
                if n == 0:
                    kw, vw, bias = ref_slice(k_r, 0, BAND), ref_slice(v_r, 0, BAND), bias_first_ref[...]
                else:
                    kw, vw = ref_slice(k_r, lo - BAND, lo + BAND), ref_slice(v_r, lo - BAND, lo + BAND)
                    bias = bias_band_ref[...]
                o, lse = _attn_block(qb, kw, vw, bias, first_head)
                if dil == 1:
                    o_scr[bi, lo:lo + BAND, :] = o
                    l_scr[bi, lo:lo + BAND, :] = lse
                else:
                    dst = pl.ds(r + dil * lo, BAND, stride=dil)
                    o_scr[bi, dst, :] = o
                    l_scr[bi, dst, :] = lse
    lses = [l_scr[i] for i in range(3)]
    m = jnp.maximum(jnp.maximum(lses[0], lses[1]), lses[2])
    es = [jnp.exp(l - m) for l in lses]
    den = es[0] + es[1] + es[2]
    o_ref[...] = (es[0] / den) * o_scr[0] + (es[1] / den) * o_scr[1] + (es[2] / den) * o_scr[2]


def _attention(qkv, batch, seq):
    rel = (np.arange(BAND)[:, None] + BAND) - np.arange(2 * BAND)[None, :]
    band = np.where((rel >= 0) & (rel <= BAND), 0.0, NEG_INF).astype(np.float32)
    bias_band = jnp.asarray(np.concatenate([band, band], axis=0))
    bias_first = jnp.asarray(np.concatenate([band[:, BAND:], band[:, BAND:]], axis=0))
    nat = pl.BlockSpec((seq, LANES), lambda b, p: (b, p))
    cls = lambda dil: pl.BlockSpec((1, dil, seq // dil, LANES), lambda b, p: (b, 0, 0, p))
    full = lambda a: pl.BlockSpec(a.shape, lambda b, p: (0, 0))
    return pl.pallas_call(
        _attn_kernel,
        grid=(batch, B_WIDTH // LANES),
        in_specs=[nat] * 3 + [cls(4)] * 3 + [cls(16)] * 3 + [full(bias_band), full(bias_first)],
        out_specs=pl.BlockSpec((seq, LANES), lambda b, p: (b, p)),
        out_shape=jax.ShapeDtypeStruct((batch * seq, B_WIDTH), F32),
        scratch_shapes=[pltpu.VMEM((3, seq, LANES), F32), pltpu.VMEM((3, seq, LANES), F32)],
        compiler_params=pltpu.CompilerParams(dimension_semantics=("arbitrary", "arbitrary"),
                                             vmem_limit_bytes=VMEM_LIMIT),
        name="dilated_attention",
    )(*qkv, bias_band, bias_first)


def _mix_kernel(a_ref, bm_ref, x_ref, bog_ref, wout_ref, n2g_ref, wr_ref, br_ref,
                h_ref, hn_ref, idx_ref, gate_ref):
    tm = x_ref.shape[0]
    bn = _rms(bm_ref[...], bog_ref[...]).astype(BF16)
    mixed = jnp.concatenate([a_ref[...], bn], axis=1)
    h = x_ref[...] + _dot(mixed, wout_ref[...])
    h_ref[...] = h
    hn = _rms(h, n2g_ref[...])
    for j in range(ROW_TILES):
        hn_ref[pl.ds(j, tm, stride=ROW_TILES), :] = hn[:, j * LANES:(j + 1) * LANES]
    logits = _dot(hn.astype(BF16), wr_ref[...]) + br_ref[...]
    lane = lax.broadcasted_iota(I32, (tm, LANES), 1)
    vals, idxs = [], []
    for _ in range(TOP_K):
        m = jnp.max(logits, axis=-1, keepdims=True)
        am = jnp.min(jnp.where(logits == m, lane, LANES), axis=-1, keepdims=True)
        vals.append(m)
        idxs.append(am)
        logits = jnp.where(lane == am, -jnp.inf, logits)
    es = [jnp.exp(v - vals[0]) for v in vals]
    den = es[0] + es[1] + es[2] + es[3]
    idx_out = jnp.zeros((tm, LANES), I32)
    gate_out = jnp.zeros((tm, LANES), F32)
    for kk in range(TOP_K):
        idx_out = jnp.where(lane == kk, idxs[kk], idx_out)
        gate_out = jnp.where(lane == kk, es[kk] / den, gate_out)
    idx_ref[...] = idx_out
    gate_ref[...] = gate_out


def _mix(a_out, b_mix, x2, bog, w_out, n2g, w_r, b_r):
    t = x2.shape[0]
    tm = TM_PROJ
    full = lambda shape: pl.BlockSpec(shape, lambda i: (0,) * len(shape))
    rows = lambda w: pl.BlockSpec((tm, w), lambda i: (i, 0))
    return pl.pallas_call(
        _mix_kernel,
        grid=(t // tm,),
        in_specs=[rows(A_WIDTH), rows(B_WIDTH), rows(D_MODEL), full((1, B_WIDTH)), full(w_out.shape),
                  full((1, D_MODEL)), full(w_r.shape), full((1, LANES))],
        out_specs=[rows(D_MODEL), pl.BlockSpec((tm * ROW_TILES, LANES), lambda i: (i, 0)), rows(LANES), rows(LANES)],
        out_shape=[jax.ShapeDtypeStruct((t, D_MODEL), F32), jax.ShapeDtypeStruct((t * ROW_TILES, LANES), F32),
                   jax.ShapeDtypeStruct((t, LANES), I32), jax.ShapeDtypeStruct((t, LANES), F32)],
        compiler_params=pltpu.CompilerParams(dimension_semantics=("arbitrary",), vmem_limit_bytes=VMEM_LIMIT),
        name="mix_router",
    )(a_out, b_mix, x2, bog, w_out, n2g, w_r, b_r)


def _row_in(hn_hbm, buf, sem, src_row, r):
    return pltpu.make_async_copy(hn_hbm.at[pl.ds(pl.multiple_of(src_row, SUBLANES), SUBLANES), :],
                                 buf.at[pl.ds(pl.multiple_of(r * SUBLANES, SUBLANES), SUBLANES), :], sem)


def _row_out(buf, yk_hbm, sem, dst_row, r):
    return pltpu.make_async_copy(buf.at[pl.ds(pl.multiple_of(r * SUBLANES, SUBLANES), SUBLANES), :],
                                 yk_hbm.at[pl.ds(pl.multiple_of(dst_row, SUBLANES), SUBLANES), :], sem)


def _moe_kernel(te_ref, nt_ref, src_first_ref, src_next_ref, dst_prev_ref, hn_hbm, wgu_ref, bgu_ref, wd_ref,
                bd_ref, yk_hbm, xbuf0, xbuf1, ybuf0, ybuf1, wgu_bf, wd_bf, act_scr, gsem, ssem):
    tm = TM_MOE
    tile_rows = tm * ROW_TILES
    i = pl.program_id(0)
    nt = nt_ref[0]
    xbufs = (xbuf0, xbuf1)
    ybufs = (ybuf0, ybuf1)

    def wait_gather(p):
        pltpu.make_async_copy(hn_hbm.at[pl.ds(0, tile_rows), :], xbufs[p], gsem.at[p]).wait()

    def wait_scatter(p):
        pltpu.make_async_copy(ybufs[p], yk_hbm.at[pl.ds(0, tile_rows), :], ssem.at[p]).wait()

    def issue_loop(start_row):
        def body(g, c):
            for u in range(DMA_UNROLL):
                start_row(g * DMA_UNROLL + u, u % 2)
            return c
        lax.fori_loop(0, tm // DMA_UNROLL, body, 0)

    @pl.when(i == 0)
    def _():
        issue_loop(lambda r, pr: _row_in(hn_hbm, xbuf0, gsem.at[0], src_first_ref[0, 0, r], r).start(priority=pr))
        ybuf1[...] = jnp.zeros((tile_rows, LANES), F32)
        spare = pltpu.make_async_copy(ybuf1, yk_hbm.at[pl.ds(yk_hbm.shape[0] - 2 * tile_rows, tile_rows), :],
                                      ssem.at[1])
        spare.start()
        spare.wait()

    for p in (0, 1):
        q = 1 - p

        @pl.when((i < nt) & (i % 2 == p))
        def _(p=p, q=q):
            wait_gather(p)

            @pl.when((i == 0) | (te_ref[i] != te_ref[jnp.maximum(i - 1, 0)]))
            def _():
                wgu_bf[...] = wgu_ref[0].astype(BF16)
                wd_bf[...] = wd_ref[0].astype(BF16)

            def dma_group(g):
                for r in range(g * MOE_DMA_GROUP, (g + 1) * MOE_DMA_GROUP):
                    _row_in(hn_hbm, xbufs[q], gsem.at[q], src_next_ref[0, 0, r], r).start(priority=r % 2)
                    _row_out(ybufs[q], yk_hbm, ssem.at[q], dst_prev_ref[0, 0, r], r).start(priority=(r + 1) % 2)

            xs = xbufs[p]
            x = jnp.concatenate([xs[pl.ds(j, tm, stride=ROW_TILES), :] for j in range(ROW_TILES)],
                                axis=1).astype(BF16)
            width = D_FF // MOE_FF_CHUNKS
            dma_group(0)
            group = 1
            for c in range(MOE_FF_CHUNKS):
                lo, hi = c * width, (c + 1) * width
                dma_group(group)
                gate = _dot(x, wgu_bf[:, lo:hi]) + bgu_ref[0, :, lo:hi]
                dma_group(group + 1)
                up = _dot(x, wgu_bf[:, D_FF + lo:D_FF + hi]) + bgu_ref[0, :, D_FF + lo:D_FF + hi]
                dma_group(group + 2)
                gate = jnp.minimum(gate, SWIGLU_LIMIT)
                up = jnp.clip(up, -SWIGLU_LIMIT, SWIGLU_LIMIT)
                act_scr[:, lo:hi] = ((up + 1.0) * (gate * jax.nn.sigmoid(gate * SWIGLU_ALPHA))).astype(BF16)
                group += 3
            dma_group(group)
            assert (group + 1) * MOE_DMA_GROUP == tm
            y = _dot(act_scr[...], wd_bf[...]) + bd_ref[0]

            @pl.when(i >= 1)
            def _():
                wait_scatter(p)
            ys = ybufs[p]
            for j in range(ROW_TILES):
                ys[pl.ds(j, tm, stride=ROW_TILES), :] = y[:, j * LANES:(j + 1) * LANES]

    for p in (0, 1):
        q = 1 - p

        @pl.when((i == nt) & (i % 2 == p))
        def _(p=p, q=q):
            wait_gather(p)
            issue_loop(lambda r, pr: _row_out(ybufs[q], yk_hbm, ssem.at[q], dst_prev_ref[0, 0, r], r)
                       .start(priority=pr))
            wait_scatter(q)
            wait_scatter(p)


def _experts(tile_expert, n_tiles, src_rows, dst_prev_rows, hn_tiles, wgu, bgu, wd, bd, out_rows):
    nt_max = src_rows.shape[0]
    tm = TM_MOE
    smem = lambda f: pl.BlockSpec((1, 1, tm), f, memory_space=pltpu.SMEM)
    row_buf = pltpu.VMEM((tm * ROW_TILES, LANES), F32)
    grid_spec = pltpu.PrefetchScalarGridSpec(
        num_scalar_prefetch=2,
        grid=(nt_max,),
        in_specs=[smem(lambda i, te, nt: (0, 0, 0)),
                  smem(lambda i, te, nt: (jnp.minimum(i + 1, nt_max - 1), 0, 0)),
                  smem(lambda i, te, nt: (i, 0, 0)),
                  pl.BlockSpec(memory_space=pl.ANY),
                  pl.BlockSpec((1, D_MODEL, 2 * D_FF), lambda i, te, nt: (te[i], 0, 0)),
                  pl.BlockSpec((1, 1, 2 * D_FF), lambda i, te, nt: (te[i], 0, 0)),
                  pl.BlockSpec((1, D_FF, D_MODEL), lambda i, te, nt: (te[i], 0, 0)),
                  pl.BlockSpec((1, 1, D_MODEL), lambda i, te, nt: (te[i], 0, 0))],
        out_specs=pl.BlockSpec(memory_space=pl.ANY),
        scratch_shapes=[row_buf, row_buf, row_buf, row_buf,
                        pltpu.VMEM((D_MODEL, 2 * D_FF), BF16), pltpu.VMEM((D_FF, D_MODEL), BF16),
                        pltpu.VMEM((tm, D_FF), BF16),
                        pltpu.SemaphoreType.DMA((2,)), pltpu.SemaphoreType.DMA((2,))],
    )
    return pl.pallas_call(
        _moe_kernel,
        grid_spec=grid_spec,
        out_shape=jax.ShapeDtypeStruct((out_rows * ROW_TILES, LANES), F32),
        compiler_params=pltpu.CompilerParams(dimension_semantics=("arbitrary",), vmem_limit_bytes=VMEM_LIMIT),
        name="experts",
    )(tile_expert, n_tiles, src_rows, src_rows, dst_prev_rows, hn_tiles, wgu, bgu, wd, bd)


def _combine_kernel(yk0_ref, yk1_ref, yk2_ref, yk3_ref, gate_ref, h_ref, g_ref, o_ref, acc_scr):
    tm = h_ref.shape[0]
    yk_refs = (yk0_ref, yk1_ref, yk2_ref, yk3_ref)
    gates = gate_ref[...]
    gk = [jnp.broadcast_to(gates[:, kk:kk + 1], (tm, LANES)) for kk in range(TOP_K)]
    ss = jnp.zeros((tm, 1), F32)
    for j in range(ROW_TILES):
        cols = slice(j * LANES, (j + 1) * LANES)
        moe = gk[0] * yk_refs[0][pl.ds(j, tm, stride=ROW_TILES), :]
        for kk in range(1, TOP_K):
            moe = moe + gk[kk] * yk_refs[kk][pl.ds(j, tm, stride=ROW_TILES), :]
        acc = h_ref[:, cols] + moe
        acc_scr[:, cols] = acc
        ss = ss + jnp.sum(acc * acc, axis=-1, keepdims=True)
    inv = lax.rsqrt(ss / D_MODEL + NORM_EPS)
    o_ref[...] = acc_scr[...] * inv * g_ref[...]


def _combine(yk, gates, h, g):
    t = h.shape[0]
    tm = TM_COMB
    return pl.pallas_call(
        _combine_kernel,
        grid=(t // tm,),
        in_specs=[pl.BlockSpec((tm * ROW_TILES, LANES), functools.partial(lambda kk, i: (kk * (t // tm) + i, 0), kk))
                  for kk in range(TOP_K)] + [
                  pl.BlockSpec((tm, LANES), lambda i: (i, 0)),
                  pl.BlockSpec((tm, D_MODEL), lambda i: (i, 0)),
                  pl.BlockSpec((1, D_MODEL), lambda i: (0, 0))],
        out_specs=pl.BlockSpec((tm, D_MODEL), lambda i: (i, 0)),
        out_shape=jax.ShapeDtypeStruct((t, D_MODEL), F32),
        scratch_shapes=[pltpu.VMEM((tm, D_MODEL), F32)],
        compiler_params=pltpu.CompilerParams(dimension_semantics=("arbitrary",), vmem_limit_bytes=VMEM_LIMIT),
        name="combine",
    )(yk, yk, yk, yk, gates, h, g)


def _routing_plan(top_idx, n_tokens):
    tm = TM_MOE
    n_rows = n_tokens * TOP_K
    nt_max = n_rows // tm + N_EXPERTS + 1
    flat_e = top_idx.reshape(-1)
    experts = jnp.arange(N_EXPERTS, dtype=I32)
    lane = jnp.arange(tm, dtype=I32)[None, :]
    counts = jnp.sum((flat_e[:, None] == experts[None, :]).astype(I32), axis=0)
    tiles_e = (counts + tm - 1) // tm
    n_tiles = jnp.sum(tiles_e)
    span = n_rows + tm
    unused = N_EXPERTS * span
    real_keys = flat_e * span + jnp.arange(n_rows, dtype=I32)
    pad_keys = jnp.where(lane < (tiles_e * tm - counts)[:, None], experts[:, None] * span + n_rows + lane, unused)
    keys = jnp.sort(jnp.concatenate([real_keys, pad_keys.reshape(-1), jnp.full((tm,), unused, I32)]))
    keys = keys.reshape(nt_max, tm)
    flat = keys % span
    valid = (keys < unused) & (flat < n_rows)
    tile_ids = jnp.arange(nt_max, dtype=I32)
    te = jnp.minimum(keys[:, 0] // span, N_EXPERTS - 1)
    last_e = jnp.take(te, jnp.maximum(n_tiles - 1, 0))
    te = jnp.where(tile_ids < n_tiles, te, last_e).astype(I32)
    src = jnp.where(valid, flat // TOP_K, 0) * ROW_TILES
    spare = n_rows + (tile_ids % 2)[:, None] * tm + lane
    slot_major = (flat % TOP_K) * n_tokens + flat // TOP_K
    dst = jnp.where(valid, slot_major, spare) * ROW_TILES
    zero_tile = (n_rows + tm + lane) * ROW_TILES
    dst_prev = jnp.concatenate([zero_tile, dst[:-1]], axis=0)
    return (te, n_tiles.reshape(1).astype(I32), src.reshape(nt_max, 1, tm).astype(I32),
            dst_prev.reshape(nt_max, 1, tm).astype(I32))


def kernel(x, norm1_g, w_in, a_ln_g, a_ln_b, a_w_s, a_b_s, a_out_g, b_out_g, w_out, norm2_g, w_router,
           b_router, w_gate_up, b_gate_up, w_down, b_down, normf_g):
    batch, seq, _ = x.shape
    t = batch * seq
    assert seq % (TM_PROJ) == 0 and t % TM_MOE == 0 and seq // DILATIONS[-1] == BAND
    h = x.reshape(t, D_MODEL)

    pos = jnp.arange(seq, dtype=F32)
    inv = ROPE_THETA ** (-jnp.arange(0, HEAD_DIM, 2, dtype=F32) / HEAD_DIM)
    ang = pos[:, None] * inv[None, :]
    cos = jnp.tile(jnp.cos(ang), (1, 2 * LANES // HEAD_DIM))
    sin = jnp.tile(jnp.concatenate([-jnp.sin(ang), jnp.sin(ang)], axis=1), (1, LANES // HEAD_DIM))
    head_of_lane = np.arange(A_WIDTH) // HEAD_DIM
    avg = jnp.asarray((head_of_lane[:, None] == head_of_lane[None, :]).astype(np.float32) / HEAD_DIM, dtype=BF16)
    row2 = lambda v: v.reshape(1, -1).astype(F32)

    for layer in range(norm1_g.shape[0]):
        causal = np.tril(np.ones((CHUNK, CHUNK), dtype=bool))
        ws = jnp.where(causal[None], a_w_s[layer], 0.0).astype(BF16)
        bs = jnp.repeat(a_b_s[layer].astype(F32).T, HEAD_DIM, axis=1)
        a_out, *qkv = _inproj(h, row2(norm1_g[layer]), w_in[layer].astype(BF16), avg,
                              row2(a_ln_g[layer]), row2(a_ln_b[layer]), ws, bs, row2(a_out_g[layer]),
                              cos, sin, seq)
        b_mix = _attention(qkv, batch, seq)
        w_r = jnp.pad(w_router[layer], ((0, 0), (0, LANES - N_EXPERTS))).astype(BF16)
        b_r = jnp.concatenate([b_router[layer].astype(F32), jnp.full((LANES - N_EXPERTS,), NEG_INF, F32)])
        h_mid, hn, idx, gates = _mix(a_out, b_mix, h, row2(b_out_g[layer]), w_out[layer].astype(BF16),
                                     row2(norm2_g[layer]), w_r, b_r.reshape(1, LANES))
        te, n_tiles, src, dst = _routing_plan(idx[:, :TOP_K], t)
        yk = _experts(te, n_tiles, src, dst, hn,
                      w_gate_up[layer], b_gate_up[layer].reshape(N_EXPERTS, 1, 2 * D_FF),
                      w_down[layer], b_down[layer].reshape(N_EXPERTS, 1, D_MODEL),
                      t * TOP_K + 2 * TM_MOE)
        last = layer == norm1_g.shape[0] - 1
        assert last, "the combine kernel fuses the final norm; depth > 1 is not supported"
        h = _combine(yk, gates, h_mid, row2(normf_g))
    return h.reshape(batch, seq, D_MODEL)
```

```python
import functools
import math

import numpy as np
import jax
import jax.numpy as jnp
from jax import lax
from jax.experimental import pallas as pl
from jax.experimental.pallas import tpu as pltpu

F32 = jnp.float32
BF16 = jnp.bfloat16
I32 = jnp.int32

D_MODEL = 1024
HEAD_DIM = 64
A_WIDTH = 512
B_WIDTH = 512
CHUNK = 128
BAND = 128
DILATIONS = (1, 4, 16)
ROPE_THETA = 10000.0
N_EXPERTS = 32
TOP_K = 4
D_FF = 1024
SWIGLU_ALPHA = 1.702
SWIGLU_LIMIT = 7.0
NORM_EPS = 1e-5
NEG_INF = -1e30

LANES = 128
SUBLANES = 8
ROW_TILES = D_MODEL // LANES

TM_PROJ = 512
PROJ_SUBTILES = 1
TM_MOE = 512
TM_COMB = 512
DMA_UNROLL = 16
MOE_FF_CHUNKS = 2
IDLE_TILES = 2
VMEM_LIMIT = 56 * 1024 * 1024


def _dot(a, b):
    return jnp.dot(a, b, preferred_element_type=F32)


def _gelu_tanh(x):
    c = math.sqrt(2.0 / math.pi)
    cdf = 0.5 * (1.0 + jnp.tanh(c * (x + 0.044715 * (x * x * x))))
    return x * cdf


def _rms(x, g):
    return x * lax.rsqrt(jnp.mean(x * x, axis=-1, keepdims=True) + NORM_EPS) * g


def _inproj_kernel(x_ref, g1_ref, w_ref, avg_ref, lng_ref, lnb_ref, ws_ref, bs_ref, aog_ref,
                   cos_ref, sin_ref, a_ref, q_ref, k_ref, v_ref, q4_ref, k4_ref, v4_ref,
                   q16_ref, k16_ref, v16_ref, a_scr, qkv_scr, cls_scr):
    tm = x_ref.shape[0]
    n = tm // PROJ_SUBTILES
    nlt = B_WIDTH // LANES
    d4, d16 = DILATIONS[1], DILATIONS[2]
    first_head = lax.broadcasted_iota(I32, (CHUNK, LANES), 1) < HEAD_DIM
    first_half = (lax.broadcasted_iota(I32, (n, LANES), 1) % HEAD_DIM) < (HEAD_DIM // 2)
    avg = avg_ref[...]

    for sub in range(PROJ_SUBTILES):
        r0 = sub * n
        rows_n = slice(r0, r0 + n)
        xn = _rms(x_ref[rows_n, :], g1_ref[...]).astype(BF16)

        ug = _gelu_tanh(_dot(xn, w_ref[:, 0:A_WIDTH]))
        vg = _gelu_tanh(_dot(xn, w_ref[:, A_WIDTH:2 * A_WIDTH]))
        mu = _dot(vg.astype(BF16), avg)
        d = vg - mu
        var = _dot((d * d).astype(BF16), avg)
        vn = (d * lax.rsqrt(var + NORM_EPS) * lng_ref[...] + lnb_ref[...]).astype(BF16)
        for c in range(n // CHUNK):
            rows = slice(c * CHUNK, (c + 1) * CHUNK)
            for p in range(A_WIDTH // LANES):
                cols = slice(p * LANES, (p + 1) * LANES)
                slab = vn[rows, cols]
                g = jnp.where(first_head, _dot(ws_ref[2 * p], slab), _dot(ws_ref[2 * p + 1], slab))
                a_scr[r0 + c * CHUNK:r0 + (c + 1) * CHUNK, cols] = ug[rows, cols] * (g + bs_ref[:, cols])
        a_ref[rows_n, :] = _rms(a_scr[rows_n, :], aog_ref[...]).astype(BF16)

        cos = cos_ref[rows_n, :]
        sin = sin_ref[rows_n, :]

        def rope(t):
            rot = jnp.where(first_half, pltpu.roll(t, LANES - HEAD_DIM // 2, 1), pltpu.roll(t, HEAD_DIM // 2, 1))
            return t * cos + rot * sin

        off = 2 * A_WIDTH
        for p in range(nlt):
            q = _dot(xn, w_ref[:, off + p * LANES: off + (p + 1) * LANES])
            qkv_scr[p, rows_n, :] = rope(q) * (HEAD_DIM ** -0.5)
            k = _dot(xn, w_ref[:, off + B_WIDTH + p * LANES: off + B_WIDTH + (p + 1) * LANES])
            qkv_scr[nlt + p, rows_n, :] = rope(k)
            qkv_scr[2 * nlt + p, rows_n, :] = _dot(xn, w_ref[:, off + 2 * B_WIDTH + p * LANES:
                                                             off + 2 * B_WIDTH + (p + 1) * LANES])

        rows4 = slice(r0 // d4, (r0 + n) // d4)
        rows16 = slice(r0 // d16, (r0 + n) // d16)
        for i, (nat_ref, c4_ref, c16_ref) in enumerate(((q_ref, q4_ref, q16_ref), (k_ref, k4_ref, k16_ref),
                                                        (v_ref, v4_ref, v16_ref))):
            for p in range(nlt):
                cols = slice(p * LANES, (p + 1) * LANES)
                nat_ref[rows_n, cols] = qkv_scr[i * nlt + p, rows_n, :].astype(BF16)
                for r4 in range(d4):
                    cls = qkv_scr[i * nlt + p, pl.ds(r0 + r4, n // d4, stride=d4), :]
                    c4_ref[0, r4, rows4, cols] = cls.astype(BF16)
                    cls_scr[i * nlt + p, r4, rows4, :] = cls
                for r4 in range(d4):
                    for a in range(d16 // d4):
                        c16_ref[0, r4 + d4 * a, rows16, cols] = cls_scr[
                            i * nlt + p, r4, pl.ds(r0 // d4 + a, n // d16, stride=d16 // d4), :].astype(BF16)


def _inproj(x2, g1, w_in, avg, lng, lnb, ws, bs, aog, cos, sin, seq):
    t = x2.shape[0]
    tm = TM_PROJ
    nseq = seq // tm
    full = lambda shape: pl.BlockSpec(shape, lambda i: (0,) * len(shape))
    rows = lambda w: pl.BlockSpec((tm, w), lambda i: (i, 0))
    classes = lambda dil: pl.BlockSpec((1, dil, tm // dil, B_WIDTH), lambda i: (i // nseq, 0, i % nseq, 0))
    class_shape = lambda dil: jax.ShapeDtypeStruct((t // seq, dil, seq // dil, B_WIDTH), BF16)
    return pl.pallas_call(
        _inproj_kernel,
        grid=(t // tm,),
        in_specs=[rows(D_MODEL), full((1, D_MODEL)), full(w_in.shape), full(avg.shape),
                  full((1, A_WIDTH)), full((1, A_WIDTH)), full(ws.shape), full(bs.shape),
                  full((1, A_WIDTH)),
                  pl.BlockSpec((tm, LANES), lambda i: (i % nseq, 0)),
                  pl.BlockSpec((tm, LANES), lambda i: (i % nseq, 0))],
        out_specs=[rows(A_WIDTH)] + [rows(B_WIDTH)] * 3 + [classes(DILATIONS[1])] * 3 + [classes(DILATIONS[2])] * 3,
        out_shape=([jax.ShapeDtypeStruct((t, A_WIDTH), BF16)] + [jax.ShapeDtypeStruct((t, B_WIDTH), BF16)] * 3
                   + [class_shape(DILATIONS[1])] * 3 + [class_shape(DILATIONS[2])] * 3),
        scratch_shapes=[pltpu.VMEM((tm, A_WIDTH), F32), pltpu.VMEM((3 * B_WIDTH // LANES, tm, LANES), F32),
                        pltpu.VMEM((3 * B_WIDTH // LANES, DILATIONS[1], tm // DILATIONS[1], LANES), F32)],
        compiler_params=pltpu.CompilerParams(dimension_semantics=("arbitrary",), vmem_limit_bytes=VMEM_LIMIT),
        name="inproj",
    )(x2, g1, w_in, avg, lng, lnb, ws, bs, aog, cos, sin)


def _attn_block(qb, kw, vw, bias, first_head):
    zero = jnp.zeros_like(qb)
    q2 = jnp.concatenate([jnp.where(first_head, qb, zero), jnp.where(first_head, zero, qb)], axis=0)
    s = lax.dot_general(q2, kw, (((1,), (1,)), ((), ())), preferred_element_type=F32) + bias
    m = jnp.max(s, axis=-1, keepdims=True)
    p = jnp.exp(s - m)
    l = jnp.sum(p, axis=-1, keepdims=True)
    o = _dot(p.astype(BF16), vw) / l
    lse = jnp.broadcast_to(m + jnp.log(l), o.shape)
    return (jnp.where(first_head, o[:BAND], o[BAND:]),
            jnp.where(first_head, lse[:BAND], lse[BAND:]))


def _attn_kernel(q1, k1, v1, q4, k4, v4, q16, k16, v16, bias_band_ref, bias_first_ref, o_ref, o_scr, l_scr):
    seq = q1.shape[0]
    first_head = lax.broadcasted_iota(I32, (BAND, LANES), 1) < HEAD_DIM
    branches = ((q1, k1, v1), (q4, k4, v4), (q16, k16, v16))
    for bi, dil in enumerate(DILATIONS):
        q_r, k_r, v_r = branches[bi]
        length = seq // dil
        for r in range(dil):
            for n in range(length // BAND):
                lo = n * BAND
                if bi == 0:
                    ref_slice = lambda ref, a, b: ref[a:b, :]
                else:
                    ref_slice = lambda ref, a, b, r=r: ref[0, r, a:b, :]
                qb = ref_slice(q_r, lo, lo + BAND)
                if n == 0:
                    kw, vw, bias = ref_slice(k_r, 0, BAND), ref_slice(v_r, 0, BAND), bias_first_ref[...]
                else:
                    kw, vw = ref_slice(k_r, lo - BAND, lo + BAND), ref_slice(v_r, lo - BAND, lo + BAND)
                    bias = bias_band_ref[...]
                o, lse = _attn_block(qb, kw, vw, bias, first_head)
                if dil == 1:
                    o_scr[bi, lo:lo + BAND, :] = o
                    l_scr[bi, lo:lo + BAND, :] = lse
                else:
                    dst = pl.ds(r + dil * lo, BAND, stride=dil)
                    o_scr[bi, dst, :] = o
                    l_scr[bi, dst, :] = lse
    lses = [l_scr[i] for i in range(3)]
    m = jnp.maximum(jnp.maximum(lses[0], lses[1]), lses[2])
    es = [jnp.exp(l - m) for l in lses]
    den = es[0] + es[1] + es[2]
    o_ref[...] = (es[0] / den) * o_scr[0] + (es[1] / den) * o_scr[1] + (es[2] / den) * o_scr[2]


def _attention(qkv, batch, seq):
    rel = (np.arange(BAND)[:, None] + BAND) - np.arange(2 * BAND)[None, :]
    band = np.where((rel >= 0) & (rel <= BAND), 0.0, NEG_INF).astype(np.float32)
    bias_band = jnp.asarray(np.concatenate([band, band], axis=0))
    bias_first = jnp.asarray(np.concatenate([band[:, BAND:], band[:, BAND:]], axis=0))
    nat = pl.BlockSpec((seq, LANES), lambda b, p: (b, p))
    cls = lambda dil: pl.BlockSpec((1, dil, seq // dil, LANES), lambda b, p: (b, 0, 0, p))
    full = lambda a: pl.BlockSpec(a.shape, lambda b, p: (0, 0))
    return pl.pallas_call(
        _attn_kernel,
        grid=(batch, B_WIDTH // LANES),
        in_specs=[nat] * 3 + [cls(4)] * 3 + [cls(16)] * 3 + [full(bias_band), full(bias_first)],
        out_specs=pl.BlockSpec((seq, LANES), lambda b, p: (b, p)),
        out_shape=jax.ShapeDtypeStruct((batch * seq, B_WIDTH), F32),
        scratch_shapes=[pltpu.VMEM((3, seq, LANES), F32), pltpu.VMEM((3, seq, LANES), F32)],
        compiler_params=pltpu.CompilerParams(dimension_semantics=("arbitrary", "arbitrary"),
                                             vmem_limit_bytes=VMEM_LIMIT),
        name="dilated_attention",
    )(*qkv, bias_band, bias_first)


def _mix_kernel(a_ref, bm_ref, x_ref, bog_ref, wout_ref, n2g_ref, wr_ref, br_ref, tri_ref,
                h_ref, hn_ref, idx_ref, gate_ref, cnt_ref, cnt_scr):
    tm = x_ref.shape[0]
    bn = _rms(bm_ref[...], bog_ref[...]).astype(BF16)
    mixed = jnp.concatenate([a_ref[...], bn], axis=1)
    h = x_ref[...] + _dot(mixed, wout_ref[...])
    h_ref[...] = h
    hn = _rms(h, n2g_ref[...])
    for j in range(ROW_TILES):
        hn_ref[pl.ds(j, tm, stride=ROW_TILES), :] = hn[:, j * LANES:(j + 1) * LANES]
    logits = _dot(hn.astype(BF16), wr_ref[...]) + br_ref[...]
    lane = lax.broadcasted_iota(I32, (tm, LANES), 1)
    vals, idxs = [], []
    for _ in range(TOP_K):
        m = jnp.max(logits, axis=-1, keepdims=True)
        am = jnp.min(jnp.where(logits == m, lane, LANES), axis=-1, keepdims=True)
        vals.append(m)
        idxs.append(am)
        logits = jnp.where(lane == am, -jnp.inf, logits)
    es = [jnp.exp(v - vals[0]) for v in vals]
    den = es[0] + es[1] + es[2] + es[3]
    @pl.when(pl.program_id(0) == 0)
    def _():
        cnt_scr[...] = jnp.zeros_like(cnt_scr)

    chosen = jnp.zeros((tm, LANES), F32)
    for kk in range(TOP_K):
        chosen = chosen + (lane == idxs[kk]).astype(F32)
    before = _dot(tri_ref[...], chosen.astype(BF16)) + cnt_scr[...]
    cnt_scr[...] = cnt_scr[...] + jnp.sum(chosen, axis=0, keepdims=True)
    cnt_ref[...] = jnp.broadcast_to(cnt_scr[...], cnt_ref.shape)

    idx_out = jnp.zeros((tm, LANES), I32)
    gate_out = jnp.zeros((tm, LANES), F32)
    for kk in range(TOP_K):
        rank = jnp.sum(jnp.where(lane == idxs[kk], before, 0.0), axis=-1, keepdims=True).astype(I32)
        idx_out = jnp.where(lane == kk, idxs[kk], idx_out)
        idx_out = jnp.where(lane == TOP_K + kk, rank, idx_out)
        gate_out = jnp.where(lane == kk, es[kk] / den, gate_out)
    idx_ref[...] = idx_out
    gate_ref[...] = gate_out


def _mix(a_out, b_mix, x2, bog, w_out, n2g, w_r, b_r):
    t = x2.shape[0]
    tm = TM_PROJ
    full = lambda shape: pl.BlockSpec(shape, lambda i: (0,) * len(shape))
    rows = lambda w: pl.BlockSpec((tm, w), lambda i: (i, 0))
    tri = jnp.asarray(np.tril(np.ones((tm, tm), np.float32), -1), dtype=BF16)
    return pl.pallas_call(
        _mix_kernel,
        grid=(t // tm,),
        in_specs=[rows(A_WIDTH), rows(B_WIDTH), rows(D_MODEL), full((1, B_WIDTH)), full(w_out.shape),
                  full((1, D_MODEL)), full(w_r.shape), full((1, LANES)), full((tm, tm))],
        out_specs=[rows(D_MODEL), pl.BlockSpec((tm * ROW_TILES, LANES), lambda i: (i, 0)), rows(LANES), rows(LANES),
                   full((SUBLANES, LANES))],
        out_shape=[jax.ShapeDtypeStruct((t, D_MODEL), F32), jax.ShapeDtypeStruct((t * ROW_TILES, LANES), F32),
                   jax.ShapeDtypeStruct((t, LANES), I32), jax.ShapeDtypeStruct((t, LANES), F32),
                   jax.ShapeDtypeStruct((SUBLANES, LANES), F32)],
        scratch_shapes=[pltpu.VMEM((1, LANES), F32)],
        compiler_params=pltpu.CompilerParams(dimension_semantics=("arbitrary",), vmem_limit_bytes=VMEM_LIMIT),
        name="mix_router",
    )(a_out, b_mix, x2, bog, w_out, n2g, w_r, b_r, tri)


def _tile_rows(ref, row):
    return ref.at[pl.ds(pl.multiple_of(row * ROW_TILES, SUBLANES), ROW_TILES), :]


def _dispatch_kernel(pos_ref, pad_ref, hn_ref, xs_hbm, zero_scr, sem):
    n_rows = pos_ref.shape[-1]
    n_pad = pad_ref.shape[-1]

    @pl.when(pl.program_id(0) == 0)
    def _():
        zero_scr[...] = jnp.zeros_like(zero_scr)

    def body(g, c):
        for u in range(DMA_UNROLL):
            f = g * DMA_UNROLL + u
            token = g * (DMA_UNROLL // TOP_K) + u // TOP_K
            pltpu.make_async_copy(_tile_rows(hn_ref, token), _tile_rows(xs_hbm, pos_ref[0, 0, f]),
                                  sem).start(priority=u % 2)
        return c
    lax.fori_loop(0, n_rows // DMA_UNROLL, body, 0)

    def pad_body(g, c):
        for u in range(DMA_UNROLL):
            pltpu.make_async_copy(zero_scr, _tile_rows(xs_hbm, pad_ref[0, 0, g * DMA_UNROLL + u]),
                                  sem).start(priority=u % 2)
        return c
    lax.fori_loop(0, n_pad // DMA_UNROLL, pad_body, 0)

    total = (n_rows + n_pad) * ROW_TILES
    pltpu.make_async_copy(xs_hbm.at[pl.ds(0, total), :], xs_hbm.at[pl.ds(0, total), :], sem).wait()


def _dispatch(pos, pad_pos, hn_tiles, n_slots):
    t = hn_tiles.shape[0] // ROW_TILES
    tm = TM_PROJ
    steps = t // tm
    n_pad = pad_pos.shape[0] // steps
    return pl.pallas_call(
        _dispatch_kernel,
        grid=(steps,),
        in_specs=[pl.BlockSpec((1, 1, tm * TOP_K), lambda i: (i, 0, 0), memory_space=pltpu.SMEM),
                  pl.BlockSpec((1, 1, n_pad), lambda i: (i, 0, 0), memory_space=pltpu.SMEM),
                  pl.BlockSpec((tm * ROW_TILES, LANES), lambda i: (i, 0))],
        out_specs=pl.BlockSpec(memory_space=pl.ANY),
        out_shape=jax.ShapeDtypeStruct((n_slots * ROW_TILES, LANES), F32),
        scratch_shapes=[pltpu.VMEM((ROW_TILES, LANES), F32), pltpu.SemaphoreType.DMA],
        compiler_params=pltpu.CompilerParams(dimension_semantics=("arbitrary",), vmem_limit_bytes=VMEM_LIMIT),
        name="dispatch",
    )(pos.reshape(steps, 1, tm * TOP_K), pad_pos.reshape(steps, 1, n_pad), hn_tiles)


def _moe_kernel(te_ref, nt_ref, x_ref, wgu_ref, bgu_ref, wd_ref, bd_ref, y_ref, wgu_bf, wd_bf, act_scr):
    tm = TM_MOE
    i = pl.program_id(0)

    @pl.when(i < nt_ref[0])
    def _():
        @pl.when((i == 0) | (te_ref[i] != te_ref[jnp.maximum(i - 1, 0)]))
        def _():
            wgu_bf[...] = wgu_ref[0].astype(BF16)
            wd_bf[...] = wd_ref[0].astype(BF16)

        x = jnp.concatenate([x_ref[pl.ds(j, tm, stride=ROW_TILES), :] for j in range(ROW_TILES)],
                            axis=1).astype(BF16)
        width = D_FF // MOE_FF_CHUNKS
        for c in range(MOE_FF_CHUNKS):
            lo, hi = c * width, (c + 1) * width
            gate = _dot(x, wgu_bf[:, lo:hi]) + bgu_ref[0, :, lo:hi]
            up = _dot(x, wgu_bf[:, D_FF + lo:D_FF + hi]) + bgu_ref[0, :, D_FF + lo:D_FF + hi]
            gate = jnp.minimum(gate, SWIGLU_LIMIT)
            up = jnp.clip(up, -SWIGLU_LIMIT, SWIGLU_LIMIT)
            act_scr[:, lo:hi] = ((up + 1.0) * (gate * jax.nn.sigmoid(gate * SWIGLU_ALPHA))).astype(BF16)
        y = _dot(act_scr[...], wd_bf[...]) + bd_ref[0]
        for j in range(ROW_TILES):
            y_ref[pl.ds(j, tm, stride=ROW_TILES), :] = y[:, j * LANES:(j + 1) * LANES]

    @pl.when(i >= nt_ref[0])
    def _():
        y_ref[...] = jnp.zeros_like(y_ref)


def _experts(tile_expert, n_tiles, xs, wgu, bgu, wd, bd, nt_max):
    tm = TM_MOE
    tile_in = pl.BlockSpec((tm * ROW_TILES, LANES), lambda i, te, nt: (jnp.minimum(i, nt[0] - 1), 0))
    tile = pl.BlockSpec((tm * ROW_TILES, LANES), lambda i, te, nt: (i, 0))
    grid_spec = pltpu.PrefetchScalarGridSpec(
        num_scalar_prefetch=2,
        grid=(nt_max,),
        in_specs=[tile_in,
                  pl.BlockSpec((1, D_MODEL, 2 * D_FF), lambda i, te, nt: (te[i], 0, 0)),
                  pl.BlockSpec((1, 1, 2 * D_FF), lambda i, te, nt: (te[i], 0, 0)),
                  pl.BlockSpec((1, D_FF, D_MODEL), lambda i, te, nt: (te[i], 0, 0)),
                  pl.BlockSpec((1, 1, D_MODEL), lambda i, te, nt: (te[i], 0, 0))],
        out_specs=tile,
        scratch_shapes=[pltpu.VMEM((D_MODEL, 2 * D_FF), BF16), pltpu.VMEM((D_FF, D_MODEL), BF16),
                        pltpu.VMEM((tm, D_FF), BF16)],
    )
    return pl.pallas_call(
        _moe_kernel,
        grid_spec=grid_spec,
        out_shape=jax.ShapeDtypeStruct((nt_max * tm * ROW_TILES, LANES), F32),
        compiler_params=pltpu.CompilerParams(dimension_semantics=("arbitrary",), vmem_limit_bytes=VMEM_LIMIT),
        name="experts",
    )(tile_expert, n_tiles, xs, wgu, bgu, wd, bd)


def _combine_kernel(pos_ref, pos_next_ref, ys_hbm, gate_ref, h_ref, g_ref, o_ref, buf0, buf1, acc_scr, sem):
    tm = h_ref.shape[0]
    n_rows = tm * TOP_K
    i = pl.program_id(0)
    bufs = (buf0, buf1)

    def gather(idx_ref, buf, s):
        def body(g, c):
            for u in range(DMA_UNROLL):
                f = g * DMA_UNROLL + u
                token = g * (DMA_UNROLL // TOP_K) + u // TOP_K
                pltpu.make_async_copy(_tile_rows(ys_hbm, idx_ref[0, 0, f]),
                                      _tile_rows(buf, (u % TOP_K) * tm + token), sem.at[s]).start(priority=u % 2)
            return c
        lax.fori_loop(0, n_rows // DMA_UNROLL, body, 0)

    @pl.when(i == 0)
    def _():
        gather(pos_ref, buf0, 0)

    for p in (0, 1):
        @pl.when(i % 2 == p)
        def _(p=p):
            @pl.when(i + 1 < pl.num_programs(0))
            def _():
                gather(pos_next_ref, bufs[1 - p], 1 - p)

            buf = bufs[p]
            pltpu.make_async_copy(ys_hbm.at[pl.ds(0, n_rows * ROW_TILES), :], buf, sem.at[p]).wait()
            gates = gate_ref[...]
            gk = [jnp.broadcast_to(gates[:, kk:kk + 1], (tm, LANES)) for kk in range(TOP_K)]
            ss = jnp.zeros((tm, 1), F32)
            for j in range(ROW_TILES):
                cols = slice(j * LANES, (j + 1) * LANES)
                moe = gk[0] * buf[pl.ds(j, tm, stride=ROW_TILES), :]
                for kk in range(1, TOP_K):
                    moe = moe + gk[kk] * buf[pl.ds(kk * tm * ROW_TILES + j, tm, stride=ROW_TILES), :]
                acc = h_ref[:, cols] + moe
                acc_scr[:, cols] = acc
                ss = ss + jnp.sum(acc * acc, axis=-1, keepdims=True)
            inv = lax.rsqrt(ss / D_MODEL + NORM_EPS)
            o_ref[...] = acc_scr[...] * inv * g_ref[...]


def _combine(pos, ys, gates, h, g):
    t = h.shape[0]
    tm = TM_COMB
    steps = t // tm
    pos_blocks = pos.reshape(steps, 1, tm * TOP_K)
    smem = lambda f: pl.BlockSpec((1, 1, tm * TOP_K), f, memory_space=pltpu.SMEM)
    row_buf = pltpu.VMEM((TOP_K * tm * ROW_TILES, LANES), F32)
    return pl.pallas_call(
        _combine_kernel,
        grid=(steps,),
        in_specs=[smem(lambda i: (i, 0, 0)),
                  smem(lambda i: (jnp.minimum(i + 1, steps - 1), 0, 0)),
                  pl.BlockSpec(memory_space=pl.ANY),
                  pl.BlockSpec((tm, LANES), lambda i: (i, 0)),
                  pl.BlockSpec((tm, D_MODEL), lambda i: (i, 0)),
                  pl.BlockSpec((1, D_MODEL), lambda i: (0, 0))],
        out_specs=pl.BlockSpec((tm, D_MODEL), lambda i: (i, 0)),
        out_shape=jax.ShapeDtypeStruct((t, D_MODEL), F32),
        scratch_shapes=[row_buf, row_buf, pltpu.VMEM((tm, D_MODEL), F32), pltpu.SemaphoreType.DMA((2,))],
        compiler_params=pltpu.CompilerParams(dimension_semantics=("arbitrary",), vmem_limit_bytes=VMEM_LIMIT),
        name="combine",
    )(pos_blocks, pos_blocks, ys, gates, h, g)


def _routing_plan(idx_rank, counts_f, n_tokens):
    tm = TM_MOE
    nt_max = n_tokens * TOP_K // tm + N_EXPERTS + IDLE_TILES
    experts = jnp.arange(N_EXPERTS, dtype=I32)
    counts = counts_f[0, :N_EXPERTS].astype(I32)
    tiles_e = (counts + tm - 1) // tm
    tile_end = jnp.cumsum(tiles_e)
    first_slot = (tile_end - tiles_e) * tm
    n_tiles = tile_end[-1]
    tile_ids = jnp.arange(nt_max, dtype=I32)
    te = jnp.minimum(jnp.sum((tile_ids[:, None] >= tile_end[None, :]).astype(I32), axis=1), N_EXPERTS - 1)
    last_e = jnp.take(te, jnp.maximum(n_tiles - 1, 0))
    te = jnp.where(tile_ids < n_tiles, te, last_e).astype(I32)
    expert = idx_rank[:, :TOP_K]
    rank = idx_rank[:, TOP_K:2 * TOP_K]
    pos = jnp.sum(jnp.where(expert[:, :, None] == experts[None, None, :], first_slot[None, None, :], 0), axis=-1) + rank
    lane = jnp.arange(tm, dtype=I32)[None, :]
    pad_e = jnp.concatenate([tiles_e * tm - counts, jnp.zeros((IDLE_TILES,), I32)])[:, None]
    pad_first = jnp.concatenate([first_slot + counts, jnp.zeros((IDLE_TILES,), I32)])[:, None]
    is_pad = lane < pad_e
    idle_rank = (jnp.cumsum((~is_pad).reshape(-1).astype(I32)) - 1).reshape(is_pad.shape)
    pad_pos = jnp.where(is_pad, pad_first + lane, n_tiles * tm + idle_rank)
    return te, n_tiles.reshape(1).astype(I32), pos.reshape(-1).astype(I32), pad_pos.reshape(-1).astype(I32), nt_max


def kernel(x, norm1_g, w_in, a_ln_g, a_ln_b, a_w_s, a_b_s, a_out_g, b_out_g, w_out, norm2_g, w_router,
           b_router, w_gate_up, b_gate_up, w_down, b_down, normf_g):
    batch, seq, _ = x.shape
    t = batch * seq
    assert seq % (TM_PROJ) == 0 and t % TM_MOE == 0 and seq // DILATIONS[-1] == BAND
    h = x.reshape(t, D_MODEL)

    pos = jnp.arange(seq, dtype=F32)
    inv = ROPE_THETA ** (-jnp.arange(0, HEAD_DIM, 2, dtype=F32) / HEAD_DIM)
    ang = pos[:, None] * inv[None, :]
    cos = jnp.tile(jnp.cos(ang), (1, 2 * LANES // HEAD_DIM))
    sin = jnp.tile(jnp.concatenate([-jnp.sin(ang), jnp.sin(ang)], axis=1), (1, LANES // HEAD_DIM))
    head_of_lane = np.arange(A_WIDTH) // HEAD_DIM
    avg = jnp.asarray((head_of_lane[:, None] == head_of_lane[None, :]).astype(np.float32) / HEAD_DIM, dtype=BF16)
    row2 = lambda v: v.reshape(1, -1).astype(F32)

    for layer in range(norm1_g.shape[0]):
        causal = np.tril(np.ones((CHUNK, CHUNK), dtype=bool))
        ws = jnp.where(causal[None], a_w_s[layer], 0.0).astype(BF16)
        bs = jnp.repeat(a_b_s[layer].astype(F32).T, HEAD_DIM, axis=1)
        a_out, *qkv = _inproj(h, row2(norm1_g[layer]), w_in[layer].astype(BF16), avg,
                              row2(a_ln_g[layer]), row2(a_ln_b[layer]), ws, bs, row2(a_out_g[layer]),
                              cos, sin, seq)
        b_mix = _attention(qkv, batch, seq)
        w_r = jnp.pad(w_router[layer], ((0, 0), (0, LANES - N_EXPERTS))).astype(BF16)
        b_r = jnp.concatenate([b_router[layer].astype(F32), jnp.full((LANES - N_EXPERTS,), NEG_INF, F32)])
        h_mid, hn, idx_rank, gates, counts = _mix(a_out, b_mix, h, row2(b_out_g[layer]), w_out[layer].astype(BF16),
                                                  row2(norm2_g[layer]), w_r, b_r.reshape(1, LANES))
        te, n_tiles, pos, pad_pos, nt_max = _routing_plan(idx_rank, counts, t)
        xs = _dispatch(pos, pad_pos, hn, nt_max * TM_MOE)
        ys = _experts(te, n_tiles, xs,
                      w_gate_up[layer], b_gate_up[layer].reshape(N_EXPERTS, 1, 2 * D_FF),
                      w_down[layer], b_down[layer].reshape(N_EXPERTS, 1, D_MODEL), nt_max)
        last = layer == norm1_g.shape[0] - 1
        assert last, "the combine kernel fuses the final norm; depth > 1 is not supported"
        h = _combine(pos, ys, gates, h_mid, row2(normf_g))
    return h.reshape(batch, seq, D_MODEL)
```

```python
import math

import numpy as np
import jax
import jax.numpy as jnp
from jax import lax
from jax.experimental import pallas as pl
from jax.experimental.pallas import tpu as pltpu

F32 = jnp.float32
BF16 = jnp.bfloat16
I32 = jnp.int32

D_MODEL = 1024
HEAD_DIM = 64
A_WIDTH = 512
B_WIDTH = 512
CHUNK = 128
BAND = 128
DILATIONS = (1, 4, 16)
ROPE_THETA = 10000.0
N_EXPERTS = 32
TOP_K = 4
D_FF = 1024
SWIGLU_ALPHA = 1.702
SWIGLU_LIMIT = 7.0
NORM_EPS = 1e-5
NEG_INF = -1e30

LANES = 128
SUBLANES = 8
ROW_TILES = D_MODEL // LANES

TM_PROJ = 512
PROJ_SUBTILES = 1
TM_MOE = 512
TM_COMB = 512
DMA_UNROLL = 16
MOE_FF_CHUNKS = 2
IDLE_TILES = 2
VMEM_LIMIT = 56 * 1024 * 1024


def _dot(a, b):
    return jnp.dot(a, b, preferred_element_type=F32)


def _gelu_tanh(x):
    c = math.sqrt(2.0 / math.pi)
    cdf = 0.5 * (1.0 + jnp.tanh(c * (x + 0.044715 * (x * x * x))))
    return x * cdf


def _rms(x, g):
    return x * lax.rsqrt(jnp.mean(x * x, axis=-1, keepdims=True) + NORM_EPS) * g


def _inproj_kernel(x_ref, g1_ref, w_ref, avg_ref, lng_ref, lnb_ref, ws_ref, bs_ref, aog_ref,
                   cos_ref, sin_ref, a_ref, q_ref, k_ref, v_ref, q4_ref, k4_ref, v4_ref,
                   q16_ref, k16_ref, v16_ref, a_scr, qkv_scr, cls_scr):
    tm = x_ref.shape[0]
    n = tm // PROJ_SUBTILES
    nlt = B_WIDTH // LANES
    d4, d16 = DILATIONS[1], DILATIONS[2]
    first_head = lax.broadcasted_iota(I32, (CHUNK, LANES), 1) < HEAD_DIM
    first_half = (lax.broadcasted_iota(I32, (n, LANES), 1) % HEAD_DIM) < (HEAD_DIM // 2)
    avg = avg_ref[...]

    for sub in range(PROJ_SUBTILES):
        r0 = sub * n
        rows_n = slice(r0, r0 + n)
        xn = _rms(x_ref[rows_n, :], g1_ref[...]).astype(BF16)

        ug = _gelu_tanh(_dot(xn, w_ref[:, 0:A_WIDTH]))
        vg = _gelu_tanh(_dot(xn, w_ref[:, A_WIDTH:2 * A_WIDTH]))
        mu = _dot(vg.astype(BF16), avg)
        d = vg - mu
        var = _dot((d * d).astype(BF16), avg)
        vn = (d * lax.rsqrt(var + NORM_EPS) * lng_ref[...] + lnb_ref[...]).astype(BF16)
        for c in range(n // CHUNK):
            rows = slice(c * CHUNK, (c + 1) * CHUNK)
            for p in range(A_WIDTH // LANES):
                cols = slice(p * LANES, (p + 1) * LANES)
                slab = vn[rows, cols]
                g = jnp.where(first_head, _dot(ws_ref[2 * p], slab), _dot(ws_ref[2 * p + 1], slab))
                a_scr[r0 + c * CHUNK:r0 + (c + 1) * CHUNK, cols] = ug[rows, cols] * (g + bs_ref[:, cols])
        a_ref[rows_n, :] = _rms(a_scr[rows_n, :], aog_ref[...]).astype(BF16)

        cos = cos_ref[rows_n, :]
        sin = sin_ref[rows_n, :]

        def rope(t):
            rot = jnp.where(first_half, pltpu.roll(t, LANES - HEAD_DIM // 2, 1), pltpu.roll(t, HEAD_DIM // 2, 1))
            return t * cos + rot * sin

        off = 2 * A_WIDTH
        for p in range(nlt):
            q = _dot(xn, w_ref[:, off + p * LANES: off + (p + 1) * LANES])
            qkv_scr[p, rows_n, :] = rope(q) * (HEAD_DIM ** -0.5)
            k = _dot(xn, w_ref[:, off + B_WIDTH + p * LANES: off + B_WIDTH + (p + 1) * LANES])
            qkv_scr[nlt + p, rows_n, :] = rope(k)
            qkv_scr[2 * nlt + p, rows_n, :] = _dot(xn, w_ref[:, off + 2 * B_WIDTH + p * LANES:
                                                             off + 2 * B_WIDTH + (p + 1) * LANES])

        rows4 = slice(r0 // d4, (r0 + n) // d4)
        rows16 = slice(r0 // d16, (r0 + n) // d16)
        for i, (nat_ref, c4_ref, c16_ref) in enumerate(((q_ref, q4_ref, q16_ref), (k_ref, k4_ref, k16_ref),
                                                        (v_ref, v4_ref, v16_ref))):
            for p in range(nlt):
                cols = slice(p * LANES, (p + 1) * LANES)
                nat_ref[rows_n, cols] = qkv_scr[i * nlt + p, rows_n, :].astype(BF16)
                for r4 in range(d4):
                    cls = qkv_scr[i * nlt + p, pl.ds(r0 + r4, n // d4, stride=d4), :]
                    c4_ref[0, r4, rows4, cols] = cls.astype(BF16)
                    cls_scr[i * nlt + p, r4, rows4, :] = cls
                for r4 in range(d4):
                    for a in range(d16 // d4):
                        c16_ref[0, r4 + d4 * a, rows16, cols] = cls_scr[
                            i * nlt + p, r4, pl.ds(r0 // d4 + a, n // d16, stride=d16 // d4), :].astype(BF16)


def _inproj(x2, g1, w_in, avg, lng, lnb, ws, bs, aog, cos, sin, seq):
    t = x2.shape[0]
    tm = TM_PROJ
    nseq = seq // tm
    full = lambda shape: pl.BlockSpec(shape, lambda i: (0,) * len(shape))
    rows = lambda w: pl.BlockSpec((tm, w), lambda i: (i, 0))
    classes = lambda dil: pl.BlockSpec((1, dil, tm // dil, B_WIDTH), lambda i: (i // nseq, 0, i % nseq, 0))
    class_shape = lambda dil: jax.ShapeDtypeStruct((t // seq, dil, seq // dil, B_WIDTH), BF16)
    return pl.pallas_call(
        _inproj_kernel,
        grid=(t // tm,),
        in_specs=[rows(D_MODEL), full((1, D_MODEL)), full(w_in.shape), full(avg.shape),
                  full((1, A_WIDTH)), full((1, A_WIDTH)), full(ws.shape), full(bs.shape),
                  full((1, A_WIDTH)),
                  pl.BlockSpec((tm, LANES), lambda i: (i % nseq, 0)),
                  pl.BlockSpec((tm, LANES), lambda i: (i % nseq, 0))],
        out_specs=[rows(A_WIDTH)] + [rows(B_WIDTH)] * 3 + [classes(DILATIONS[1])] * 3 + [classes(DILATIONS[2])] * 3,
        out_shape=([jax.ShapeDtypeStruct((t, A_WIDTH), BF16)] + [jax.ShapeDtypeStruct((t, B_WIDTH), BF16)] * 3
                   + [class_shape(DILATIONS[1])] * 3 + [class_shape(DILATIONS[2])] * 3),
        scratch_shapes=[pltpu.VMEM((tm, A_WIDTH), F32), pltpu.VMEM((3 * B_WIDTH // LANES, tm, LANES), F32),
                        pltpu.VMEM((3 * B_WIDTH // LANES, DILATIONS[1], tm // DILATIONS[1], LANES), F32)],
        compiler_params=pltpu.CompilerParams(dimension_semantics=("arbitrary",), vmem_limit_bytes=VMEM_LIMIT),
        name="inproj",
    )(x2, g1, w_in, avg, lng, lnb, ws, bs, aog, cos, sin)


def _attn_block(qb, kw, vw, bias, first_head):
    zero = jnp.zeros_like(qb)
    q2 = jnp.concatenate([jnp.where(first_head, qb, zero), jnp.where(first_head, zero, qb)], axis=0)
    s = lax.dot_general(q2, kw, (((1,), (1,)), ((), ())), preferred_element_type=F32) + bias
    m = jnp.max(s, axis=-1, keepdims=True)
    p = jnp.exp(s - m)
    l = jnp.sum(p, axis=-1, keepdims=True)
    o = _dot(p.astype(BF16), vw) / l
    lse = jnp.broadcast_to(m + jnp.log(l), o.shape)
    return (jnp.where(first_head, o[:BAND], o[BAND:]),
            jnp.where(first_head, lse[:BAND], lse[BAND:]))


def _attn_kernel(q1, k1, v1, q4, k4, v4, q16, k16, v16, bias_band_ref, bias_first_ref, o_ref, o_scr, l_scr):
    seq = q1.shape[0]
    first_head = lax.broadcasted_iota(I32, (BAND, LANES), 1) < HEAD_DIM
    branches = ((q1, k1, v1), (q4, k4, v4), (q16, k16, v16))
    for bi, dil in enumerate(DILATIONS):
        q_r, k_r, v_r = branches[bi]
        length = seq // dil
        for r in range(dil):
            for n in range(length // BAND):
                lo = n * BAND
                if bi == 0:
                    ref_slice = lambda ref, a, b: ref[a:b, :]
                else:
                    ref_slice = lambda ref, a, b, r=r: ref[0, r, a:b, :]
                qb = ref_slice(q_r, lo, lo + BAND)
                if n == 0:
                    kw, vw, bias = ref_slice(k_r, 0, BAND), ref_slice(v_r, 0, BAND), bias_first_ref[...]
                else:
                    kw, vw = ref_slice(k_r, lo - BAND, lo + BAND), ref_slice(v_r, lo - BAND, lo + BAND)
                    bias = bias_band_ref[...]
                o, lse = _attn_block(qb, kw, vw, bias, first_head)
                if dil == 1:
                    o_scr[bi, lo:lo + BAND, :] = o
                    l_scr[bi, lo:lo + BAND, :] = lse
                else:
                    dst = pl.ds(r + dil * lo, BAND, stride=dil)
                    o_scr[bi, dst, :] = o
                    l_scr[bi, dst, :] = lse
    lses = [l_scr[i] for i in range(3)]
    m = jnp.maximum(jnp.maximum(lses[0], lses[1]), lses[2])
    es = [jnp.exp(l - m) for l in lses]
    den = es[0] + es[1] + es[2]
    o_ref[...] = (es[0] / den) * o_scr[0] + (es[1] / den) * o_scr[1] + (es[2] / den) * o_scr[2]


def _attention(qkv, batch, seq):
    rel = (np.arange(BAND)[:, None] + BAND) - np.arange(2 * BAND)[None, :]
    band = np.where((rel >= 0) & (rel <= BAND), 0.0, NEG_INF).astype(np.float32)
    bias_band = jnp.asarray(np.concatenate([band, band], axis=0))
    bias_first = jnp.asarray(np.concatenate([band[:, BAND:], band[:, BAND:]], axis=0))
    nat = pl.BlockSpec((seq, LANES), lambda b, p: (b, p))
    cls = lambda dil: pl.BlockSpec((1, dil, seq // dil, LANES), lambda b, p: (b, 0, 0, p))
    full = lambda a: pl.BlockSpec(a.shape, lambda b, p: (0, 0))
    return pl.pallas_call(
        _attn_kernel,
        grid=(batch, B_WIDTH // LANES),
        in_specs=[nat] * 3 + [cls(4)] * 3 + [cls(16)] * 3 + [full(bias_band), full(bias_first)],
        out_specs=pl.BlockSpec((seq, LANES), lambda b, p: (b, p)),
        out_shape=jax.ShapeDtypeStruct((batch * seq, B_WIDTH), F32),
        scratch_shapes=[pltpu.VMEM((3, seq, LANES), F32), pltpu.VMEM((3, seq, LANES), F32)],
        compiler_params=pltpu.CompilerParams(dimension_semantics=("arbitrary", "arbitrary"),
                                             vmem_limit_bytes=VMEM_LIMIT),
        name="dilated_attention",
    )(*qkv, bias_band, bias_first)


def _mix_kernel(a_ref, bm_ref, x_ref, bog_ref, wout_ref, n2g_ref, wr_ref, br_ref, tri_ref,
                h_ref, hn_ref, idx_ref, gate_ref, cnt_ref, cnt_scr):
    tm = x_ref.shape[0]
    bn = _rms(bm_ref[...], bog_ref[...]).astype(BF16)
    mixed = jnp.concatenate([a_ref[...], bn], axis=1)
    h = x_ref[...] + _dot(mixed, wout_ref[...])
    h_ref[...] = h
    hn = _rms(h, n2g_ref[...])
    for j in range(ROW_TILES):
        hn_ref[pl.ds(j, tm, stride=ROW_TILES), :] = hn[:, j * LANES:(j + 1) * LANES]
    logits = _dot(hn.astype(BF16), wr_ref[...]) + br_ref[...]
    lane = lax.broadcasted_iota(I32, (tm, LANES), 1)
    vals, idxs = [], []
    for _ in range(TOP_K):
        m = jnp.max(logits, axis=-1, keepdims=True)
        am = jnp.min(jnp.where(logits == m, lane, LANES), axis=-1, keepdims=True)
        vals.append(m)
        idxs.append(am)
        logits = jnp.where(lane == am, -jnp.inf, logits)
    es = [jnp.exp(v - vals[0]) for v in vals]
    den = es[0] + es[1] + es[2] + es[3]
    @pl.when(pl.program_id(0) == 0)
    def _():
        cnt_scr[...] = jnp.zeros_like(cnt_scr)

    chosen = jnp.zeros((tm, LANES), F32)
    for kk in range(TOP_K):
        chosen = chosen + (lane == idxs[kk]).astype(F32)
    before = _dot(tri_ref[...], chosen.astype(BF16)) + cnt_scr[...]
    cnt_scr[...] = cnt_scr[...] + jnp.sum(chosen, axis=0, keepdims=True)
    cnt_ref[...] = jnp.broadcast_to(cnt_scr[...], cnt_ref.shape)

    idx_out = jnp.zeros((tm, LANES), I32)
    gate_out = jnp.zeros((tm, LANES), F32)
    for kk in range(TOP_K):
        rank = jnp.sum(jnp.where(lane == idxs[kk], before, 0.0), axis=-1, keepdims=True).astype(I32)
        idx_out = jnp.where(lane == kk, idxs[kk], idx_out)
        idx_out = jnp.where(lane == TOP_K + kk, rank, idx_out)
        gate_out = jnp.where(lane == kk, es[kk] / den, gate_out)
    idx_ref[...] = idx_out
    gate_ref[...] = gate_out


def _mix(a_out, b_mix, x2, bog, w_out, n2g, w_r, b_r):
    t = x2.shape[0]
    tm = TM_PROJ
    full = lambda shape: pl.BlockSpec(shape, lambda i: (0,) * len(shape))
    rows = lambda w: pl.BlockSpec((tm, w), lambda i: (i, 0))
    tri = jnp.asarray(np.tril(np.ones((tm, tm), np.float32), -1), dtype=BF16)
    return pl.pallas_call(
        _mix_kernel,
        grid=(t // tm,),
        in_specs=[rows(A_WIDTH), rows(B_WIDTH), rows(D_MODEL), full((1, B_WIDTH)), full(w_out.shape),
                  full((1, D_MODEL)), full(w_r.shape), full((1, LANES)), full((tm, tm))],
        out_specs=[rows(D_MODEL), pl.BlockSpec((tm * ROW_TILES, LANES), lambda i: (i, 0)), rows(LANES), rows(LANES),
                   full((SUBLANES, LANES))],
        out_shape=[jax.ShapeDtypeStruct((t, D_MODEL), F32), jax.ShapeDtypeStruct((t * ROW_TILES, LANES), F32),
                   jax.ShapeDtypeStruct((t, LANES), I32), jax.ShapeDtypeStruct((t, LANES), F32),
                   jax.ShapeDtypeStruct((SUBLANES, LANES), F32)],
        scratch_shapes=[pltpu.VMEM((1, LANES), F32)],
        compiler_params=pltpu.CompilerParams(dimension_semantics=("arbitrary",), vmem_limit_bytes=VMEM_LIMIT),
        name="mix_router",
    )(a_out, b_mix, x2, bog, w_out, n2g, w_r, b_r, tri)


def _tile_rows(ref, row):
    return ref.at[pl.ds(pl.multiple_of(row * ROW_TILES, SUBLANES), ROW_TILES), :]


def _dispatch_kernel(pos_ref, pad_ref, hn_ref, xs_hbm, zero_scr, sem):
    n_rows = pos_ref.shape[-1]
    n_pad = pad_ref.shape[-1]

    @pl.when(pl.program_id(0) == 0)
    def _():
        zero_scr[...] = jnp.zeros_like(zero_scr)

    def body(g, c):
        for u in range(DMA_UNROLL):
            f = g * DMA_UNROLL + u
            token = g * (DMA_UNROLL // TOP_K) + u // TOP_K
            pltpu.make_async_copy(_tile_rows(hn_ref, token), _tile_rows(xs_hbm, pos_ref[0, 0, f]),
                                  sem).start(priority=u % 2)
        return c
    lax.fori_loop(0, n_rows // DMA_UNROLL, body, 0)

    def pad_body(g, c):
        for u in range(DMA_UNROLL):
            pltpu.make_async_copy(zero_scr, _tile_rows(xs_hbm, pad_ref[0, 0, g * DMA_UNROLL + u]),
                                  sem).start(priority=u % 2)
        return c
    lax.fori_loop(0, n_pad // DMA_UNROLL, pad_body, 0)

    total = (n_rows + n_pad) * ROW_TILES
    pltpu.make_async_copy(xs_hbm.at[pl.ds(0, total), :], xs_hbm.at[pl.ds(0, total), :], sem).wait()


def _dispatch(pos, pad_pos, hn_tiles, n_slots):
    t = hn_tiles.shape[0] // ROW_TILES
    tm = TM_PROJ
    steps = t // tm
    n_pad = pad_pos.shape[0] // steps
    return pl.pallas_call(
        _dispatch_kernel,
        grid=(steps,),
        in_specs=[pl.BlockSpec((1, 1, tm * TOP_K), lambda i: (i, 0, 0), memory_space=pltpu.SMEM),
                  pl.BlockSpec((1, 1, n_pad), lambda i: (i, 0, 0), memory_space=pltpu.SMEM),
                  pl.BlockSpec((tm * ROW_TILES, LANES), lambda i: (i, 0))],
        out_specs=pl.BlockSpec(memory_space=pl.ANY),
        out_shape=jax.ShapeDtypeStruct((n_slots * ROW_TILES, LANES), F32),
        scratch_shapes=[pltpu.VMEM((ROW_TILES, LANES), F32), pltpu.SemaphoreType.DMA],
        compiler_params=pltpu.CompilerParams(dimension_semantics=("arbitrary",), vmem_limit_bytes=VMEM_LIMIT),
        name="dispatch",
    )(pos.reshape(steps, 1, tm * TOP_K), pad_pos.reshape(steps, 1, n_pad), hn_tiles)


def _moe_kernel(te_ref, set_ref, nt_ref, x_ref, wgu_ref, bgu_ref, wd_ref, bd_ref, y_ref, wgu_bf, wd_bf, act_scr):
    tm = TM_MOE
    s = pl.program_id(0)
    i = jnp.maximum(s - 1, 0)
    last = te_ref.shape[0] - 1
    ahead = jnp.minimum(s, last)

    @pl.when((s == 0) | ((s < nt_ref[0]) & (te_ref[ahead] != te_ref[i])))
    def _():
        dst = set_ref[ahead]
        wgu_bf[dst] = wgu_ref[0].astype(BF16)
        wd_bf[dst] = wd_ref[0].astype(BF16)

    @pl.when((s >= 1) & (s - 1 < nt_ref[0]))
    def _():
        cur = set_ref[i]
        x = jnp.concatenate([x_ref[pl.ds(j, tm, stride=ROW_TILES), :] for j in range(ROW_TILES)],
                            axis=1).astype(BF16)
        width = D_FF // MOE_FF_CHUNKS
        for c in range(MOE_FF_CHUNKS):
            lo, hi = c * width, (c + 1) * width
            gate = _dot(x, wgu_bf[cur, :, lo:hi]) + bgu_ref[0, :, lo:hi]
            up = _dot(x, wgu_bf[cur, :, D_FF + lo:D_FF + hi]) + bgu_ref[0, :, D_FF + lo:D_FF + hi]
            gate = jnp.minimum(gate, SWIGLU_LIMIT)
            up = jnp.clip(up, -SWIGLU_LIMIT, SWIGLU_LIMIT)
            act_scr[:, lo:hi] = ((up + 1.0) * (gate * jax.nn.sigmoid(gate * SWIGLU_ALPHA))).astype(BF16)
        y = _dot(act_scr[...], wd_bf[cur]) + bd_ref[0]
        for j in range(ROW_TILES):
            y_ref[pl.ds(j, tm, stride=ROW_TILES), :] = y[:, j * LANES:(j + 1) * LANES]

    @pl.when((s >= 1) & (s - 1 >= nt_ref[0]))
    def _():
        y_ref[...] = jnp.zeros_like(y_ref)


def _experts(tile_expert, n_tiles, xs, wgu, bgu, wd, bd, nt_max):
    tm = TM_MOE
    changes = jnp.concatenate([jnp.zeros((1,), I32), (tile_expert[1:] != tile_expert[:-1]).astype(I32)])
    weight_set = jnp.cumsum(changes) % 2
    cur = lambda s: jnp.maximum(s - 1, 0)
    ahead = lambda s: jnp.minimum(s, nt_max - 1)
    tile_in = pl.BlockSpec((tm * ROW_TILES, LANES), lambda s, te, ws, nt: (jnp.minimum(cur(s), nt[0] - 1), 0))
    tile_out = pl.BlockSpec((tm * ROW_TILES, LANES), lambda s, te, ws, nt: (cur(s), 0))
    grid_spec = pltpu.PrefetchScalarGridSpec(
        num_scalar_prefetch=3,
        grid=(nt_max + 1,),
        in_specs=[tile_in,
                  pl.BlockSpec((1, D_MODEL, 2 * D_FF), lambda s, te, ws, nt: (te[ahead(s)], 0, 0)),
                  pl.BlockSpec((1, 1, 2 * D_FF), lambda s, te, ws, nt: (te[cur(s)], 0, 0)),
                  pl.BlockSpec((1, D_FF, D_MODEL), lambda s, te, ws, nt: (te[ahead(s)], 0, 0)),
                  pl.BlockSpec((1, 1, D_MODEL), lambda s, te, ws, nt: (te[cur(s)], 0, 0))],
        out_specs=tile_out,
        scratch_shapes=[pltpu.VMEM((2, D_MODEL, 2 * D_FF), BF16), pltpu.VMEM((2, D_FF, D_MODEL), BF16),
                        pltpu.VMEM((tm, D_FF), BF16)],
    )
    return pl.pallas_call(
        _moe_kernel,
        grid_spec=grid_spec,
        out_shape=jax.ShapeDtypeStruct((nt_max * tm * ROW_TILES, LANES), F32),
        compiler_params=pltpu.CompilerParams(dimension_semantics=("arbitrary",), vmem_limit_bytes=VMEM_LIMIT),
        name="experts",
    )(tile_expert, weight_set.astype(I32), n_tiles, xs, wgu, bgu, wd, bd)


def _combine_kernel(pos_ref, pos_next_ref, ys_hbm, gate_ref, h_ref, g_ref, o_ref, buf0, buf1, acc_scr, sem):
    tm = h_ref.shape[0]
    n_rows = tm * TOP_K
    i = pl.program_id(0)
    bufs = (buf0, buf1)

    def gather(idx_ref, buf, s):
        def body(g, c):
            for u in range(DMA_UNROLL):
                f = g * DMA_UNROLL + u
                token = g * (DMA_UNROLL // TOP_K) + u // TOP_K
                pltpu.make_async_copy(_tile_rows(ys_hbm, idx_ref[0, 0, f]),
                                      _tile_rows(buf, (u % TOP_K) * tm + token), sem.at[s]).start(priority=u % 2)
            return c
        lax.fori_loop(0, n_rows // DMA_UNROLL, body, 0)

    @pl.when(i == 0)
    def _():
        gather(pos_ref, buf0, 0)

    for p in (0, 1):
        @pl.when(i % 2 == p)
        def _(p=p):
            @pl.when(i + 1 < pl.num_programs(0))
            def _():
                gather(pos_next_ref, bufs[1 - p], 1 - p)

            buf = bufs[p]
            pltpu.make_async_copy(ys_hbm.at[pl.ds(0, n_rows * ROW_TILES), :], buf, sem.at[p]).wait()
            gates = gate_ref[...]
            gk = [jnp.broadcast_to(gates[:, kk:kk + 1], (tm, LANES)) for kk in range(TOP_K)]
            ss = jnp.zeros((tm, 1), F32)
            for j in range(ROW_TILES):
                cols = slice(j * LANES, (j + 1) * LANES)
                moe = gk[0] * buf[pl.ds(j, tm, stride=ROW_TILES), :]
                for kk in range(1, TOP_K):
                    moe = moe + gk[kk] * buf[pl.ds(kk * tm * ROW_TILES + j, tm, stride=ROW_TILES), :]
                acc = h_ref[:, cols] + moe
                acc_scr[:, cols] = acc
                ss = ss + jnp.sum(acc * acc, axis=-1, keepdims=True)
            inv = lax.rsqrt(ss / D_MODEL + NORM_EPS)
            o_ref[...] = acc_scr[...] * inv * g_ref[...]


def _combine(pos, ys, gates, h, g):
    t = h.shape[0]
    tm = TM_COMB
    steps = t // tm
    pos_blocks = pos.reshape(steps, 1, tm * TOP_K)
    smem = lambda f: pl.BlockSpec((1, 1, tm * TOP_K), f, memory_space=pltpu.SMEM)
    row_buf = pltpu.VMEM((TOP_K * tm * ROW_TILES, LANES), F32)
    return pl.pallas_call(
        _combine_kernel,
        grid=(steps,),
        in_specs=[smem(lambda i: (i, 0, 0)),
                  smem(lambda i: (jnp.minimum(i + 1, steps - 1), 0, 0)),
                  pl.BlockSpec(memory_space=pl.ANY),
                  pl.BlockSpec((tm, LANES), lambda i: (i, 0)),
                  pl.BlockSpec((tm, D_MODEL), lambda i: (i, 0)),
                  pl.BlockSpec((1, D_MODEL), lambda i: (0, 0))],
        out_specs=pl.BlockSpec((tm, D_MODEL), lambda i: (i, 0)),
        out_shape=jax.ShapeDtypeStruct((t, D_MODEL), F32),
        scratch_shapes=[row_buf, row_buf, pltpu.VMEM((tm, D_MODEL), F32), pltpu.SemaphoreType.DMA((2,))],
        compiler_params=pltpu.CompilerParams(dimension_semantics=("arbitrary",), vmem_limit_bytes=VMEM_LIMIT),
        name="combine",
    )(pos_blocks, pos_blocks, ys, gates, h, g)


def _routing_plan(idx_rank, counts_f, n_tokens):
    tm = TM_MOE
    nt_max = n_tokens * TOP_K // tm + N_EXPERTS + IDLE_TILES
    experts = jnp.arange(N_EXPERTS, dtype=I32)
    counts = counts_f[0, :N_EXPERTS].astype(I32)
    tiles_e = (counts + tm - 1) // tm
    tile_end = jnp.cumsum(tiles_e)
    first_slot = (tile_end - tiles_e) * tm
    n_tiles = tile_end[-1]
    tile_ids = jnp.arange(nt_max, dtype=I32)
    te = jnp.minimum(jnp.sum((tile_ids[:, None] >= tile_end[None, :]).astype(I32), axis=1), N_EXPERTS - 1)
    last_e = jnp.take(te, jnp.maximum(n_tiles - 1, 0))
    te = jnp.where(tile_ids < n_tiles, te, last_e).astype(I32)
    expert = idx_rank[:, :TOP_K]
    rank = idx_rank[:, TOP_K:2 * TOP_K]
    pos = jnp.sum(jnp.where(expert[:, :, None] == experts[None, None, :], first_slot[None, None, :], 0), axis=-1) + rank
    lane = jnp.arange(tm, dtype=I32)[None, :]
    pad_e = jnp.concatenate([tiles_e * tm - counts, jnp.zeros((IDLE_TILES,), I32)])[:, None]
    pad_first = jnp.concatenate([first_slot + counts, jnp.zeros((IDLE_TILES,), I32)])[:, None]
    is_pad = lane < pad_e
    idle_rank = (jnp.cumsum((~is_pad).reshape(-1).astype(I32)) - 1).reshape(is_pad.shape)
    pad_pos = jnp.where(is_pad, pad_first + lane, n_tiles * tm + idle_rank)
    return te, n_tiles.reshape(1).astype(I32), pos.reshape(-1).astype(I32), pad_pos.reshape(-1).astype(I32), nt_max


def kernel(x, norm1_g, w_in, a_ln_g, a_ln_b, a_w_s, a_b_s, a_out_g, b_out_g, w_out, norm2_g, w_router,
           b_router, w_gate_up, b_gate_up, w_down, b_down, normf_g):
    batch, seq, _ = x.shape
    t = batch * seq
    assert seq % (TM_PROJ) == 0 and t % TM_MOE == 0 and seq // DILATIONS[-1] == BAND
    h = x.reshape(t, D_MODEL)

    pos = jnp.arange(seq, dtype=F32)
    inv = ROPE_THETA ** (-jnp.arange(0, HEAD_DIM, 2, dtype=F32) / HEAD_DIM)
    ang = pos[:, None] * inv[None, :]
    cos = jnp.tile(jnp.cos(ang), (1, 2 * LANES // HEAD_DIM))
    sin = jnp.tile(jnp.concatenate([-jnp.sin(ang), jnp.sin(ang)], axis=1), (1, LANES // HEAD_DIM))
    head_of_lane = np.arange(A_WIDTH) // HEAD_DIM
    avg = jnp.asarray((head_of_lane[:, None] == head_of_lane[None, :]).astype(np.float32) / HEAD_DIM, dtype=BF16)
    row2 = lambda v: v.reshape(1, -1).astype(F32)

    for layer in range(norm1_g.shape[0]):
        causal = np.tril(np.ones((CHUNK, CHUNK), dtype=bool))
        ws = jnp.where(causal[None], a_w_s[layer], 0.0).astype(BF16)
        bs = jnp.repeat(a_b_s[layer].astype(F32).T, HEAD_DIM, axis=1)
        a_out, *qkv = _inproj(h, row2(norm1_g[layer]), w_in[layer].astype(BF16), avg,
                              row2(a_ln_g[layer]), row2(a_ln_b[layer]), ws, bs, row2(a_out_g[layer]),
                              cos, sin, seq)
        b_mix = _attention(qkv, batch, seq)
        w_r = jnp.pad(w_router[layer], ((0, 0), (0, LANES - N_EXPERTS))).astype(BF16)
        b_r = jnp.concatenate([b_router[layer].astype(F32), jnp.full((LANES - N_EXPERTS,), NEG_INF, F32)])
        h_mid, hn, idx_rank, gates, counts = _mix(a_out, b_mix, h, row2(b_out_g[layer]), w_out[layer].astype(BF16),
                                                  row2(norm2_g[layer]), w_r, b_r.reshape(1, LANES))
        te, n_tiles, pos, pad_pos, nt_max = _routing_plan(idx_rank, counts, t)
        xs = _dispatch(pos, pad_pos, hn, nt_max * TM_MOE)
        ys = _experts(te, n_tiles, xs,
                      w_gate_up[layer], b_gate_up[layer].reshape(N_EXPERTS, 1, 2 * D_FF),
                      w_down[layer], b_down[layer].reshape(N_EXPERTS, 1, D_MODEL), nt_max)
        last = layer == norm1_g.shape[0] - 1
        assert last, "the combine kernel fuses the final norm; depth > 1 is not supported"
        h = _combine(pos, ys, gates, h_mid, row2(normf_g))
    return h.reshape(batch, seq, D_MODEL)
```

```python
import math

import numpy as np
import jax
import jax.numpy as jnp
from jax import lax
from jax.experimental import pallas as pl
from jax.experimental.pallas import tpu as pltpu

F32 = jnp.float32
BF16 = jnp.bfloat16
I32 = jnp.int32

D_MODEL = 1024
HEAD_DIM = 64
A_WIDTH = 512
B_WIDTH = 512
CHUNK = 128
BAND = 128
DILATIONS = (1, 4, 16)
ROPE_THETA = 10000.0
N_EXPERTS = 32
TOP_K = 4
D_FF = 1024
SWIGLU_ALPHA = 1.702
SWIGLU_LIMIT = 7.0
NORM_EPS = 1e-5
NEG_INF = -1e30

LANES = 128
SUBLANES = 8
ROW_TILES = D_MODEL // LANES

TM_PROJ = 512
PROJ_SUBTILES = 1
TM_DISPATCH = 1024
TM_MOE = 512
TM_COMB = 512
DMA_UNROLL = 16
MOE_FF_CHUNKS = 2
IDLE_TILES = 2
VMEM_LIMIT = 56 * 1024 * 1024


def _dot(a, b):
    return jnp.dot(a, b, preferred_element_type=F32)


def _gelu_tanh(x):
    c = math.sqrt(2.0 / math.pi)
    cdf = 0.5 * (1.0 + jnp.tanh(c * (x + 0.044715 * (x * x * x))))
    return x * cdf


def _rms(x, g):
    return x * lax.rsqrt(jnp.mean(x * x, axis=-1, keepdims=True) + NORM_EPS) * g


def _inproj_kernel(x_ref, g1_ref, w_ref, avg_ref, lng_ref, lnb_ref, ws_ref, bs_ref, aog_ref,
                   cos_ref, sin_ref, a_ref, q_ref, k_ref, v_ref, q4_ref, k4_ref, v4_ref,
                   q16_ref, k16_ref, v16_ref, a_scr, qkv_scr, cls_scr):
    tm = x_ref.shape[0]
    n = tm // PROJ_SUBTILES
    nlt = B_WIDTH // LANES
    d4, d16 = DILATIONS[1], DILATIONS[2]
    first_head = lax.broadcasted_iota(I32, (CHUNK, LANES), 1) < HEAD_DIM
    first_half = (lax.broadcasted_iota(I32, (n, LANES), 1) % HEAD_DIM) < (HEAD_DIM // 2)
    avg = avg_ref[...]

    for sub in range(PROJ_SUBTILES):
        r0 = sub * n
        rows_n = slice(r0, r0 + n)
        xn = _rms(x_ref[rows_n, :], g1_ref[...]).astype(BF16)

        ug = _gelu_tanh(_dot(xn, w_ref[:, 0:A_WIDTH]))
        vg = _gelu_tanh(_dot(xn, w_ref[:, A_WIDTH:2 * A_WIDTH]))
        mu = _dot(vg.astype(BF16), avg)
        d = vg - mu
        var = _dot((d * d).astype(BF16), avg)
        vn = (d * lax.rsqrt(var + NORM_EPS) * lng_ref[...] + lnb_ref[...]).astype(BF16)
        for c in range(n // CHUNK):
            rows = slice(c * CHUNK, (c + 1) * CHUNK)
            for p in range(A_WIDTH // LANES):
                cols = slice(p * LANES, (p + 1) * LANES)
                slab = vn[rows, cols]
                g = jnp.where(first_head, _dot(ws_ref[2 * p], slab), _dot(ws_ref[2 * p + 1], slab))
                a_scr[r0 + c * CHUNK:r0 + (c + 1) * CHUNK, cols] = ug[rows, cols] * (g + bs_ref[:, cols])
        a_ref[rows_n, :] = _rms(a_scr[rows_n, :], aog_ref[...]).astype(BF16)

        cos = cos_ref[rows_n, :]
        sin = sin_ref[rows_n, :]

        def rope(t):
            rot = jnp.where(first_half, pltpu.roll(t, LANES - HEAD_DIM // 2, 1), pltpu.roll(t, HEAD_DIM // 2, 1))
            return t * cos + rot * sin

        off = 2 * A_WIDTH
        for p in range(nlt):
            q = _dot(xn, w_ref[:, off + p * LANES: off + (p + 1) * LANES])
            qkv_scr[p, rows_n, :] = rope(q) * (HEAD_DIM ** -0.5)
            k = _dot(xn, w_ref[:, off + B_WIDTH + p * LANES: off + B_WIDTH + (p + 1) * LANES])
            qkv_scr[nlt + p, rows_n, :] = rope(k)
            qkv_scr[2 * nlt + p, rows_n, :] = _dot(xn, w_ref[:, off + 2 * B_WIDTH + p * LANES:
                                                             off + 2 * B_WIDTH + (p + 1) * LANES])

        rows4 = slice(r0 // d4, (r0 + n) // d4)
        rows16 = slice(r0 // d16, (r0 + n) // d16)
        for i, (nat_ref, c4_ref, c16_ref) in enumerate(((q_ref, q4_ref, q16_ref), (k_ref, k4_ref, k16_ref),
                                                        (v_ref, v4_ref, v16_ref))):
            for p in range(nlt):
                cols = slice(p * LANES, (p + 1) * LANES)
                nat_ref[rows_n, cols] = qkv_scr[i * nlt + p, rows_n, :].astype(BF16)
                for r4 in range(d4):
                    cls = qkv_scr[i * nlt + p, pl.ds(r0 + r4, n // d4, stride=d4), :]
                    c4_ref[0, r4, rows4, cols] = cls.astype(BF16)
                    cls_scr[i * nlt + p, r4, rows4, :] = cls
                for r4 in range(d4):
                    for a in range(d16 // d4):
                        c16_ref[0, r4 + d4 * a, rows16, cols] = cls_scr[
                            i * nlt + p, r4, pl.ds(r0 // d4 + a, n // d16, stride=d16 // d4), :].astype(BF16)


def _inproj(x2, g1, w_in, avg, lng, lnb, ws, bs, aog, cos, sin, seq):
    t = x2.shape[0]
    tm = TM_PROJ
    nseq = seq // tm
    full = lambda shape: pl.BlockSpec(shape, lambda i: (0,) * len(shape))
    rows = lambda w: pl.BlockSpec((tm, w), lambda i: (i, 0))
    classes = lambda dil: pl.BlockSpec((1, dil, tm // dil, B_WIDTH), lambda i: (i // nseq, 0, i % nseq, 0))
    class_shape = lambda dil: jax.ShapeDtypeStruct((t // seq, dil, seq // dil, B_WIDTH), BF16)
    return pl.pallas_call(
        _inproj_kernel,
        grid=(t // tm,),
        in_specs=[rows(D_MODEL), full((1, D_MODEL)), full(w_in.shape), full(avg.shape),
                  full((1, A_WIDTH)), full((1, A_WIDTH)), full(ws.shape), full(bs.shape),
                  full((1, A_WIDTH)),
                  pl.BlockSpec((tm, LANES), lambda i: (i % nseq, 0)),
                  pl.BlockSpec((tm, LANES), lambda i: (i % nseq, 0))],
        out_specs=[rows(A_WIDTH)] + [rows(B_WIDTH)] * 3 + [classes(DILATIONS[1])] * 3 + [classes(DILATIONS[2])] * 3,
        out_shape=([jax.ShapeDtypeStruct((t, A_WIDTH), BF16)] + [jax.ShapeDtypeStruct((t, B_WIDTH), BF16)] * 3
                   + [class_shape(DILATIONS[1])] * 3 + [class_shape(DILATIONS[2])] * 3),
        scratch_shapes=[pltpu.VMEM((tm, A_WIDTH), F32), pltpu.VMEM((3 * B_WIDTH // LANES, tm, LANES), F32),
                        pltpu.VMEM((3 * B_WIDTH // LANES, DILATIONS[1], tm // DILATIONS[1], LANES), F32)],
        compiler_params=pltpu.CompilerParams(dimension_semantics=("arbitrary",), vmem_limit_bytes=VMEM_LIMIT),
        name="inproj",
    )(x2, g1, w_in, avg, lng, lnb, ws, bs, aog, cos, sin)


def _attn_block(qb, kw, vw, bias, first_head):
    zero = jnp.zeros_like(qb)
    q2 = jnp.concatenate([jnp.where(first_head, qb, zero), jnp.where(first_head, zero, qb)], axis=0)
    s = lax.dot_general(q2, kw, (((1,), (1,)), ((), ())), preferred_element_type=F32) + bias
    m = jnp.max(s, axis=-1, keepdims=True)
    p = jnp.exp(s - m)
    l = jnp.sum(p, axis=-1, keepdims=True)
    o = _dot(p.astype(BF16), vw) / l
    lse = jnp.broadcast_to(m + jnp.log(l), o.shape)
    return (jnp.where(first_head, o[:BAND], o[BAND:]),
            jnp.where(first_head, lse[:BAND], lse[BAND:]))


def _attn_kernel(q1, k1, v1, q4, k4, v4, q16, k16, v16, bias_band_ref, bias_first_ref, o_ref, o_scr, l_scr):
    seq = q1.shape[0]
    first_head = lax.broadcasted_iota(I32, (BAND, LANES), 1) < HEAD_DIM
    branches = ((q1, k1, v1), (q4, k4, v4), (q16, k16, v16))
    for bi, dil in enumerate(DILATIONS):
        q_r, k_r, v_r = branches[bi]
        length = seq // dil
        for r in range(dil):
            for n in range(length // BAND):
                lo = n * BAND
                if bi == 0:
                    ref_slice = lambda ref, a, b: ref[a:b, :]
                else:
                    ref_slice = lambda ref, a, b, r=r: ref[0, r, a:b, :]
                qb = ref_slice(q_r, lo, lo + BAND)
                if n == 0:
                    kw, vw, bias = ref_slice(k_r, 0, BAND), ref_slice(v_r, 0, BAND), bias_first_ref[...]
                else:
                    kw, vw = ref_slice(k_r, lo - BAND, lo + BAND), ref_slice(v_r, lo - BAND, lo + BAND)
                    bias = bias_band_ref[...]
                o, lse = _attn_block(qb, kw, vw, bias, first_head)
                if dil == 1:
                    o_scr[bi, lo:lo + BAND, :] = o
                    l_scr[bi, lo:lo + BAND, :] = lse
                else:
                    dst = pl.ds(r + dil * lo, BAND, stride=dil)
                    o_scr[bi, dst, :] = o
                    l_scr[bi, dst, :] = lse
    lses = [l_scr[i] for i in range(3)]
    m = jnp.maximum(jnp.maximum(lses[0], lses[1]), lses[2])
    es = [jnp.exp(l - m) for l in lses]
    den = es[0] + es[1] + es[2]
    o_ref[...] = (es[0] / den) * o_scr[0] + (es[1] / den) * o_scr[1] + (es[2] / den) * o_scr[2]


def _attention(qkv, batch, seq):
    rel = (np.arange(BAND)[:, None] + BAND) - np.arange(2 * BAND)[None, :]
    band = np.where((rel >= 0) & (rel <= BAND), 0.0, NEG_INF).astype(np.float32)
    bias_band = jnp.asarray(np.concatenate([band, band], axis=0))
    bias_first = jnp.asarray(np.concatenate([band[:, BAND:], band[:, BAND:]], axis=0))
    nat = pl.BlockSpec((seq, LANES), lambda b, p: (b, p))
    cls = lambda dil: pl.BlockSpec((1, dil, seq // dil, LANES), lambda b, p: (b, 0, 0, p))
    full = lambda a: pl.BlockSpec(a.shape, lambda b, p: (0, 0))
    return pl.pallas_call(
        _attn_kernel,
        grid=(batch, B_WIDTH // LANES),
        in_specs=[nat] * 3 + [cls(4)] * 3 + [cls(16)] * 3 + [full(bias_band), full(bias_first)],
        out_specs=pl.BlockSpec((seq, LANES), lambda b, p: (b, p)),
        out_shape=jax.ShapeDtypeStruct((batch * seq, B_WIDTH), F32),
        scratch_shapes=[pltpu.VMEM((3, seq, LANES), F32), pltpu.VMEM((3, seq, LANES), F32)],
        compiler_params=pltpu.CompilerParams(dimension_semantics=("arbitrary", "arbitrary"),
                                             vmem_limit_bytes=VMEM_LIMIT),
        name="dilated_attention",
    )(*qkv, bias_band, bias_first)


def _mix_kernel(a_ref, bm_ref, x_ref, bog_ref, wout_ref, n2g_ref, wr_ref, br_ref, tri_ref,
                h_ref, hn_ref, idx_ref, gate_ref, cnt_ref, cnt_scr):
    tm = x_ref.shape[0]
    bn = _rms(bm_ref[...], bog_ref[...]).astype(BF16)
    mixed = jnp.concatenate([a_ref[...], bn], axis=1)
    h = x_ref[...] + _dot(mixed, wout_ref[...])
    h_ref[...] = h
    hn = _rms(h, n2g_ref[...])
    for j in range(ROW_TILES):
        hn_ref[pl.ds(j, tm, stride=ROW_TILES), :] = hn[:, j * LANES:(j + 1) * LANES]
    logits = _dot(hn.astype(BF16), wr_ref[...]) + br_ref[...]
    lane = lax.broadcasted_iota(I32, (tm, LANES), 1)
    vals, idxs = [], []
    for _ in range(TOP_K):
        m = jnp.max(logits, axis=-1, keepdims=True)
        am = jnp.min(jnp.where(logits == m, lane, LANES), axis=-1, keepdims=True)
        vals.append(m)
        idxs.append(am)
        logits = jnp.where(lane == am, -jnp.inf, logits)
    es = [jnp.exp(v - vals[0]) for v in vals]
    den = es[0] + es[1] + es[2] + es[3]
    @pl.when(pl.program_id(0) == 0)
    def _():
        cnt_scr[...] = jnp.zeros_like(cnt_scr)

    chosen = jnp.zeros((tm, LANES), F32)
    for kk in range(TOP_K):
        chosen = chosen + (lane == idxs[kk]).astype(F32)
    before = _dot(tri_ref[...], chosen.astype(BF16)) + cnt_scr[...]
    cnt_scr[...] = cnt_scr[...] + jnp.sum(chosen, axis=0, keepdims=True)
    cnt_ref[...] = jnp.broadcast_to(cnt_scr[...], cnt_ref.shape)

    idx_out = jnp.zeros((tm, LANES), I32)
    gate_out = jnp.zeros((tm, LANES), F32)
    for kk in range(TOP_K):
        rank = jnp.sum(jnp.where(lane == idxs[kk], before, 0.0), axis=-1, keepdims=True).astype(I32)
        idx_out = jnp.where(lane == kk, idxs[kk], idx_out)
        idx_out = jnp.where(lane == TOP_K + kk, rank, idx_out)
        gate_out = jnp.where(lane == kk, es[kk] / den, gate_out)
    idx_ref[...] = idx_out
    gate_ref[...] = gate_out


def _mix(a_out, b_mix, x2, bog, w_out, n2g, w_r, b_r):
    t = x2.shape[0]
    tm = TM_PROJ
    full = lambda shape: pl.BlockSpec(shape, lambda i: (0,) * len(shape))
    rows = lambda w: pl.BlockSpec((tm, w), lambda i: (i, 0))
    tri = jnp.asarray(np.tril(np.ones((tm, tm), np.float32), -1), dtype=BF16)
    return pl.pallas_call(
        _mix_kernel,
        grid=(t // tm,),
        in_specs=[rows(A_WIDTH), rows(B_WIDTH), rows(D_MODEL), full((1, B_WIDTH)), full(w_out.shape),
                  full((1, D_MODEL)), full(w_r.shape), full((1, LANES)), full((tm, tm))],
        out_specs=[rows(D_MODEL), pl.BlockSpec((tm * ROW_TILES, LANES), lambda i: (i, 0)), rows(LANES), rows(LANES),
                   full((SUBLANES, LANES))],
        out_shape=[jax.ShapeDtypeStruct((t, D_MODEL), F32), jax.ShapeDtypeStruct((t * ROW_TILES, LANES), F32),
                   jax.ShapeDtypeStruct((t, LANES), I32), jax.ShapeDtypeStruct((t, LANES), F32),
                   jax.ShapeDtypeStruct((SUBLANES, LANES), F32)],
        scratch_shapes=[pltpu.VMEM((1, LANES), F32)],
        compiler_params=pltpu.CompilerParams(dimension_semantics=("arbitrary",), vmem_limit_bytes=VMEM_LIMIT),
        name="mix_router",
    )(a_out, b_mix, x2, bog, w_out, n2g, w_r, b_r, tri)


def _tile_rows(ref, row):
    return ref.at[pl.ds(pl.multiple_of(row * ROW_TILES, SUBLANES), ROW_TILES), :]


def _dispatch_kernel(pos_ref, pad_ref, hn_ref, xs_hbm, zero_scr, sem):
    n_rows = pos_ref.shape[-1]
    n_pad = pad_ref.shape[-1]

    @pl.when(pl.program_id(0) == 0)
    def _():
        zero_scr[...] = jnp.zeros_like(zero_scr)

    def body(g, c):
        for u in range(DMA_UNROLL):
            f = g * DMA_UNROLL + u
            token = g * (DMA_UNROLL // TOP_K) + u // TOP_K
            pltpu.make_async_copy(_tile_rows(hn_ref, token), _tile_rows(xs_hbm, pos_ref[0, 0, f]),
                                  sem).start(priority=u % 2)
        return c
    lax.fori_loop(0, n_rows // DMA_UNROLL, body, 0)

    def pad_body(g, c):
        for u in range(DMA_UNROLL):
            pltpu.make_async_copy(zero_scr, _tile_rows(xs_hbm, pad_ref[0, 0, g * DMA_UNROLL + u]),
                                  sem).start(priority=u % 2)
        return c
    lax.fori_loop(0, n_pad // DMA_UNROLL, pad_body, 0)

    total = (n_rows + n_pad) * ROW_TILES
    pltpu.make_async_copy(xs_hbm.at[pl.ds(0, total), :], xs_hbm.at[pl.ds(0, total), :], sem).wait()


def _dispatch(pos, pad_pos, hn_tiles, n_slots):
    t = hn_tiles.shape[0] // ROW_TILES
    tm = TM_DISPATCH
    steps = t // tm
    n_pad = pad_pos.shape[0] // steps
    assert t % tm == 0 and pad_pos.shape[0] % (steps * DMA_UNROLL) == 0
    return pl.pallas_call(
        _dispatch_kernel,
        grid=(steps,),
        in_specs=[pl.BlockSpec((1, 1, tm * TOP_K), lambda i: (i, 0, 0), memory_space=pltpu.SMEM),
                  pl.BlockSpec((1, 1, n_pad), lambda i: (i, 0, 0), memory_space=pltpu.SMEM),
                  pl.BlockSpec((tm * ROW_TILES, LANES), lambda i: (i, 0))],
        out_specs=pl.BlockSpec(memory_space=pl.ANY),
        out_shape=jax.ShapeDtypeStruct((n_slots * ROW_TILES, LANES), F32),
        scratch_shapes=[pltpu.VMEM((ROW_TILES, LANES), F32), pltpu.SemaphoreType.DMA],
        compiler_params=pltpu.CompilerParams(dimension_semantics=("arbitrary",), vmem_limit_bytes=VMEM_LIMIT),
        name="dispatch",
    )(pos.reshape(steps, 1, tm * TOP_K), pad_pos.reshape(steps, 1, n_pad), hn_tiles)


def _moe_kernel(te_ref, nt_ref, x_ref, wgu_ref, bgu_ref, wd_ref, bd_ref, y_ref, wgu_bf, wd_bf, act_scr):
    tm = TM_MOE
    i = pl.program_id(0)

    @pl.when(i < nt_ref[0])
    def _():
        @pl.when((i == 0) | (te_ref[i] != te_ref[jnp.maximum(i - 1, 0)]))
        def _():
            wgu_bf[...] = wgu_ref[0].astype(BF16)
            wd_bf[...] = wd_ref[0].astype(BF16)

        x = jnp.concatenate([x_ref[pl.ds(j, tm, stride=ROW_TILES), :] for j in range(ROW_TILES)],
                            axis=1).astype(BF16)
        width = D_FF // MOE_FF_CHUNKS
        for c in range(MOE_FF_CHUNKS):
            lo, hi = c * width, (c + 1) * width
            gate = _dot(x, wgu_bf[:, lo:hi]) + bgu_ref[0, :, lo:hi]
            up = _dot(x, wgu_bf[:, D_FF + lo:D_FF + hi]) + bgu_ref[0, :, D_FF + lo:D_FF + hi]
            gate = jnp.minimum(gate, SWIGLU_LIMIT)
            up = jnp.clip(up, -SWIGLU_LIMIT, SWIGLU_LIMIT)
            act_scr[:, lo:hi] = ((up + 1.0) * (gate * jax.nn.sigmoid(gate * SWIGLU_ALPHA))).astype(BF16)
        y = _dot(act_scr[...], wd_bf[...]) + bd_ref[0]
        for j in range(ROW_TILES):
            y_ref[pl.ds(j, tm, stride=ROW_TILES), :] = y[:, j * LANES:(j + 1) * LANES]

    @pl.when(i >= nt_ref[0])
    def _():
        y_ref[...] = jnp.zeros_like(y_ref)


def _experts(tile_expert, n_tiles, xs, wgu, bgu, wd, bd, nt_max):
    tm = TM_MOE
    tile_in = pl.BlockSpec((tm * ROW_TILES, LANES), lambda i, te, nt: (jnp.minimum(i, nt[0] - 1), 0))
    tile = pl.BlockSpec((tm * ROW_TILES, LANES), lambda i, te, nt: (i, 0))
    grid_spec = pltpu.PrefetchScalarGridSpec(
        num_scalar_prefetch=2,
        grid=(nt_max,),
        in_specs=[tile_in,
                  pl.BlockSpec((1, D_MODEL, 2 * D_FF), lambda i, te, nt: (te[i], 0, 0)),
                  pl.BlockSpec((1, 1, 2 * D_FF), lambda i, te, nt: (te[i], 0, 0)),
                  pl.BlockSpec((1, D_FF, D_MODEL), lambda i, te, nt: (te[i], 0, 0)),
                  pl.BlockSpec((1, 1, D_MODEL), lambda i, te, nt: (te[i], 0, 0))],
        out_specs=tile,
        scratch_shapes=[pltpu.VMEM((D_MODEL, 2 * D_FF), BF16), pltpu.VMEM((D_FF, D_MODEL), BF16),
                        pltpu.VMEM((tm, D_FF), BF16)],
    )
    return pl.pallas_call(
        _moe_kernel,
        grid_spec=grid_spec,
        out_shape=jax.ShapeDtypeStruct((nt_max * tm * ROW_TILES, LANES), F32),
        compiler_params=pltpu.CompilerParams(dimension_semantics=("arbitrary",), vmem_limit_bytes=VMEM_LIMIT),
        name="experts",
    )(tile_expert, n_tiles, xs, wgu, bgu, wd, bd)


def _combine_kernel(pos_ref, pos_next_ref, ys_hbm, gate_ref, h_ref, g_ref, o_ref, buf0, buf1, acc_scr, sem):
    tm = h_ref.shape[0]
    n_rows = tm * TOP_K
    i = pl.program_id(0)
    bufs = (buf0, buf1)

    def gather(idx_ref, buf, s):
        def body(g, c):
            for u in range(DMA_UNROLL):
                f = g * DMA_UNROLL + u
                token = g * (DMA_UNROLL // TOP_K) + u // TOP_K
                pltpu.make_async_copy(_tile_rows(ys_hbm, idx_ref[0, 0, f]),
                                      _tile_rows(buf, (u % TOP_K) * tm + token), sem.at[s]).start(priority=u % 2)
            return c
        lax.fori_loop(0, n_rows // DMA_UNROLL, body, 0)

    @pl.when(i == 0)
    def _():
        gather(pos_ref, buf0, 0)

    for p in (0, 1):
        @pl.when(i % 2 == p)
        def _(p=p):
            @pl.when(i + 1 < pl.num_programs(0))
            def _():
                gather(pos_next_ref, bufs[1 - p], 1 - p)

            buf = bufs[p]
            pltpu.make_async_copy(ys_hbm.at[pl.ds(0, n_rows * ROW_TILES), :], buf, sem.at[p]).wait()
            gates = gate_ref[...]
            gk = [jnp.broadcast_to(gates[:, kk:kk + 1], (tm, LANES)) for kk in range(TOP_K)]
            ss = jnp.zeros((tm, 1), F32)
            for j in range(ROW_TILES):
                cols = slice(j * LANES, (j + 1) * LANES)
                moe = gk[0] * buf[pl.ds(j, tm, stride=ROW_TILES), :]
                for kk in range(1, TOP_K):
                    moe = moe + gk[kk] * buf[pl.ds(kk * tm * ROW_TILES + j, tm, stride=ROW_TILES), :]
                acc = h_ref[:, cols] + moe
                acc_scr[:, cols] = acc
                ss = ss + jnp.sum(acc * acc, axis=-1, keepdims=True)
            inv = lax.rsqrt(ss / D_MODEL + NORM_EPS)
            o_ref[...] = acc_scr[...] * inv * g_ref[...]


def _combine(pos, ys, gates, h, g):
    t = h.shape[0]
    tm = TM_COMB
    steps = t // tm
    pos_blocks = pos.reshape(steps, 1, tm * TOP_K)
    smem = lambda f: pl.BlockSpec((1, 1, tm * TOP_K), f, memory_space=pltpu.SMEM)
    row_buf = pltpu.VMEM((TOP_K * tm * ROW_TILES, LANES), F32)
    return pl.pallas_call(
        _combine_kernel,
        grid=(steps,),
        in_specs=[smem(lambda i: (i, 0, 0)),
                  smem(lambda i: (jnp.minimum(i + 1, steps - 1), 0, 0)),
                  pl.BlockSpec(memory_space=pl.ANY),
                  pl.BlockSpec((tm, LANES), lambda i: (i, 0)),
                  pl.BlockSpec((tm, D_MODEL), lambda i: (i, 0)),
                  pl.BlockSpec((1, D_MODEL), lambda i: (0, 0))],
        out_specs=pl.BlockSpec((tm, D_MODEL), lambda i: (i, 0)),
        out_shape=jax.ShapeDtypeStruct((t, D_MODEL), F32),
        scratch_shapes=[row_buf, row_buf, pltpu.VMEM((tm, D_MODEL), F32), pltpu.SemaphoreType.DMA((2,))],
        compiler_params=pltpu.CompilerParams(dimension_semantics=("arbitrary",), vmem_limit_bytes=VMEM_LIMIT),
        name="combine",
    )(pos_blocks, pos_blocks, ys, gates, h, g)


def _routing_plan(idx_rank, counts_f, n_tokens):
    tm = TM_MOE
    nt_max = n_tokens * TOP_K // tm + N_EXPERTS + IDLE_TILES
    experts = jnp.arange(N_EXPERTS, dtype=I32)
    counts = counts_f[0, :N_EXPERTS].astype(I32)
    tiles_e = (counts + tm - 1) // tm
    tile_end = jnp.cumsum(tiles_e)
    first_slot = (tile_end - tiles_e) * tm
    n_tiles = tile_end[-1]
    tile_ids = jnp.arange(nt_max, dtype=I32)
    te = jnp.minimum(jnp.sum((tile_ids[:, None] >= tile_end[None, :]).astype(I32), axis=1), N_EXPERTS - 1)
    last_e = jnp.take(te, jnp.maximum(n_tiles - 1, 0))
    te = jnp.where(tile_ids < n_tiles, te, last_e).astype(I32)
    expert = idx_rank[:, :TOP_K]
    rank = idx_rank[:, TOP_K:2 * TOP_K]
    pos = jnp.sum(jnp.where(expert[:, :, None] == experts[None, None, :], first_slot[None, None, :], 0), axis=-1) + rank
    lane = jnp.arange(tm, dtype=I32)[None, :]
    pad_e = jnp.concatenate([tiles_e * tm - counts, jnp.zeros((IDLE_TILES,), I32)])[:, None]
    pad_first = jnp.concatenate([first_slot + counts, jnp.zeros((IDLE_TILES,), I32)])[:, None]
    is_pad = lane < pad_e
    idle_rank = (jnp.cumsum((~is_pad).reshape(-1).astype(I32)) - 1).reshape(is_pad.shape)
    pad_pos = jnp.where(is_pad, pad_first + lane, n_tiles * tm + idle_rank)
    return te, n_tiles.reshape(1).astype(I32), pos.reshape(-1).astype(I32), pad_pos.reshape(-1).astype(I32), nt_max


def kernel(x, norm1_g, w_in, a_ln_g, a_ln_b, a_w_s, a_b_s, a_out_g, b_out_g, w_out, norm2_g, w_router,
           b_router, w_gate_up, b_gate_up, w_down, b_down, normf_g):
    batch, seq, _ = x.shape
    t = batch * seq
    assert seq % (TM_PROJ) == 0 and t % TM_MOE == 0 and seq // DILATIONS[-1] == BAND
    h = x.reshape(t, D_MODEL)

    pos = jnp.arange(seq, dtype=F32)
    inv = ROPE_THETA ** (-jnp.arange(0, HEAD_DIM, 2, dtype=F32) / HEAD_DIM)
    ang = pos[:, None] * inv[None, :]
    cos = jnp.tile(jnp.cos(ang), (1, 2 * LANES // HEAD_DIM))
    sin = jnp.tile(jnp.concatenate([-jnp.sin(ang), jnp.sin(ang)], axis=1), (1, LANES // HEAD_DIM))
    head_of_lane = np.arange(A_WIDTH) // HEAD_DIM
    avg = jnp.asarray((head_of_lane[:, None] == head_of_lane[None, :]).astype(np.float32) / HEAD_DIM, dtype=BF16)
    row2 = lambda v: v.reshape(1, -1).astype(F32)

    for layer in range(norm1_g.shape[0]):
        causal = np.tril(np.ones((CHUNK, CHUNK), dtype=bool))
        ws = jnp.where(causal[None], a_w_s[layer], 0.0).astype(BF16)
        bs = jnp.repeat(a_b_s[layer].astype(F32).T, HEAD_DIM, axis=1)
        a_out, *qkv = _inproj(h, row2(norm1_g[layer]), w_in[layer].astype(BF16), avg,
                              row2(a_ln_g[layer]), row2(a_ln_b[layer]), ws, bs, row2(a_out_g[layer]),
                              cos, sin, seq)
        b_mix = _attention(qkv, batch, seq)
        w_r = jnp.pad(w_router[layer], ((0, 0), (0, LANES - N_EXPERTS))).astype(BF16)
        b_r = jnp.concatenate([b_router[layer].astype(F32), jnp.full((LANES - N_EXPERTS,), NEG_INF, F32)])
        h_mid, hn, idx_rank, gates, counts = _mix(a_out, b_mix, h, row2(b_out_g[layer]), w_out[layer].astype(BF16),
                                                  row2(norm2_g[layer]), w_r, b_r.reshape(1, LANES))
        te, n_tiles, pos, pad_pos, nt_max = _routing_plan(idx_rank, counts, t)
        xs = _dispatch(pos, pad_pos, hn, nt_max * TM_MOE)
        ys = _experts(te, n_tiles, xs,
                      w_gate_up[layer], b_gate_up[layer].reshape(N_EXPERTS, 1, 2 * D_FF),
                      w_down[layer], b_down[layer].reshape(N_EXPERTS, 1, D_MODEL), nt_max)
        last = layer == norm1_g.shape[0] - 1
        assert last, "the combine kernel fuses the final norm; depth > 1 is not supported"
        h = _combine(pos, ys, gates, h_mid, row2(normf_g))
    return h.reshape(batch, seq, D_MODEL)
```

```python
import math

import numpy as np
import jax
import jax.numpy as jnp
from jax import lax
from jax.experimental import pallas as pl
from jax.experimental.pallas import tpu as pltpu

F32 = jnp.float32
BF16 = jnp.bfloat16
I32 = jnp.int32

D_MODEL = 1024
HEAD_DIM = 64
A_WIDTH = 512
B_WIDTH = 512
CHUNK = 128
BAND = 128
DILATIONS = (1, 4, 16)
ROPE_THETA = 10000.0
N_EXPERTS = 32
TOP_K = 4
D_FF = 1024
SWIGLU_ALPHA = 1.702
SWIGLU_LIMIT = 7.0
NORM_EPS = 1e-5
NEG_INF = -1e30

LANES = 128
SUBLANES = 8
ROW_TILES = D_MODEL // LANES

TM_PROJ = 512
PROJ_SUBTILES = 1
TM_DISPATCH = 1024
TM_MOE = 512
TM_COMB = 512
DMA_UNROLL = 16
MOE_FF_CHUNKS = 2
VMEM_LIMIT = 56 * 1024 * 1024


def _dot(a, b):
    return jnp.dot(a, b, preferred_element_type=F32)


def _gelu_tanh(x):
    c = math.sqrt(2.0 / math.pi)
    cdf = 0.5 * (1.0 + jnp.tanh(c * (x + 0.044715 * (x * x * x))))
    return x * cdf


def _rms(x, g):
    return x * lax.rsqrt(jnp.mean(x * x, axis=-1, keepdims=True) + NORM_EPS) * g


def _inproj_kernel(x_ref, g1_ref, w_ref, avg_ref, lng_ref, lnb_ref, ws_ref, bs_ref, aog_ref,
                   cos_ref, sin_ref, a_ref, q_ref, k_ref, v_ref, q4_ref, k4_ref, v4_ref,
                   q16_ref, k16_ref, v16_ref, a_scr, qkv_scr, cls_scr):
    tm = x_ref.shape[0]
    n = tm // PROJ_SUBTILES
    nlt = B_WIDTH // LANES
    d4, d16 = DILATIONS[1], DILATIONS[2]
    first_head = lax.broadcasted_iota(I32, (CHUNK, LANES), 1) < HEAD_DIM
    first_half = (lax.broadcasted_iota(I32, (n, LANES), 1) % HEAD_DIM) < (HEAD_DIM // 2)
    avg = avg_ref[...]

    for sub in range(PROJ_SUBTILES):
        r0 = sub * n
        rows_n = slice(r0, r0 + n)
        xn = _rms(x_ref[rows_n, :], g1_ref[...]).astype(BF16)

        ug = _gelu_tanh(_dot(xn, w_ref[:, 0:A_WIDTH]))
        vg = _gelu_tanh(_dot(xn, w_ref[:, A_WIDTH:2 * A_WIDTH]))
        mu = _dot(vg.astype(BF16), avg)
        d = vg - mu
        var = _dot((d * d).astype(BF16), avg)
        vn = (d * lax.rsqrt(var + NORM_EPS) * lng_ref[...] + lnb_ref[...]).astype(BF16)
        for c in range(n // CHUNK):
            rows = slice(c * CHUNK, (c + 1) * CHUNK)
            for p in range(A_WIDTH // LANES):
                cols = slice(p * LANES, (p + 1) * LANES)
                slab = vn[rows, cols]
                g = jnp.where(first_head, _dot(ws_ref[2 * p], slab), _dot(ws_ref[2 * p + 1], slab))
                a_scr[r0 + c * CHUNK:r0 + (c + 1) * CHUNK, cols] = ug[rows, cols] * (g + bs_ref[:, cols])
        a_ref[rows_n, :] = _rms(a_scr[rows_n, :], aog_ref[...]).astype(BF16)

        cos = cos_ref[rows_n, :]
        sin = sin_ref[rows_n, :]

        def rope(t):
            rot = jnp.where(first_half, pltpu.roll(t, LANES - HEAD_DIM // 2, 1), pltpu.roll(t, HEAD_DIM // 2, 1))
            return t * cos + rot * sin

        off = 2 * A_WIDTH
        for p in range(nlt):
            q = _dot(xn, w_ref[:, off + p * LANES: off + (p + 1) * LANES])
            qkv_scr[p, rows_n, :] = rope(q) * (HEAD_DIM ** -0.5)
            k = _dot(xn, w_ref[:, off + B_WIDTH + p * LANES: off + B_WIDTH + (p + 1) * LANES])
            qkv_scr[nlt + p, rows_n, :] = rope(k)
            qkv_scr[2 * nlt + p, rows_n, :] = _dot(xn, w_ref[:, off + 2 * B_WIDTH + p * LANES:
                                                             off + 2 * B_WIDTH + (p + 1) * LANES])

        rows4 = slice(r0 // d4, (r0 + n) // d4)
        rows16 = slice(r0 // d16, (r0 + n) // d16)
        for i, (nat_ref, c4_ref, c16_ref) in enumerate(((q_ref, q4_ref, q16_ref), (k_ref, k4_ref, k16_ref),
                                                        (v_ref, v4_ref, v16_ref))):
            for p in range(nlt):
                cols = slice(p * LANES, (p + 1) * LANES)
                nat_ref[rows_n, cols] = qkv_scr[i * nlt + p, rows_n, :].astype(BF16)
                for r4 in range(d4):
                    cls = qkv_scr[i * nlt + p, pl.ds(r0 + r4, n // d4, stride=d4), :]
                    c4_ref[0, r4, rows4, cols] = cls.astype(BF16)
                    cls_scr[i * nlt + p, r4, rows4, :] = cls
                for r4 in range(d4):
                    for a in range(d16 // d4):
                        c16_ref[0, r4 + d4 * a, rows16, cols] = cls_scr[
                            i * nlt + p, r4, pl.ds(r0 // d4 + a, n // d16, stride=d16 // d4), :].astype(BF16)


def _inproj(x2, g1, w_in, avg, lng, lnb, ws, bs, aog, cos, sin, seq):
    t = x2.shape[0]
    tm = TM_PROJ
    nseq = seq // tm
    full = lambda shape: pl.BlockSpec(shape, lambda i: (0,) * len(shape))
    rows = lambda w: pl.BlockSpec((tm, w), lambda i: (i, 0))
    classes = lambda dil: pl.BlockSpec((1, dil, tm // dil, B_WIDTH), lambda i: (i // nseq, 0, i % nseq, 0))
    class_shape = lambda dil: jax.ShapeDtypeStruct((t // seq, dil, seq // dil, B_WIDTH), BF16)
    return pl.pallas_call(
        _inproj_kernel,
        grid=(t // tm,),
        in_specs=[rows(D_MODEL), full((1, D_MODEL)), full(w_in.shape), full(avg.shape),
                  full((1, A_WIDTH)), full((1, A_WIDTH)), full(ws.shape), full(bs.shape),
                  full((1, A_WIDTH)),
                  pl.BlockSpec((tm, LANES), lambda i: (i % nseq, 0)),
                  pl.BlockSpec((tm, LANES), lambda i: (i % nseq, 0))],
        out_specs=[rows(A_WIDTH)] + [rows(B_WIDTH)] * 3 + [classes(DILATIONS[1])] * 3 + [classes(DILATIONS[2])] * 3,
        out_shape=([jax.ShapeDtypeStruct((t, A_WIDTH), BF16)] + [jax.ShapeDtypeStruct((t, B_WIDTH), BF16)] * 3
                   + [class_shape(DILATIONS[1])] * 3 + [class_shape(DILATIONS[2])] * 3),
        scratch_shapes=[pltpu.VMEM((tm, A_WIDTH), F32), pltpu.VMEM((3 * B_WIDTH // LANES, tm, LANES), F32),
                        pltpu.VMEM((3 * B_WIDTH // LANES, DILATIONS[1], tm // DILATIONS[1], LANES), F32)],
        compiler_params=pltpu.CompilerParams(dimension_semantics=("arbitrary",), vmem_limit_bytes=VMEM_LIMIT),
        name="inproj",
    )(x2, g1, w_in, avg, lng, lnb, ws, bs, aog, cos, sin)


def _attn_block(qb, kw, vw, bias, first_head):
    zero = jnp.zeros_like(qb)
    q2 = jnp.concatenate([jnp.where(first_head, qb, zero), jnp.where(first_head, zero, qb)], axis=0)
    s = lax.dot_general(q2, kw, (((1,), (1,)), ((), ())), preferred_element_type=F32) + bias
    m = jnp.max(s, axis=-1, keepdims=True)
    p = jnp.exp(s - m)
    l = jnp.sum(p, axis=-1, keepdims=True)
    o = _dot(p.astype(BF16), vw) / l
    lse = jnp.broadcast_to(m + jnp.log(l), o.shape)
    return (jnp.where(first_head, o[:BAND], o[BAND:]),
            jnp.where(first_head, lse[:BAND], lse[BAND:]))


def _attn_kernel(q1, k1, v1, q4, k4, v4, q16, k16, v16, bias_band_ref, bias_first_ref, o_ref, o_scr, l_scr):
    seq = q1.shape[0]
    first_head = lax.broadcasted_iota(I32, (BAND, LANES), 1) < HEAD_DIM
    branches = ((q1, k1, v1), (q4, k4, v4), (q16, k16, v16))
    for bi, dil in enumerate(DILATIONS):
        q_r, k_r, v_r = branches[bi]
        length = seq // dil
        for r in range(dil):
            for n in range(length // BAND):
                lo = n * BAND
                if bi == 0:
                    ref_slice = lambda ref, a, b: ref[a:b, :]
                else:
                    ref_slice = lambda ref, a, b, r=r: ref[0, r, a:b, :]
                qb = ref_slice(q_r, lo, lo + BAND)
                if n == 0:
                    kw, vw, bias = ref_slice(k_r, 0, BAND), ref_slice(v_r, 0, BAND), bias_first_ref[...]
                else:
                    kw, vw = ref_slice(k_r, lo - BAND, lo + BAND), ref_slice(v_r, lo - BAND, lo + BAND)
                    bias = bias_band_ref[...]
                o, lse = _attn_block(qb, kw, vw, bias, first_head)
                if dil == 1:
                    o_scr[bi, lo:lo + BAND, :] = o
                    l_scr[bi, lo:lo + BAND, :] = lse
                else:
                    dst = pl.ds(r + dil * lo, BAND, stride=dil)
                    o_scr[bi, dst, :] = o
                    l_scr[bi, dst, :] = lse
    lses = [l_scr[i] for i in range(3)]
    m = jnp.maximum(jnp.maximum(lses[0], lses[1]), lses[2])
    es = [jnp.exp(l - m) for l in lses]
    den = es[0] + es[1] + es[2]
    o_ref[...] = (es[0] / den) * o_scr[0] + (es[1] / den) * o_scr[1] + (es[2] / den) * o_scr[2]


def _attention(qkv, batch, seq):
    rel = (np.arange(BAND)[:, None] + BAND) - np.arange(2 * BAND)[None, :]
    band = np.where((rel >= 0) & (rel <= BAND), 0.0, NEG_INF).astype(np.float32)
    bias_band = jnp.asarray(np.concatenate([band, band], axis=0))
    bias_first = jnp.asarray(np.concatenate([band[:, BAND:], band[:, BAND:]], axis=0))
    nat = pl.BlockSpec((seq, LANES), lambda b, p: (b, p))
    cls = lambda dil: pl.BlockSpec((1, dil, seq // dil, LANES), lambda b, p: (b, 0, 0, p))
    full = lambda a: pl.BlockSpec(a.shape, lambda b, p: (0, 0))
    return pl.pallas_call(
        _attn_kernel,
        grid=(batch, B_WIDTH // LANES),
        in_specs=[nat] * 3 + [cls(4)] * 3 + [cls(16)] * 3 + [full(bias_band), full(bias_first)],
        out_specs=pl.BlockSpec((seq, LANES), lambda b, p: (b, p)),
        out_shape=jax.ShapeDtypeStruct((batch * seq, B_WIDTH), F32),
        scratch_shapes=[pltpu.VMEM((3, seq, LANES), F32), pltpu.VMEM((3, seq, LANES), F32)],
        compiler_params=pltpu.CompilerParams(dimension_semantics=("arbitrary", "arbitrary"),
                                             vmem_limit_bytes=VMEM_LIMIT),
        name="dilated_attention",
    )(*qkv, bias_band, bias_first)


def _mix_kernel(a_ref, bm_ref, x_ref, bog_ref, wout_ref, n2g_ref, wr_ref, br_ref, tri_ref,
                h_ref, hn_ref, idx_ref, gate_ref, cnt_ref, cnt_scr):
    tm = x_ref.shape[0]
    bn = _rms(bm_ref[...], bog_ref[...]).astype(BF16)
    mixed = jnp.concatenate([a_ref[...], bn], axis=1)
    h = x_ref[...] + _dot(mixed, wout_ref[...])
    h_ref[...] = h
    hn = _rms(h, n2g_ref[...])
    for j in range(ROW_TILES):
        hn_ref[pl.ds(j, tm, stride=ROW_TILES), :] = hn[:, j * LANES:(j + 1) * LANES]
    logits = _dot(hn.astype(BF16), wr_ref[...]) + br_ref[...]
    lane = lax.broadcasted_iota(I32, (tm, LANES), 1)
    vals, idxs = [], []
    for _ in range(TOP_K):
        m = jnp.max(logits, axis=-1, keepdims=True)
        am = jnp.min(jnp.where(logits == m, lane, LANES), axis=-1, keepdims=True)
        vals.append(m)
        idxs.append(am)
        logits = jnp.where(lane == am, -jnp.inf, logits)
    es = [jnp.exp(v - vals[0]) for v in vals]
    den = es[0] + es[1] + es[2] + es[3]
    @pl.when(pl.program_id(0) == 0)
    def _():
        cnt_scr[...] = jnp.zeros_like(cnt_scr)

    chosen = jnp.zeros((tm, LANES), F32)
    for kk in range(TOP_K):
        chosen = chosen + (lane == idxs[kk]).astype(F32)
    before = _dot(tri_ref[...], chosen.astype(BF16)) + cnt_scr[...]
    cnt_scr[...] = cnt_scr[...] + jnp.sum(chosen, axis=0, keepdims=True)
    cnt_ref[...] = jnp.broadcast_to(cnt_scr[...], cnt_ref.shape)

    idx_out = jnp.zeros((tm, LANES), I32)
    gate_out = jnp.zeros((tm, LANES), F32)
    for kk in range(TOP_K):
        rank = jnp.sum(jnp.where(lane == idxs[kk], before, 0.0), axis=-1, keepdims=True).astype(I32)
        idx_out = jnp.where(lane == kk, idxs[kk], idx_out)
        idx_out = jnp.where(lane == TOP_K + kk, rank, idx_out)
        gate_out = jnp.where(lane == kk, es[kk] / den, gate_out)
    idx_ref[...] = idx_out
    gate_ref[...] = gate_out


def _mix(a_out, b_mix, x2, bog, w_out, n2g, w_r, b_r):
    t = x2.shape[0]
    tm = TM_PROJ
    full = lambda shape: pl.BlockSpec(shape, lambda i: (0,) * len(shape))
    rows = lambda w: pl.BlockSpec((tm, w), lambda i: (i, 0))
    tri = jnp.asarray(np.tril(np.ones((tm, tm), np.float32), -1), dtype=BF16)
    return pl.pallas_call(
        _mix_kernel,
        grid=(t // tm,),
        in_specs=[rows(A_WIDTH), rows(B_WIDTH), rows(D_MODEL), full((1, B_WIDTH)), full(w_out.shape),
                  full((1, D_MODEL)), full(w_r.shape), full((1, LANES)), full((tm, tm))],
        out_specs=[rows(D_MODEL), pl.BlockSpec((tm * ROW_TILES, LANES), lambda i: (i, 0)), rows(LANES), rows(LANES),
                   full((SUBLANES, LANES))],
        out_shape=[jax.ShapeDtypeStruct((t, D_MODEL), F32), jax.ShapeDtypeStruct((t * ROW_TILES, LANES), F32),
                   jax.ShapeDtypeStruct((t, LANES), I32), jax.ShapeDtypeStruct((t, LANES), F32),
                   jax.ShapeDtypeStruct((SUBLANES, LANES), F32)],
        scratch_shapes=[pltpu.VMEM((1, LANES), F32)],
        compiler_params=pltpu.CompilerParams(dimension_semantics=("arbitrary",), vmem_limit_bytes=VMEM_LIMIT),
        name="mix_router",
    )(a_out, b_mix, x2, bog, w_out, n2g, w_r, b_r, tri)


def _tile_rows(ref, row):
    return ref.at[pl.ds(pl.multiple_of(row * ROW_TILES, SUBLANES), ROW_TILES), :]


def _zero_runs(pad_start_ref, pad_count_ref, nt_ref, xs_hbm, zero_scr, sem, wait):
    tile_rows = TM_MOE * ROW_TILES

    def copy(rows, first_slot):
        c = pltpu.make_async_copy(zero_scr.at[pl.ds(0, rows * ROW_TILES), :],
                                  xs_hbm.at[pl.ds(pl.multiple_of(first_slot * ROW_TILES, SUBLANES), rows * ROW_TILES), :],
                                  sem)
        c.wait() if wait else c.start()

    def expert_body(e, c):
        count = pad_count_ref[e]
        slot = pad_start_ref[e]
        bit = TM_MOE // 2
        while bit:
            @pl.when((count & bit) != 0)
            def _(bit=bit, slot=slot):
                copy(bit, slot)
            slot = slot + (count & bit)
            bit //= 2
        return c
    lax.fori_loop(0, N_EXPERTS, expert_body, 0)

    def idle_body(tile, c):
        copy(TM_MOE, tile * TM_MOE)
        return c
    lax.fori_loop(nt_ref[0], xs_hbm.shape[0] // tile_rows, idle_body, 0)


def _dispatch_kernel(pad_start_ref, pad_count_ref, nt_ref, pos_ref, hn_ref, xs_hbm, zero_scr, sem, zero_sem):
    n_rows = pos_ref.shape[-1]
    i = pl.program_id(0)

    @pl.when(i == 0)
    def _():
        zero_scr[...] = jnp.zeros_like(zero_scr)
        _zero_runs(pad_start_ref, pad_count_ref, nt_ref, xs_hbm, zero_scr, zero_sem, wait=False)

    def body(g, c):
        for u in range(DMA_UNROLL):
            f = g * DMA_UNROLL + u
            token = g * (DMA_UNROLL // TOP_K) + u // TOP_K
            pltpu.make_async_copy(_tile_rows(hn_ref, token), _tile_rows(xs_hbm, pos_ref[0, 0, f]),
                                  sem).start(priority=u % 2)
        return c
    lax.fori_loop(0, n_rows // DMA_UNROLL, body, 0)

    total = n_rows * ROW_TILES
    pltpu.make_async_copy(xs_hbm.at[pl.ds(0, total), :], xs_hbm.at[pl.ds(0, total), :], sem).wait()

    @pl.when(i == pl.num_programs(0) - 1)
    def _():
        _zero_runs(pad_start_ref, pad_count_ref, nt_ref, xs_hbm, zero_scr, zero_sem, wait=True)


def _dispatch(pos, pad_start, pad_count, n_tiles, hn_tiles, n_slots):
    t = hn_tiles.shape[0] // ROW_TILES
    tm = TM_DISPATCH
    steps = t // tm
    assert t % tm == 0
    grid_spec = pltpu.PrefetchScalarGridSpec(
        num_scalar_prefetch=3,
        grid=(steps,),
        in_specs=[pl.BlockSpec((1, 1, tm * TOP_K), lambda i, *_: (i, 0, 0), memory_space=pltpu.SMEM),
                  pl.BlockSpec((tm * ROW_TILES, LANES), lambda i, *_: (i, 0))],
        out_specs=pl.BlockSpec(memory_space=pl.ANY),
        scratch_shapes=[pltpu.VMEM((TM_MOE * ROW_TILES, LANES), F32), pltpu.SemaphoreType.DMA,
                        pltpu.SemaphoreType.DMA],
    )
    return pl.pallas_call(
        _dispatch_kernel,
        grid_spec=grid_spec,
        out_shape=jax.ShapeDtypeStruct((n_slots * ROW_TILES, LANES), F32),
        compiler_params=pltpu.CompilerParams(dimension_semantics=("arbitrary",), vmem_limit_bytes=VMEM_LIMIT),
        name="dispatch",
    )(pad_start, pad_count, n_tiles, pos.reshape(steps, 1, tm * TOP_K), hn_tiles)


def _moe_kernel(te_ref, nt_ref, x_ref, wgu_ref, bgu_ref, wd_ref, bd_ref, y_ref, wgu_bf, wd_bf, act_scr):
    tm = TM_MOE
    i = pl.program_id(0)

    @pl.when(i < nt_ref[0])
    def _():
        @pl.when((i == 0) | (te_ref[i] != te_ref[jnp.maximum(i - 1, 0)]))
        def _():
            wgu_bf[...] = wgu_ref[0].astype(BF16)
            wd_bf[...] = wd_ref[0].astype(BF16)

        x = jnp.concatenate([x_ref[pl.ds(j, tm, stride=ROW_TILES), :] for j in range(ROW_TILES)],
                            axis=1).astype(BF16)
        width = D_FF // MOE_FF_CHUNKS
        for c in range(MOE_FF_CHUNKS):
            lo, hi = c * width, (c + 1) * width
            gate = _dot(x, wgu_bf[:, lo:hi]) + bgu_ref[0, :, lo:hi]
            up = _dot(x, wgu_bf[:, D_FF + lo:D_FF + hi]) + bgu_ref[0, :, D_FF + lo:D_FF + hi]
            gate = jnp.minimum(gate, SWIGLU_LIMIT)
            up = jnp.clip(up, -SWIGLU_LIMIT, SWIGLU_LIMIT)
            act_scr[:, lo:hi] = ((up + 1.0) * (gate * jax.nn.sigmoid(gate * SWIGLU_ALPHA))).astype(BF16)
        y = _dot(act_scr[...], wd_bf[...]) + bd_ref[0]
        for j in range(ROW_TILES):
            y_ref[pl.ds(j, tm, stride=ROW_TILES), :] = y[:, j * LANES:(j + 1) * LANES]

    @pl.when(i >= nt_ref[0])
    def _():
        y_ref[...] = jnp.zeros_like(y_ref)


def _experts(tile_expert, n_tiles, xs, wgu, bgu, wd, bd, nt_max):
    tm = TM_MOE
    tile_in = pl.BlockSpec((tm * ROW_TILES, LANES), lambda i, te, nt: (jnp.minimum(i, nt[0] - 1), 0))
    tile = pl.BlockSpec((tm * ROW_TILES, LANES), lambda i, te, nt: (i, 0))
    grid_spec = pltpu.PrefetchScalarGridSpec(
        num_scalar_prefetch=2,
        grid=(nt_max,),
        in_specs=[tile_in,
                  pl.BlockSpec((1, D_MODEL, 2 * D_FF), lambda i, te, nt: (te[i], 0, 0)),
                  pl.BlockSpec((1, 1, 2 * D_FF), lambda i, te, nt: (te[i], 0, 0)),
                  pl.BlockSpec((1, D_FF, D_MODEL), lambda i, te, nt: (te[i], 0, 0)),
                  pl.BlockSpec((1, 1, D_MODEL), lambda i, te, nt: (te[i], 0, 0))],
        out_specs=tile,
        scratch_shapes=[pltpu.VMEM((D_MODEL, 2 * D_FF), BF16), pltpu.VMEM((D_FF, D_MODEL), BF16),
                        pltpu.VMEM((tm, D_FF), BF16)],
    )
    return pl.pallas_call(
        _moe_kernel,
        grid_spec=grid_spec,
        out_shape=jax.ShapeDtypeStruct((nt_max * tm * ROW_TILES, LANES), F32),
        compiler_params=pltpu.CompilerParams(dimension_semantics=("arbitrary",), vmem_limit_bytes=VMEM_LIMIT),
        name="experts",
    )(tile_expert, n_tiles, xs, wgu, bgu, wd, bd)


def _combine_kernel(pos_ref, pos_next_ref, ys_hbm, gate_ref, h_ref, g_ref, o_ref, buf0, buf1, acc_scr, sem):
    tm = h_ref.shape[0]
    n_rows = tm * TOP_K
    i = pl.program_id(0)
    bufs = (buf0, buf1)

    def gather(idx_ref, buf, s):
        def body(g, c):
            for u in range(DMA_UNROLL):
                f = g * DMA_UNROLL + u
                token = g * (DMA_UNROLL // TOP_K) + u // TOP_K
                pltpu.make_async_copy(_tile_rows(ys_hbm, idx_ref[0, 0, f]),
                                      _tile_rows(buf, (u % TOP_K) * tm + token), sem.at[s]).start(priority=u % 2)
            return c
        lax.fori_loop(0, n_rows // DMA_UNROLL, body, 0)

    @pl.when(i == 0)
    def _():
        gather(pos_ref, buf0, 0)

    for p in (0, 1):
        @pl.when(i % 2 == p)
        def _(p=p):
            @pl.when(i + 1 < pl.num_programs(0))
            def _():
                gather(pos_next_ref, bufs[1 - p], 1 - p)

            buf = bufs[p]
            pltpu.make_async_copy(ys_hbm.at[pl.ds(0, n_rows * ROW_TILES), :], buf, sem.at[p]).wait()
            gates = gate_ref[...]
            gk = [jnp.broadcast_to(gates[:, kk:kk + 1], (tm, LANES)) for kk in range(TOP_K)]
            ss = jnp.zeros((tm, 1), F32)
            for j in range(ROW_TILES):
                cols = slice(j * LANES, (j + 1) * LANES)
                moe = gk[0] * buf[pl.ds(j, tm, stride=ROW_TILES), :]
                for kk in range(1, TOP_K):
                    moe = moe + gk[kk] * buf[pl.ds(kk * tm * ROW_TILES + j, tm, stride=ROW_TILES), :]
                acc = h_ref[:, cols] + moe
                acc_scr[:, cols] = acc
                ss = ss + jnp.sum(acc * acc, axis=-1, keepdims=True)
            inv = lax.rsqrt(ss / D_MODEL + NORM_EPS)
            o_ref[...] = acc_scr[...] * inv * g_ref[...]


def _combine(pos, ys, gates, h, g):
    t = h.shape[0]
    tm = TM_COMB
    steps = t // tm
    pos_blocks = pos.reshape(steps, 1, tm * TOP_K)
    smem = lambda f: pl.BlockSpec((1, 1, tm * TOP_K), f, memory_space=pltpu.SMEM)
    row_buf = pltpu.VMEM((TOP_K * tm * ROW_TILES, LANES), F32)
    return pl.pallas_call(
        _combine_kernel,
        grid=(steps,),
        in_specs=[smem(lambda i: (i, 0, 0)),
                  smem(lambda i: (jnp.minimum(i + 1, steps - 1), 0, 0)),
                  pl.BlockSpec(memory_space=pl.ANY),
                  pl.BlockSpec((tm, LANES), lambda i: (i, 0)),
                  pl.BlockSpec((tm, D_MODEL), lambda i: (i, 0)),
                  pl.BlockSpec((1, D_MODEL), lambda i: (0, 0))],
        out_specs=pl.BlockSpec((tm, D_MODEL), lambda i: (i, 0)),
        out_shape=jax.ShapeDtypeStruct((t, D_MODEL), F32),
        scratch_shapes=[row_buf, row_buf, pltpu.VMEM((tm, D_MODEL), F32), pltpu.SemaphoreType.DMA((2,))],
        compiler_params=pltpu.CompilerParams(dimension_semantics=("arbitrary",), vmem_limit_bytes=VMEM_LIMIT),
        name="combine",
    )(pos_blocks, pos_blocks, ys, gates, h, g)


def _routing_plan(idx_rank, counts_f, n_tokens):
    tm = TM_MOE
    nt_max = n_tokens * TOP_K // tm + N_EXPERTS
    experts = jnp.arange(N_EXPERTS, dtype=I32)
    counts = counts_f[0, :N_EXPERTS].astype(I32)
    tiles_e = (counts + tm - 1) // tm
    tile_end = jnp.cumsum(tiles_e)
    first_slot = (tile_end - tiles_e) * tm
    n_tiles = tile_end[-1]
    tile_ids = jnp.arange(nt_max, dtype=I32)
    te = jnp.minimum(jnp.sum((tile_ids[:, None] >= tile_end[None, :]).astype(I32), axis=1), N_EXPERTS - 1)
    last_e = jnp.take(te, jnp.maximum(n_tiles - 1, 0))
    te = jnp.where(tile_ids < n_tiles, te, last_e).astype(I32)
    expert = idx_rank[:, :TOP_K]
    rank = idx_rank[:, TOP_K:2 * TOP_K]
    pos = jnp.sum(jnp.where(expert[:, :, None] == experts[None, None, :], first_slot[None, None, :], 0), axis=-1) + rank
    return (te, n_tiles.reshape(1).astype(I32), pos.reshape(-1).astype(I32), (first_slot + counts).astype(I32),
            (tiles_e * tm - counts).astype(I32), nt_max)


def kernel(x, norm1_g, w_in, a_ln_g, a_ln_b, a_w_s, a_b_s, a_out_g, b_out_g, w_out, norm2_g, w_router,
           b_router, w_gate_up, b_gate_up, w_down, b_down, normf_g):
    batch, seq, _ = x.shape
    t = batch * seq
    assert seq % (TM_PROJ) == 0 and t % TM_MOE == 0 and seq // DILATIONS[-1] == BAND
    h = x.reshape(t, D_MODEL)

    pos = jnp.arange(seq, dtype=F32)
    inv = ROPE_THETA ** (-jnp.arange(0, HEAD_DIM, 2, dtype=F32) / HEAD_DIM)
    ang = pos[:, None] * inv[None, :]
    cos = jnp.tile(jnp.cos(ang), (1, 2 * LANES // HEAD_DIM))
    sin = jnp.tile(jnp.concatenate([-jnp.sin(ang), jnp.sin(ang)], axis=1), (1, LANES // HEAD_DIM))
    head_of_lane = np.arange(A_WIDTH) // HEAD_DIM
    avg = jnp.asarray((head_of_lane[:, None] == head_of_lane[None, :]).astype(np.float32) / HEAD_DIM, dtype=BF16)
    row2 = lambda v: v.reshape(1, -1).astype(F32)

    for layer in range(norm1_g.shape[0]):
        causal = np.tril(np.ones((CHUNK, CHUNK), dtype=bool))
        ws = jnp.where(causal[None], a_w_s[layer], 0.0).astype(BF16)
        bs = jnp.repeat(a_b_s[layer].astype(F32).T, HEAD_DIM, axis=1)
        a_out, *qkv = _inproj(h, row2(norm1_g[layer]), w_in[layer].astype(BF16), avg,
                              row2(a_ln_g[layer]), row2(a_ln_b[layer]), ws, bs, row2(a_out_g[layer]),
                              cos, sin, seq)
        b_mix = _attention(qkv, batch, seq)
        w_r = jnp.pad(w_router[layer], ((0, 0), (0, LANES - N_EXPERTS))).astype(BF16)
        b_r = jnp.concatenate([b_router[layer].astype(F32), jnp.full((LANES - N_EXPERTS,), NEG_INF, F32)])
        h_mid, hn, idx_rank, gates, counts = _mix(a_out, b_mix, h, row2(b_out_g[layer]), w_out[layer].astype(BF16),
                                                  row2(norm2_g[layer]), w_r, b_r.reshape(1, LANES))
        te, n_tiles, pos, pad_start, pad_count, nt_max = _routing_plan(idx_rank, counts, t)
        xs = _dispatch(pos, pad_start, pad_count, n_tiles, hn, nt_max * TM_MOE)
        ys = _experts(te, n_tiles, xs,
                      w_gate_up[layer], b_gate_up[layer].reshape(N_EXPERTS, 1, 2 * D_FF),
                      w_down[layer], b_down[layer].reshape(N_EXPERTS, 1, D_MODEL), nt_max)
        last = layer == norm1_g.shape[0] - 1
        assert last, "the combine kernel fuses the final norm; depth > 1 is not supported"
        h = _combine(pos, ys, gates, h_mid, row2(normf_g))
    return h.reshape(batch, seq, D_MODEL)
```

```python
import math

import numpy as np
import jax
import jax.numpy as jnp
from jax import lax
from jax.experimental import pallas as pl
from jax.experimental.pallas import tpu as pltpu

F32 = jnp.float32
BF16 = jnp.bfloat16
I32 = jnp.int32

D_MODEL = 1024
HEAD_DIM = 64
A_WIDTH = 512
B_WIDTH = 512
CHUNK = 128
BAND = 128
DILATIONS = (1, 4, 16)
ROPE_THETA = 10000.0
N_EXPERTS = 32
TOP_K = 4
D_FF = 1024
SWIGLU_ALPHA = 1.702
SWIGLU_LIMIT = 7.0
NORM_EPS = 1e-5
NEG_INF = -1e30

LANES = 128
SUBLANES = 8
ROW_TILES = D_MODEL // LANES

TM_PROJ = 512
PROJ_SUBTILES = 1
TM_DISPATCH = 2048
TM_MOE = 512
TM_COMB = 512
DMA_UNROLL = 16
MOE_FF_CHUNKS = 2
VMEM_LIMIT = 56 * 1024 * 1024


def _dot(a, b):
    return jnp.dot(a, b, preferred_element_type=F32)


def _gelu_tanh(x):
    c = math.sqrt(2.0 / math.pi)
    cdf = 0.5 * (1.0 + jnp.tanh(c * (x + 0.044715 * (x * x * x))))
    return x * cdf


def _rms(x, g):
    return x * lax.rsqrt(jnp.mean(x * x, axis=-1, keepdims=True) + NORM_EPS) * g


def _inproj_kernel(x_ref, g1_ref, w_ref, avg_ref, lng_ref, lnb_ref, ws_ref, bs_ref, aog_ref,
                   cos_ref, sin_ref, a_ref, q_ref, k_ref, v_ref, q4_ref, k4_ref, v4_ref,
                   q16_ref, k16_ref, v16_ref, a_scr, qkv_scr, cls_scr):
    tm = x_ref.shape[0]
    n = tm // PROJ_SUBTILES
    nlt = B_WIDTH // LANES
    d4, d16 = DILATIONS[1], DILATIONS[2]
    first_head = lax.broadcasted_iota(I32, (CHUNK, LANES), 1) < HEAD_DIM
    first_half = (lax.broadcasted_iota(I32, (n, LANES), 1) % HEAD_DIM) < (HEAD_DIM // 2)
    avg = avg_ref[...]

    for sub in range(PROJ_SUBTILES):
        r0 = sub * n
        rows_n = slice(r0, r0 + n)
        xn = _rms(x_ref[rows_n, :], g1_ref[...]).astype(BF16)

        ug = _gelu_tanh(_dot(xn, w_ref[:, 0:A_WIDTH]))
        vg = _gelu_tanh(_dot(xn, w_ref[:, A_WIDTH:2 * A_WIDTH]))
        mu = _dot(vg.astype(BF16), avg)
        d = vg - mu
        var = _dot((d * d).astype(BF16), avg)
        vn = (d * lax.rsqrt(var + NORM_EPS) * lng_ref[...] + lnb_ref[...]).astype(BF16)
        for c in range(n // CHUNK):
            rows = slice(c * CHUNK, (c + 1) * CHUNK)
            for p in range(A_WIDTH // LANES):
                cols = slice(p * LANES, (p + 1) * LANES)
                slab = vn[rows, cols]
                g = jnp.where(first_head, _dot(ws_ref[2 * p], slab), _dot(ws_ref[2 * p + 1], slab))
                a_scr[r0 + c * CHUNK:r0 + (c + 1) * CHUNK, cols] = ug[rows, cols] * (g + bs_ref[:, cols])
        a_ref[rows_n, :] = _rms(a_scr[rows_n, :], aog_ref[...]).astype(BF16)

        cos = cos_ref[rows_n, :]
        sin = sin_ref[rows_n, :]

        def rope(t):
            rot = jnp.where(first_half, pltpu.roll(t, LANES - HEAD_DIM // 2, 1), pltpu.roll(t, HEAD_DIM // 2, 1))
            return t * cos + rot * sin

        off = 2 * A_WIDTH
        for p in range(nlt):
            q = _dot(xn, w_ref[:, off + p * LANES: off + (p + 1) * LANES])
            qkv_scr[p, rows_n, :] = rope(q) * (HEAD_DIM ** -0.5)
            k = _dot(xn, w_ref[:, off + B_WIDTH + p * LANES: off + B_WIDTH + (p + 1) * LANES])
            qkv_scr[nlt + p, rows_n, :] = rope(k)
            qkv_scr[2 * nlt + p, rows_n, :] = _dot(xn, w_ref[:, off + 2 * B_WIDTH + p * LANES:
                                                             off + 2 * B_WIDTH + (p + 1) * LANES])

        rows4 = slice(r0 // d4, (r0 + n) // d4)
        rows16 = slice(r0 // d16, (r0 + n) // d16)
        for i, (nat_ref, c4_ref, c16_ref) in enumerate(((q_ref, q4_ref, q16_ref), (k_ref, k4_ref, k16_ref),
                                                        (v_ref, v4_ref, v16_ref))):
            for p in range(nlt):
                cols = slice(p * LANES, (p + 1) * LANES)
                nat_ref[rows_n, cols] = qkv_scr[i * nlt + p, rows_n, :].astype(BF16)
                for r4 in range(d4):
                    cls = qkv_scr[i * nlt + p, pl.ds(r0 + r4, n // d4, stride=d4), :]
                    c4_ref[0, r4, rows4, cols] = cls.astype(BF16)
                    cls_scr[i * nlt + p, r4, rows4, :] = cls
                for r4 in range(d4):
                    for a in range(d16 // d4):
                        c16_ref[0, r4 + d4 * a, rows16, cols] = cls_scr[
                            i * nlt + p, r4, pl.ds(r0 // d4 + a, n // d16, stride=d16 // d4), :].astype(BF16)


def _inproj(x2, g1, w_in, avg, lng, lnb, ws, bs, aog, cos, sin, seq):
    t = x2.shape[0]
    tm = TM_PROJ
    nseq = seq // tm
    full = lambda shape: pl.BlockSpec(shape, lambda i: (0,) * len(shape))
    rows = lambda w: pl.BlockSpec((tm, w), lambda i: (i, 0))
    classes = lambda dil: pl.BlockSpec((1, dil, tm // dil, B_WIDTH), lambda i: (i // nseq, 0, i % nseq, 0))
    class_shape = lambda dil: jax.ShapeDtypeStruct((t // seq, dil, seq // dil, B_WIDTH), BF16)
    return pl.pallas_call(
        _inproj_kernel,
        grid=(t // tm,),
        in_specs=[rows(D_MODEL), full((1, D_MODEL)), full(w_in.shape), full(avg.shape),
                  full((1, A_WIDTH)), full((1, A_WIDTH)), full(ws.shape), full(bs.shape),
                  full((1, A_WIDTH)),
                  pl.BlockSpec((tm, LANES), lambda i: (i % nseq, 0)),
                  pl.BlockSpec((tm, LANES), lambda i: (i % nseq, 0))],
        out_specs=[rows(A_WIDTH)] + [rows(B_WIDTH)] * 3 + [classes(DILATIONS[1])] * 3 + [classes(DILATIONS[2])] * 3,
        out_shape=([jax.ShapeDtypeStruct((t, A_WIDTH), BF16)] + [jax.ShapeDtypeStruct((t, B_WIDTH), BF16)] * 3
                   + [class_shape(DILATIONS[1])] * 3 + [class_shape(DILATIONS[2])] * 3),
        scratch_shapes=[pltpu.VMEM((tm, A_WIDTH), F32), pltpu.VMEM((3 * B_WIDTH // LANES, tm, LANES), F32),
                        pltpu.VMEM((3 * B_WIDTH // LANES, DILATIONS[1], tm // DILATIONS[1], LANES), F32)],
        compiler_params=pltpu.CompilerParams(dimension_semantics=("arbitrary",), vmem_limit_bytes=VMEM_LIMIT),
        name="inproj",
    )(x2, g1, w_in, avg, lng, lnb, ws, bs, aog, cos, sin)


def _attn_block(qb, kw, vw, bias, first_head):
    zero = jnp.zeros_like(qb)
    q2 = jnp.concatenate([jnp.where(first_head, qb, zero), jnp.where(first_head, zero, qb)], axis=0)
    s = lax.dot_general(q2, kw, (((1,), (1,)), ((), ())), preferred_element_type=F32) + bias
    m = jnp.max(s, axis=-1, keepdims=True)
    p = jnp.exp(s - m)
    l = jnp.sum(p, axis=-1, keepdims=True)
    o = _dot(p.astype(BF16), vw) / l
    lse = jnp.broadcast_to(m + jnp.log(l), o.shape)
    return (jnp.where(first_head, o[:BAND], o[BAND:]),
            jnp.where(first_head, lse[:BAND], lse[BAND:]))


def _attn_kernel(q1, k1, v1, q4, k4, v4, q16, k16, v16, bias_band_ref, bias_first_ref, o_ref, o_scr, l_scr):
    seq = q1.shape[0]
    first_head = lax.broadcasted_iota(I32, (BAND, LANES), 1) < HEAD_DIM
    branches = ((q1, k1, v1), (q4, k4, v4), (q16, k16, v16))
    for bi, dil in enumerate(DILATIONS):
        q_r, k_r, v_r = branches[bi]
        length = seq // dil
        for r in range(dil):
            for n in range(length // BAND):
                lo = n * BAND
                if bi == 0:
                    ref_slice = lambda ref, a, b: ref[a:b, :]
                else:
                    ref_slice = lambda ref, a, b, r=r: ref[0, r, a:b, :]
                qb = ref_slice(q_r, lo, lo + BAND)
                if n == 0:
                    kw, vw, bias = ref_slice(k_r, 0, BAND), ref_slice(v_r, 0, BAND), bias_first_ref[...]
                else:
                    kw, vw = ref_slice(k_r, lo - BAND, lo + BAND), ref_slice(v_r, lo - BAND, lo + BAND)
                    bias = bias_band_ref[...]
                o, lse = _attn_block(qb, kw, vw, bias, first_head)
                if dil == 1:
                    o_scr[bi, lo:lo + BAND, :] = o
                    l_scr[bi, lo:lo + BAND, :] = lse
                else:
                    dst = pl.ds(r + dil * lo, BAND, stride=dil)
                    o_scr[bi, dst, :] = o
                    l_scr[bi, dst, :] = lse
    lses = [l_scr[i] for i in range(3)]
    m = jnp.maximum(jnp.maximum(lses[0], lses[1]), lses[2])
    es = [jnp.exp(l - m) for l in lses]
    den = es[0] + es[1] + es[2]
    o_ref[...] = (es[0] / den) * o_scr[0] + (es[1] / den) * o_scr[1] + (es[2] / den) * o_scr[2]


def _attention(qkv, batch, seq):
    rel = (np.arange(BAND)[:, None] + BAND) - np.arange(2 * BAND)[None, :]
    band = np.where((rel >= 0) & (rel <= BAND), 0.0, NEG_INF).astype(np.float32)
    bias_band = jnp.asarray(np.concatenate([band, band], axis=0))
    bias_first = jnp.asarray(np.concatenate([band[:, BAND:], band[:, BAND:]], axis=0))
    nat = pl.BlockSpec((seq, LANES), lambda b, p: (b, p))
    cls = lambda dil: pl.BlockSpec((1, dil, seq // dil, LANES), lambda b, p: (b, 0, 0, p))
    full = lambda a: pl.BlockSpec(a.shape, lambda b, p: (0, 0))
    return pl.pallas_call(
        _attn_kernel,
        grid=(batch, B_WIDTH // LANES),
        in_specs=[nat] * 3 + [cls(4)] * 3 + [cls(16)] * 3 + [full(bias_band), full(bias_first)],
        out_specs=pl.BlockSpec((seq, LANES), lambda b, p: (b, p)),
        out_shape=jax.ShapeDtypeStruct((batch * seq, B_WIDTH), F32),
        scratch_shapes=[pltpu.VMEM((3, seq, LANES), F32), pltpu.VMEM((3, seq, LANES), F32)],
        compiler_params=pltpu.CompilerParams(dimension_semantics=("arbitrary", "arbitrary"),
                                             vmem_limit_bytes=VMEM_LIMIT),
        name="dilated_attention",
    )(*qkv, bias_band, bias_first)


def _mix_kernel(a_ref, bm_ref, x_ref, bog_ref, wout_ref, n2g_ref, wr_ref, br_ref, tri_ref,
                h_ref, hn_ref, idx_ref, gate_ref, cnt_ref, cnt_scr):
    tm = x_ref.shape[0]
    bn = _rms(bm_ref[...], bog_ref[...]).astype(BF16)
    mixed = jnp.concatenate([a_ref[...], bn], axis=1)
    h = x_ref[...] + _dot(mixed, wout_ref[...])
    h_ref[...] = h
    hn = _rms(h, n2g_ref[...])
    for j in range(ROW_TILES):
        hn_ref[pl.ds(j, tm, stride=ROW_TILES), :] = hn[:, j * LANES:(j + 1) * LANES]
    logits = _dot(hn.astype(BF16), wr_ref[...]) + br_ref[...]
    lane = lax.broadcasted_iota(I32, (tm, LANES), 1)
    vals, idxs = [], []
    for _ in range(TOP_K):
        m = jnp.max(logits, axis=-1, keepdims=True)
        am = jnp.min(jnp.where(logits == m, lane, LANES), axis=-1, keepdims=True)
        vals.append(m)
        idxs.append(am)
        logits = jnp.where(lane == am, -jnp.inf, logits)
    es = [jnp.exp(v - vals[0]) for v in vals]
    den = es[0] + es[1] + es[2] + es[3]
    @pl.when(pl.program_id(0) == 0)
    def _():
        cnt_scr[...] = jnp.zeros_like(cnt_scr)

    chosen = jnp.zeros((tm, LANES), F32)
    for kk in range(TOP_K):
        chosen = chosen + (lane == idxs[kk]).astype(F32)
    before = _dot(tri_ref[...], chosen.astype(BF16)) + cnt_scr[...]
    cnt_scr[...] = cnt_scr[...] + jnp.sum(chosen, axis=0, keepdims=True)
    cnt_ref[...] = jnp.broadcast_to(cnt_scr[...], cnt_ref.shape)

    idx_out = jnp.zeros((tm, LANES), I32)
    gate_out = jnp.zeros((tm, LANES), F32)
    for kk in range(TOP_K):
        rank = jnp.sum(jnp.where(lane == idxs[kk], before, 0.0), axis=-1, keepdims=True).astype(I32)
        idx_out = jnp.where(lane == kk, idxs[kk], idx_out)
        idx_out = jnp.where(lane == TOP_K + kk, rank, idx_out)
        gate_out = jnp.where(lane == kk, es[kk] / den, gate_out)
    idx_ref[...] = idx_out
    gate_ref[...] = gate_out


def _mix(a_out, b_mix, x2, bog, w_out, n2g, w_r, b_r):
    t = x2.shape[0]
    tm = TM_PROJ
    full = lambda shape: pl.BlockSpec(shape, lambda i: (0,) * len(shape))
    rows = lambda w: pl.BlockSpec((tm, w), lambda i: (i, 0))
    tri = jnp.asarray(np.tril(np.ones((tm, tm), np.float32), -1), dtype=BF16)
    return pl.pallas_call(
        _mix_kernel,
        grid=(t // tm,),
        in_specs=[rows(A_WIDTH), rows(B_WIDTH), rows(D_MODEL), full((1, B_WIDTH)), full(w_out.shape),
                  full((1, D_MODEL)), full(w_r.shape), full((1, LANES)), full((tm, tm))],
        out_specs=[rows(D_MODEL), pl.BlockSpec((tm * ROW_TILES, LANES), lambda i: (i, 0)), rows(LANES), rows(LANES),
                   full((SUBLANES, LANES))],
        out_shape=[jax.ShapeDtypeStruct((t, D_MODEL), F32), jax.ShapeDtypeStruct((t * ROW_TILES, LANES), F32),
                   jax.ShapeDtypeStruct((t, LANES), I32), jax.ShapeDtypeStruct((t, LANES), F32),
                   jax.ShapeDtypeStruct((SUBLANES, LANES), F32)],
        scratch_shapes=[pltpu.VMEM((1, LANES), F32)],
        compiler_params=pltpu.CompilerParams(dimension_semantics=("arbitrary",), vmem_limit_bytes=VMEM_LIMIT),
        name="mix_router",
    )(a_out, b_mix, x2, bog, w_out, n2g, w_r, b_r, tri)


def _tile_rows(ref, row):
    return ref.at[pl.ds(pl.multiple_of(row * ROW_TILES, SUBLANES), ROW_TILES), :]


def _zero_runs(pad_start_ref, pad_count_ref, nt_ref, xs_hbm, zero_scr, sem, wait):
    tile_rows = TM_MOE * ROW_TILES

    def copy(rows, first_slot):
        c = pltpu.make_async_copy(zero_scr.at[pl.ds(0, rows * ROW_TILES), :],
                                  xs_hbm.at[pl.ds(pl.multiple_of(first_slot * ROW_TILES, SUBLANES), rows * ROW_TILES), :],
                                  sem)
        c.wait() if wait else c.start()

    def expert_body(e, c):
        count = pad_count_ref[e]
        slot = pad_start_ref[e]
        bit = TM_MOE // 2
        while bit:
            @pl.when((count & bit) != 0)
            def _(bit=bit, slot=slot):
                copy(bit, slot)
            slot = slot + (count & bit)
            bit //= 2
        return c
    lax.fori_loop(0, N_EXPERTS, expert_body, 0)

    def idle_body(tile, c):
        copy(TM_MOE, tile * TM_MOE)
        return c
    lax.fori_loop(nt_ref[0], xs_hbm.shape[0] // tile_rows, idle_body, 0)


def _dispatch_kernel(pad_start_ref, pad_count_ref, nt_ref, pos_ref, hn_ref, xs_hbm, zero_scr, sem, zero_sem):
    n_rows = pos_ref.shape[-1]
    i = pl.program_id(0)

    @pl.when(i == 0)
    def _():
        zero_scr[...] = jnp.zeros_like(zero_scr)
        _zero_runs(pad_start_ref, pad_count_ref, nt_ref, xs_hbm, zero_scr, zero_sem, wait=False)

    def body(g, c):
        for u in range(DMA_UNROLL):
            f = g * DMA_UNROLL + u
            token = g * (DMA_UNROLL // TOP_K) + u // TOP_K
            pltpu.make_async_copy(_tile_rows(hn_ref, token), _tile_rows(xs_hbm, pos_ref[0, 0, f]),
                                  sem).start(priority=u % 2)
        return c
    lax.fori_loop(0, n_rows // DMA_UNROLL, body, 0)

    total = n_rows * ROW_TILES
    pltpu.make_async_copy(xs_hbm.at[pl.ds(0, total), :], xs_hbm.at[pl.ds(0, total), :], sem).wait()

    @pl.when(i == pl.num_programs(0) - 1)
    def _():
        _zero_runs(pad_start_ref, pad_count_ref, nt_ref, xs_hbm, zero_scr, zero_sem, wait=True)


def _dispatch(pos, pad_start, pad_count, n_tiles, hn_tiles, n_slots):
    t = hn_tiles.shape[0] // ROW_TILES
    tm = TM_DISPATCH
    steps = t // tm
    assert t % tm == 0
    grid_spec = pltpu.PrefetchScalarGridSpec(
        num_scalar_prefetch=3,
        grid=(steps,),
        in_specs=[pl.BlockSpec((1, 1, tm * TOP_K), lambda i, *_: (i, 0, 0), memory_space=pltpu.SMEM),
                  pl.BlockSpec((tm * ROW_TILES, LANES), lambda i, *_: (i, 0))],
        out_specs=pl.BlockSpec(memory_space=pl.ANY),
        scratch_shapes=[pltpu.VMEM((TM_MOE * ROW_TILES, LANES), F32), pltpu.SemaphoreType.DMA,
                        pltpu.SemaphoreType.DMA],
    )
    return pl.pallas_call(
        _dispatch_kernel,
        grid_spec=grid_spec,
        out_shape=jax.ShapeDtypeStruct((n_slots * ROW_TILES, LANES), F32),
        compiler_params=pltpu.CompilerParams(dimension_semantics=("arbitrary",), vmem_limit_bytes=VMEM_LIMIT),
        name="dispatch",
    )(pad_start, pad_count, n_tiles, pos.reshape(steps, 1, tm * TOP_K), hn_tiles)


def _moe_kernel(te_ref, nt_ref, x_ref, wgu_ref, bgu_ref, wd_ref, bd_ref, y_ref, wgu_bf, wd_bf, act_scr):
    tm = TM_MOE
    i = pl.program_id(0)

    @pl.when(i < nt_ref[0])
    def _():
        @pl.when((i == 0) | (te_ref[i] != te_ref[jnp.maximum(i - 1, 0)]))
        def _():
            wgu_bf[...] = wgu_ref[0].astype(BF16)
            wd_bf[...] = wd_ref[0].astype(BF16)

        x = jnp.concatenate([x_ref[pl.ds(j, tm, stride=ROW_TILES), :] for j in range(ROW_TILES)],
                            axis=1).astype(BF16)
        width = D_FF // MOE_FF_CHUNKS
        for c in range(MOE_FF_CHUNKS):
            lo, hi = c * width, (c + 1) * width
            gate = _dot(x, wgu_bf[:, lo:hi]) + bgu_ref[0, :, lo:hi]
            up = _dot(x, wgu_bf[:, D_FF + lo:D_FF + hi]) + bgu_ref[0, :, D_FF + lo:D_FF + hi]
            gate = jnp.minimum(gate, SWIGLU_LIMIT)
            up = jnp.clip(up, -SWIGLU_LIMIT, SWIGLU_LIMIT)
            act_scr[:, lo:hi] = ((up + 1.0) * (gate * jax.nn.sigmoid(gate * SWIGLU_ALPHA))).astype(BF16)
        y = _dot(act_scr[...], wd_bf[...]) + bd_ref[0]
        for j in range(ROW_TILES):
            y_ref[pl.ds(j, tm, stride=ROW_TILES), :] = y[:, j * LANES:(j + 1) * LANES]

    @pl.when(i >= nt_ref[0])
    def _():
        y_ref[...] = jnp.zeros_like(y_ref)


def _experts(tile_expert, n_tiles, xs, wgu, bgu, wd, bd, nt_max):
    tm = TM_MOE
    tile_in = pl.BlockSpec((tm * ROW_TILES, LANES), lambda i, te, nt: (jnp.minimum(i, nt[0] - 1), 0))
    tile = pl.BlockSpec((tm * ROW_TILES, LANES), lambda i, te, nt: (i, 0))
    grid_spec = pltpu.PrefetchScalarGridSpec(
        num_scalar_prefetch=2,
        grid=(nt_max,),
        in_specs=[tile_in,
                  pl.BlockSpec((1, D_MODEL, 2 * D_FF), lambda i, te, nt: (te[i], 0, 0)),
                  pl.BlockSpec((1, 1, 2 * D_FF), lambda i, te, nt: (te[i], 0, 0)),
                  pl.BlockSpec((1, D_FF, D_MODEL), lambda i, te, nt: (te[i], 0, 0)),
                  pl.BlockSpec((1, 1, D_MODEL), lambda i, te, nt: (te[i], 0, 0))],
        out_specs=tile,
        scratch_shapes=[pltpu.VMEM((D_MODEL, 2 * D_FF), BF16), pltpu.VMEM((D_FF, D_MODEL), BF16),
                        pltpu.VMEM((tm, D_FF), BF16)],
    )
    return pl.pallas_call(
        _moe_kernel,
        grid_spec=grid_spec,
        out_shape=jax.ShapeDtypeStruct((nt_max * tm * ROW_TILES, LANES), F32),
        compiler_params=pltpu.CompilerParams(dimension_semantics=("arbitrary",), vmem_limit_bytes=VMEM_LIMIT),
        name="experts",
    )(tile_expert, n_tiles, xs, wgu, bgu, wd, bd)


def _combine_kernel(pos_ref, pos_next_ref, ys_hbm, gate_ref, h_ref, g_ref, o_ref, buf0, buf1, acc_scr, sem):
    tm = h_ref.shape[0]
    n_rows = tm * TOP_K
    i = pl.program_id(0)
    bufs = (buf0, buf1)

    def gather(idx_ref, buf, s):
        def body(g, c):
            for u in range(DMA_UNROLL):
                f = g * DMA_UNROLL + u
                token = g * (DMA_UNROLL // TOP_K) + u // TOP_K
                pltpu.make_async_copy(_tile_rows(ys_hbm, idx_ref[0, 0, f]),
                                      _tile_rows(buf, (u % TOP_K) * tm + token), sem.at[s]).start(priority=u % 2)
            return c
        lax.fori_loop(0, n_rows // DMA_UNROLL, body, 0)

    @pl.when(i == 0)
    def _():
        gather(pos_ref, buf0, 0)

    for p in (0, 1):
        @pl.when(i % 2 == p)
        def _(p=p):
            @pl.when(i + 1 < pl.num_programs(0))
            def _():
                gather(pos_next_ref, bufs[1 - p], 1 - p)

            buf = bufs[p]
            pltpu.make_async_copy(ys_hbm.at[pl.ds(0, n_rows * ROW_TILES), :], buf, sem.at[p]).wait()
            gates = gate_ref[...]
            gk = [jnp.broadcast_to(gates[:, kk:kk + 1], (tm, LANES)) for kk in range(TOP_K)]
            ss = jnp.zeros((tm, 1), F32)
            for j in range(ROW_TILES):
                cols = slice(j * LANES, (j + 1) * LANES)
                moe = gk[0] * buf[pl.ds(j, tm, stride=ROW_TILES), :]
                for kk in range(1, TOP_K):
                    moe = moe + gk[kk] * buf[pl.ds(kk * tm * ROW_TILES + j, tm, stride=ROW_TILES), :]
                acc = h_ref[:, cols] + moe
                acc_scr[:, cols] = acc
                ss = ss + jnp.sum(acc * acc, axis=-1, keepdims=True)
            inv = lax.rsqrt(ss / D_MODEL + NORM_EPS)
            o_ref[...] = acc_scr[...] * inv * g_ref[...]


def _combine(pos, ys, gates, h, g):
    t = h.shape[0]
    tm = TM_COMB
    steps = t // tm
    pos_blocks = pos.reshape(steps, 1, tm * TOP_K)
    smem = lambda f: pl.BlockSpec((1, 1, tm * TOP_K), f, memory_space=pltpu.SMEM)
    row_buf = pltpu.VMEM((TOP_K * tm * ROW_TILES, LANES), F32)
    return pl.pallas_call(
        _combine_kernel,
        grid=(steps,),
        in_specs=[smem(lambda i: (i, 0, 0)),
                  smem(lambda i: (jnp.minimum(i + 1, steps - 1), 0, 0)),
                  pl.BlockSpec(memory_space=pl.ANY),
                  pl.BlockSpec((tm, LANES), lambda i: (i, 0)),
                  pl.BlockSpec((tm, D_MODEL), lambda i: (i, 0)),
                  pl.BlockSpec((1, D_MODEL), lambda i: (0, 0))],
        out_specs=pl.BlockSpec((tm, D_MODEL), lambda i: (i, 0)),
        out_shape=jax.ShapeDtypeStruct((t, D_MODEL), F32),
        scratch_shapes=[row_buf, row_buf, pltpu.VMEM((tm, D_MODEL), F32), pltpu.SemaphoreType.DMA((2,))],
        compiler_params=pltpu.CompilerParams(dimension_semantics=("arbitrary",), vmem_limit_bytes=VMEM_LIMIT),
        name="combine",
    )(pos_blocks, pos_blocks, ys, gates, h, g)


def _routing_plan(idx_rank, counts_f, n_tokens):
    tm = TM_MOE
    nt_max = n_tokens * TOP_K // tm + N_EXPERTS
    experts = jnp.arange(N_EXPERTS, dtype=I32)
    counts = counts_f[0, :N_EXPERTS].astype(I32)
    tiles_e = (counts + tm - 1) // tm
    tile_end = jnp.cumsum(tiles_e)
    first_slot = (tile_end - tiles_e) * tm
    n_tiles = tile_end[-1]
    tile_ids = jnp.arange(nt_max, dtype=I32)
    te = jnp.minimum(jnp.sum((tile_ids[:, None] >= tile_end[None, :]).astype(I32), axis=1), N_EXPERTS - 1)
    last_e = jnp.take(te, jnp.maximum(n_tiles - 1, 0))
    te = jnp.where(tile_ids < n_tiles, te, last_e).astype(I32)
    expert = idx_rank[:, :TOP_K]
    rank = idx_rank[:, TOP_K:2 * TOP_K]
    pos = jnp.sum(jnp.where(expert[:, :, None] == experts[None, None, :], first_slot[None, None, :], 0), axis=-1) + rank
    return (te, n_tiles.reshape(1).astype(I32), pos.reshape(-1).astype(I32), (first_slot + counts).astype(I32),
            (tiles_e * tm - counts).astype(I32), nt_max)


def kernel(x, norm1_g, w_in, a_ln_g, a_ln_b, a_w_s, a_b_s, a_out_g, b_out_g, w_out, norm2_g, w_router,
           b_router, w_gate_up, b_gate_up, w_down, b_down, normf_g):
    batch, seq, _ = x.shape
    t = batch * seq
    assert seq % (TM_PROJ) == 0 and t % TM_MOE == 0 and seq // DILATIONS[-1] == BAND
    h = x.reshape(t, D_MODEL)

    pos = jnp.arange(seq, dtype=F32)
    inv = ROPE_THETA ** (-jnp.arange(0, HEAD_DIM, 2, dtype=F32) / HEAD_DIM)
    ang = pos[:, None] * inv[None, :]
    cos = jnp.tile(jnp.cos(ang), (1, 2 * LANES // HEAD_DIM))
    sin = jnp.tile(jnp.concatenate([-jnp.sin(ang), jnp.sin(ang)], axis=1), (1, LANES // HEAD_DIM))
    head_of_lane = np.arange(A_WIDTH) // HEAD_DIM
    avg = jnp.asarray((head_of_lane[:, None] == head_of_lane[None, :]).astype(np.float32) / HEAD_DIM, dtype=BF16)
    row2 = lambda v: v.reshape(1, -1).astype(F32)

    for layer in range(norm1_g.shape[0]):
        causal = np.tril(np.ones((CHUNK, CHUNK), dtype=bool))
        ws = jnp.where(causal[None], a_w_s[layer], 0.0).astype(BF16)
        bs = jnp.repeat(a_b_s[layer].astype(F32).T, HEAD_DIM, axis=1)
        a_out, *qkv = _inproj(h, row2(norm1_g[layer]), w_in[layer].astype(BF16), avg,
                              row2(a_ln_g[layer]), row2(a_ln_b[layer]), ws, bs, row2(a_out_g[layer]),
                              cos, sin, seq)
        b_mix = _attention(qkv, batch, seq)
        w_r = jnp.pad(w_router[layer], ((0, 0), (0, LANES - N_EXPERTS))).astype(BF16)
        b_r = jnp.concatenate([b_router[layer].astype(F32), jnp.full((LANES - N_EXPERTS,), NEG_INF, F32)])
        h_mid, hn, idx_rank, gates, counts = _mix(a_out, b_mix, h, row2(b_out_g[layer]), w_out[layer].astype(BF16),
                                                  row2(norm2_g[layer]), w_r, b_r.reshape(1, LANES))
        te, n_tiles, pos, pad_start, pad_count, nt_max = _routing_plan(idx_rank, counts, t)
        xs = _dispatch(pos, pad_start, pad_count, n_tiles, hn, nt_max * TM_MOE)
        ys = _experts(te, n_tiles, xs,
                      w_gate_up[layer], b_gate_up[layer].reshape(N_EXPERTS, 1, 2 * D_FF),
                      w_down[layer], b_down[layer].reshape(N_EXPERTS, 1, D_MODEL), nt_max)
        last = layer == norm1_g.shape[0] - 1
        assert last, "the combine kernel fuses the final norm; depth > 1 is not supported"
        h = _combine(pos, ys, gates, h_mid, row2(normf_g))
    return h.reshape(batch, seq, D_MODEL)
```

```python
import math

import numpy as np
import jax
import jax.numpy as jnp
from jax import lax
from jax.experimental import pallas as pl
from jax.experimental.pallas import tpu as pltpu

F32 = jnp.float32
BF16 = jnp.bfloat16
I32 = jnp.int32

D_MODEL = 1024
HEAD_DIM = 64
A_WIDTH = 512
B_WIDTH = 512
CHUNK = 128
BAND = 128
DILATIONS = (1, 4, 16)
ROPE_THETA = 10000.0
N_EXPERTS = 32
TOP_K = 4
D_FF = 1024
SWIGLU_ALPHA = 1.702
SWIGLU_LIMIT = 7.0
NORM_EPS = 1e-5
NEG_INF = -1e30

LANES = 128
SUBLANES = 8
ROW_TILES = D_MODEL // LANES

TM_PROJ = 512
PROJ_SUBTILES = 1
TM_DISPATCH = 2048
TM_MOE = 512
TM_COMB = 512
DMA_UNROLL = 16
MOE_FF_CHUNKS = 2
VMEM_LIMIT = 56 * 1024 * 1024


def _dot(a, b):
    return jnp.dot(a, b, preferred_element_type=F32)


def _gelu_tanh(x):
    c = math.sqrt(2.0 / math.pi)
    cdf = 0.5 * (1.0 + jnp.tanh(c * (x + 0.044715 * (x * x * x))))
    return x * cdf


def _rms(x, g):
    return x * lax.rsqrt(jnp.mean(x * x, axis=-1, keepdims=True) + NORM_EPS) * g


def _inproj_kernel(x_ref, g1_ref, w_ref, avg_ref, lng_ref, lnb_ref, ws_ref, bs_ref, aog_ref,
                   cos_ref, sin_ref, a_ref, q_ref, k_ref, v_ref, q4_ref, k4_ref, v4_ref,
                   q16_ref, k16_ref, v16_ref, a_scr, qkv_scr, cls_scr):
    tm = x_ref.shape[0]
    n = tm // PROJ_SUBTILES
    nlt = B_WIDTH // LANES
    d4, d16 = DILATIONS[1], DILATIONS[2]
    first_head = lax.broadcasted_iota(I32, (CHUNK, LANES), 1) < HEAD_DIM
    first_half = (lax.broadcasted_iota(I32, (n, LANES), 1) % HEAD_DIM) < (HEAD_DIM // 2)
    avg = avg_ref[...]

    for sub in range(PROJ_SUBTILES):
        r0 = sub * n
        rows_n = slice(r0, r0 + n)
        xn = _rms(x_ref[rows_n, :], g1_ref[...]).astype(BF16)

        ug = _gelu_tanh(_dot(xn, w_ref[:, 0:A_WIDTH]))
        vg = _gelu_tanh(_dot(xn, w_ref[:, A_WIDTH:2 * A_WIDTH]))
        mu = _dot(vg.astype(BF16), avg)
        d = vg - mu
        var = _dot((d * d).astype(BF16), avg)
        vn = (d * lax.rsqrt(var + NORM_EPS) * lng_ref[...] + lnb_ref[...]).astype(BF16)
        for c in range(n // CHUNK):
            rows = slice(c * CHUNK, (c + 1) * CHUNK)
            for p in range(A_WIDTH // LANES):
                cols = slice(p * LANES, (p + 1) * LANES)
                slab = vn[rows, cols]
                g = jnp.where(first_head, _dot(ws_ref[2 * p], slab), _dot(ws_ref[2 * p + 1], slab))
                a_scr[r0 + c * CHUNK:r0 + (c + 1) * CHUNK, cols] = ug[rows, cols] * (g + bs_ref[:, cols])
        a_ref[rows_n, :] = _rms(a_scr[rows_n, :], aog_ref[...]).astype(BF16)

        cos = cos_ref[rows_n, :]
        sin = sin_ref[rows_n, :]

        def rope(t):
            rot = jnp.where(first_half, pltpu.roll(t, LANES - HEAD_DIM // 2, 1), pltpu.roll(t, HEAD_DIM // 2, 1))
            return t * cos + rot * sin

        off = 2 * A_WIDTH
        for p in range(nlt):
            q = _dot(xn, w_ref[:, off + p * LANES: off + (p + 1) * LANES])
            qkv_scr[p, rows_n, :] = rope(q) * (HEAD_DIM ** -0.5)
            k = _dot(xn, w_ref[:, off + B_WIDTH + p * LANES: off + B_WIDTH + (p + 1) * LANES])
            qkv_scr[nlt + p, rows_n, :] = rope(k)
            qkv_scr[2 * nlt + p, rows_n, :] = _dot(xn, w_ref[:, off + 2 * B_WIDTH + p * LANES:
                                                             off + 2 * B_WIDTH + (p + 1) * LANES])

        rows4 = slice(r0 // d4, (r0 + n) // d4)
        rows16 = slice(r0 // d16, (r0 + n) // d16)
        for i, (nat_ref, c4_ref, c16_ref) in enumerate(((q_ref, q4_ref, q16_ref), (k_ref, k4_ref, k16_ref),
                                                        (v_ref, v4_ref, v16_ref))):
            for p in range(nlt):
                cols = slice(p * LANES, (p + 1) * LANES)
                nat_ref[rows_n, cols] = qkv_scr[i * nlt + p, rows_n, :].astype(BF16)
                for r4 in range(d4):
                    cls = qkv_scr[i * nlt + p, pl.ds(r0 + r4, n // d4, stride=d4), :]
                    c4_ref[0, r4, rows4, cols] = cls.astype(BF16)
                    cls_scr[i * nlt + p, r4, rows4, :] = cls
                for r4 in range(d4):
                    for a in range(d16 // d4):
                        c16_ref[0, r4 + d4 * a, rows16, cols] = cls_scr[
                            i * nlt + p, r4, pl.ds(r0 // d4 + a, n // d16, stride=d16 // d4), :].astype(BF16)


def _inproj(x2, g1, w_in, avg, lng, lnb, ws, bs, aog, cos, sin, seq):
    t = x2.shape[0]
    tm = TM_PROJ
    nseq = seq // tm
    full = lambda shape: pl.BlockSpec(shape, lambda i: (0,) * len(shape))
    rows = lambda w: pl.BlockSpec((tm, w), lambda i: (i, 0))
    classes = lambda dil: pl.BlockSpec((1, dil, tm // dil, B_WIDTH), lambda i: (i // nseq, 0, i % nseq, 0))
    class_shape = lambda dil: jax.ShapeDtypeStruct((t // seq, dil, seq // dil, B_WIDTH), BF16)
    return pl.pallas_call(
        _inproj_kernel,
        grid=(t // tm,),
        in_specs=[rows(D_MODEL), full((1, D_MODEL)), full(w_in.shape), full(avg.shape),
                  full((1, A_WIDTH)), full((1, A_WIDTH)), full(ws.shape), full(bs.shape),
                  full((1, A_WIDTH)),
                  pl.BlockSpec((tm, LANES), lambda i: (i % nseq, 0)),
                  pl.BlockSpec((tm, LANES), lambda i: (i % nseq, 0))],
        out_specs=[rows(A_WIDTH)] + [rows(B_WIDTH)] * 3 + [classes(DILATIONS[1])] * 3 + [classes(DILATIONS[2])] * 3,
        out_shape=([jax.ShapeDtypeStruct((t, A_WIDTH), BF16)] + [jax.ShapeDtypeStruct((t, B_WIDTH), BF16)] * 3
                   + [class_shape(DILATIONS[1])] * 3 + [class_shape(DILATIONS[2])] * 3),
        scratch_shapes=[pltpu.VMEM((tm, A_WIDTH), F32), pltpu.VMEM((3 * B_WIDTH // LANES, tm, LANES), F32),
                        pltpu.VMEM((3 * B_WIDTH // LANES, DILATIONS[1], tm // DILATIONS[1], LANES), F32)],
        compiler_params=pltpu.CompilerParams(dimension_semantics=("arbitrary",), vmem_limit_bytes=VMEM_LIMIT),
        name="inproj",
    )(x2, g1, w_in, avg, lng, lnb, ws, bs, aog, cos, sin)


def _attn_block(qb, kw, vw, bias, first_head):
    zero = jnp.zeros_like(qb)
    q2 = jnp.concatenate([jnp.where(first_head, qb, zero), jnp.where(first_head, zero, qb)], axis=0)
    s = lax.dot_general(q2, kw, (((1,), (1,)), ((), ())), preferred_element_type=F32) + bias
    m = jnp.max(s, axis=-1, keepdims=True)
    p = jnp.exp(s - m)
    l = jnp.sum(p, axis=-1, keepdims=True)
    o = _dot(p.astype(BF16), vw) / l
    lse = jnp.broadcast_to(m + jnp.log(l), o.shape)
    return (jnp.where(first_head, o[:BAND], o[BAND:]),
            jnp.where(first_head, lse[:BAND], lse[BAND:]))


def _attn_kernel(q1, k1, v1, q4, k4, v4, q16, k16, v16, bias_band_ref, bias_first_ref, o_ref, o_scr, l_scr):
    seq = q1.shape[0]
    first_head = lax.broadcasted_iota(I32, (BAND, LANES), 1) < HEAD_DIM
    branches = ((q1, k1, v1), (q4, k4, v4), (q16, k16, v16))
    for bi, dil in enumerate(DILATIONS):
        q_r, k_r, v_r = branches[bi]
        length = seq // dil
        for r in range(dil):
            for n in range(length // BAND):
                lo = n * BAND
                if bi == 0:
                    ref_slice = lambda ref, a, b: ref[a:b, :]
                else:
                    ref_slice = lambda ref, a, b, r=r: ref[0, r, a:b, :]
                qb = ref_slice(q_r, lo, lo + BAND)
                if n == 0:
                    kw, vw, bias = ref_slice(k_r, 0, BAND), ref_slice(v_r, 0, BAND), bias_first_ref[...]
                else:
                    kw, vw = ref_slice(k_r, lo - BAND, lo + BAND), ref_slice(v_r, lo - BAND, lo + BAND)
                    bias = bias_band_ref[...]
                o, lse = _attn_block(qb, kw, vw, bias, first_head)
                if dil == 1:
                    o_scr[bi, lo:lo + BAND, :] = o
                    l_scr[bi, lo:lo + BAND, :] = lse
                else:
                    dst = pl.ds(r + dil * lo, BAND, stride=dil)
                    o_scr[bi, dst, :] = o
                    l_scr[bi, dst, :] = lse
    lses = [l_scr[i] for i in range(3)]
    m = jnp.maximum(jnp.maximum(lses[0], lses[1]), lses[2])
    es = [jnp.exp(l - m) for l in lses]
    den = es[0] + es[1] + es[2]
    o_ref[...] = (es[0] / den) * o_scr[0] + (es[1] / den) * o_scr[1] + (es[2] / den) * o_scr[2]


def _attention(qkv, batch, seq):
    rel = (np.arange(BAND)[:, None] + BAND) - np.arange(2 * BAND)[None, :]
    band = np.where((rel >= 0) & (rel <= BAND), 0.0, NEG_INF).astype(np.float32)
    bias_band = jnp.asarray(np.concatenate([band, band], axis=0))
    bias_first = jnp.asarray(np.concatenate([band[:, BAND:], band[:, BAND:]], axis=0))
    nat = pl.BlockSpec((seq, LANES), lambda b, p: (b, p))
    cls = lambda dil: pl.BlockSpec((1, dil, seq // dil, LANES), lambda b, p: (b, 0, 0, p))
    full = lambda a: pl.BlockSpec(a.shape, lambda b, p: (0, 0))
    return pl.pallas_call(
        _attn_kernel,
        grid=(batch, B_WIDTH // LANES),
        in_specs=[nat] * 3 + [cls(4)] * 3 + [cls(16)] * 3 + [full(bias_band), full(bias_first)],
        out_specs=pl.BlockSpec((seq, LANES), lambda b, p: (b, p)),
        out_shape=jax.ShapeDtypeStruct((batch * seq, B_WIDTH), F32),
        scratch_shapes=[pltpu.VMEM((3, seq, LANES), F32), pltpu.VMEM((3, seq, LANES), F32)],
        compiler_params=pltpu.CompilerParams(dimension_semantics=("arbitrary", "arbitrary"),
                                             vmem_limit_bytes=VMEM_LIMIT),
        name="dilated_attention",
    )(*qkv, bias_band, bias_first)


def _mix_kernel(a_ref, bm_ref, x_ref, bog_ref, wout_ref, n2g_ref, wr_ref, br_ref, tri_ref,
                h_ref, hn_ref, idx_ref, gate_ref, cnt_ref, cnt_scr):
    tm = x_ref.shape[0]
    bn = _rms(bm_ref[...], bog_ref[...]).astype(BF16)
    mixed = jnp.concatenate([a_ref[...], bn], axis=1)
    h = x_ref[...] + _dot(mixed, wout_ref[...])
    h_ref[...] = h
    hn = _rms(h, n2g_ref[...])
    for j in range(ROW_TILES):
        hn_ref[pl.ds(j, tm, stride=ROW_TILES), :] = hn[:, j * LANES:(j + 1) * LANES]
    logits = _dot(hn.astype(BF16), wr_ref[...]) + br_ref[...]
    lane = lax.broadcasted_iota(I32, (tm, LANES), 1)
    vals, idxs = [], []
    for _ in range(TOP_K):
        m = jnp.max(logits, axis=-1, keepdims=True)
        am = jnp.min(jnp.where(logits == m, lane, LANES), axis=-1, keepdims=True)
        vals.append(m)
        idxs.append(am)
        logits = jnp.where(lane == am, -jnp.inf, logits)
    es = [jnp.exp(v - vals[0]) for v in vals]
    den = es[0] + es[1] + es[2] + es[3]
    @pl.when(pl.program_id(0) == 0)
    def _():
        cnt_scr[...] = jnp.zeros_like(cnt_scr)

    chosen = jnp.zeros((tm, LANES), F32)
    for kk in range(TOP_K):
        chosen = chosen + (lane == idxs[kk]).astype(F32)
    before = _dot(tri_ref[...], chosen.astype(BF16)) + cnt_scr[...]
    cnt_scr[...] = cnt_scr[...] + jnp.sum(chosen, axis=0, keepdims=True)
    cnt_ref[...] = jnp.broadcast_to(cnt_scr[...], cnt_ref.shape)

    idx_out = jnp.zeros((tm, LANES), I32)
    gate_out = jnp.zeros((tm, LANES), F32)
    for kk in range(TOP_K):
        rank = jnp.sum(jnp.where(lane == idxs[kk], before, 0.0), axis=-1, keepdims=True).astype(I32)
        idx_out = jnp.where(lane == kk, idxs[kk], idx_out)
        idx_out = jnp.where(lane == TOP_K + kk, rank, idx_out)
        gate_out = jnp.where(lane == kk, es[kk] / den, gate_out)
    idx_ref[...] = idx_out.T[:SUBLANES, :]
    gate_ref[...] = gate_out


def _mix(a_out, b_mix, x2, bog, w_out, n2g, w_r, b_r):
    t = x2.shape[0]
    tm = TM_PROJ
    full = lambda shape: pl.BlockSpec(shape, lambda i: (0,) * len(shape))
    rows = lambda w: pl.BlockSpec((tm, w), lambda i: (i, 0))
    tri = jnp.asarray(np.tril(np.ones((tm, tm), np.float32), -1), dtype=BF16)
    return pl.pallas_call(
        _mix_kernel,
        grid=(t // tm,),
        in_specs=[rows(A_WIDTH), rows(B_WIDTH), rows(D_MODEL), full((1, B_WIDTH)), full(w_out.shape),
                  full((1, D_MODEL)), full(w_r.shape), full((1, LANES)), full((tm, tm))],
        out_specs=[rows(D_MODEL), pl.BlockSpec((tm * ROW_TILES, LANES), lambda i: (i, 0)),
                   pl.BlockSpec((SUBLANES, tm), lambda i: (0, i)), rows(LANES), full((SUBLANES, LANES))],
        out_shape=[jax.ShapeDtypeStruct((t, D_MODEL), F32), jax.ShapeDtypeStruct((t * ROW_TILES, LANES), F32),
                   jax.ShapeDtypeStruct((SUBLANES, t), I32), jax.ShapeDtypeStruct((t, LANES), F32),
                   jax.ShapeDtypeStruct((SUBLANES, LANES), F32)],
        scratch_shapes=[pltpu.VMEM((1, LANES), F32)],
        compiler_params=pltpu.CompilerParams(dimension_semantics=("arbitrary",), vmem_limit_bytes=VMEM_LIMIT),
        name="mix_router",
    )(a_out, b_mix, x2, bog, w_out, n2g, w_r, b_r, tri)


def _tile_rows(ref, row):
    return ref.at[pl.ds(pl.multiple_of(row * ROW_TILES, SUBLANES), ROW_TILES), :]


def _zero_runs(pad_start_ref, pad_count_ref, nt_ref, xs_hbm, zero_scr, sem, wait):
    tile_rows = TM_MOE * ROW_TILES

    def copy(rows, first_slot):
        c = pltpu.make_async_copy(zero_scr.at[pl.ds(0, rows * ROW_TILES), :],
                                  xs_hbm.at[pl.ds(pl.multiple_of(first_slot * ROW_TILES, SUBLANES), rows * ROW_TILES), :],
                                  sem)
        c.wait() if wait else c.start()

    def expert_body(e, c):
        count = pad_count_ref[e]
        slot = pad_start_ref[e]
        bit = TM_MOE // 2
        while bit:
            @pl.when((count & bit) != 0)
            def _(bit=bit, slot=slot):
                copy(bit, slot)
            slot = slot + (count & bit)
            bit //= 2
        return c
    lax.fori_loop(0, N_EXPERTS, expert_body, 0)

    def idle_body(tile, c):
        copy(TM_MOE, tile * TM_MOE)
        return c
    lax.fori_loop(nt_ref[0], xs_hbm.shape[0] // tile_rows, idle_body, 0)


def _dispatch_kernel(pad_start_ref, pad_count_ref, nt_ref, pos_ref, hn_ref, xs_hbm, zero_scr, sem, zero_sem):
    n_rows = pos_ref.shape[-1] * TOP_K
    i = pl.program_id(0)

    @pl.when(i == 0)
    def _():
        zero_scr[...] = jnp.zeros_like(zero_scr)
        _zero_runs(pad_start_ref, pad_count_ref, nt_ref, xs_hbm, zero_scr, zero_sem, wait=False)

    def body(g, c):
        for u in range(DMA_UNROLL):
            token = g * (DMA_UNROLL // TOP_K) + u // TOP_K
            pltpu.make_async_copy(_tile_rows(hn_ref, token), _tile_rows(xs_hbm, pos_ref[u % TOP_K, token]),
                                  sem).start(priority=u % 2)
        return c
    lax.fori_loop(0, n_rows // DMA_UNROLL, body, 0)

    total = n_rows * ROW_TILES
    pltpu.make_async_copy(xs_hbm.at[pl.ds(0, total), :], xs_hbm.at[pl.ds(0, total), :], sem).wait()

    @pl.when(i == pl.num_programs(0) - 1)
    def _():
        _zero_runs(pad_start_ref, pad_count_ref, nt_ref, xs_hbm, zero_scr, zero_sem, wait=True)


def _dispatch(pos, pad_start, pad_count, n_tiles, hn_tiles, n_slots):
    t = hn_tiles.shape[0] // ROW_TILES
    tm = TM_DISPATCH
    steps = t // tm
    assert t % tm == 0
    grid_spec = pltpu.PrefetchScalarGridSpec(
        num_scalar_prefetch=3,
        grid=(steps,),
        in_specs=[pl.BlockSpec((TOP_K, tm), lambda i, *_: (0, i), memory_space=pltpu.SMEM),
                  pl.BlockSpec((tm * ROW_TILES, LANES), lambda i, *_: (i, 0))],
        out_specs=pl.BlockSpec(memory_space=pl.ANY),
        scratch_shapes=[pltpu.VMEM((TM_MOE * ROW_TILES, LANES), F32), pltpu.SemaphoreType.DMA,
                        pltpu.SemaphoreType.DMA],
    )
    return pl.pallas_call(
        _dispatch_kernel,
        grid_spec=grid_spec,
        out_shape=jax.ShapeDtypeStruct((n_slots * ROW_TILES, LANES), F32),
        compiler_params=pltpu.CompilerParams(dimension_semantics=("arbitrary",), vmem_limit_bytes=VMEM_LIMIT),
        name="dispatch",
    )(pad_start, pad_count, n_tiles, pos, hn_tiles)


def _moe_kernel(te_ref, nt_ref, x_ref, wgu_ref, bgu_ref, wd_ref, bd_ref, y_ref, wgu_bf, wd_bf, act_scr):
    tm = TM_MOE
    i = pl.program_id(0)

    @pl.when(i < nt_ref[0])
    def _():
        @pl.when((i == 0) | (te_ref[i] != te_ref[jnp.maximum(i - 1, 0)]))
        def _():
            wgu_bf[...] = wgu_ref[0].astype(BF16)
            wd_bf[...] = wd_ref[0].astype(BF16)

        x = jnp.concatenate([x_ref[pl.ds(j, tm, stride=ROW_TILES), :] for j in range(ROW_TILES)],
                            axis=1).astype(BF16)
        width = D_FF // MOE_FF_CHUNKS
        for c in range(MOE_FF_CHUNKS):
            lo, hi = c * width, (c + 1) * width
            gate = _dot(x, wgu_bf[:, lo:hi]) + bgu_ref[0, :, lo:hi]
            up = _dot(x, wgu_bf[:, D_FF + lo:D_FF + hi]) + bgu_ref[0, :, D_FF + lo:D_FF + hi]
            gate = jnp.minimum(gate, SWIGLU_LIMIT)
            up = jnp.clip(up, -SWIGLU_LIMIT, SWIGLU_LIMIT)
            act_scr[:, lo:hi] = ((up + 1.0) * (gate * jax.nn.sigmoid(gate * SWIGLU_ALPHA))).astype(BF16)
        y = _dot(act_scr[...], wd_bf[...]) + bd_ref[0]
        for j in range(ROW_TILES):
            y_ref[pl.ds(j, tm, stride=ROW_TILES), :] = y[:, j * LANES:(j + 1) * LANES]

    @pl.when(i >= nt_ref[0])
    def _():
        y_ref[...] = jnp.zeros_like(y_ref)


def _experts(tile_expert, n_tiles, xs, wgu, bgu, wd, bd, nt_max):
    tm = TM_MOE
    tile_in = pl.BlockSpec((tm * ROW_TILES, LANES), lambda i, te, nt: (jnp.minimum(i, nt[0] - 1), 0))
    tile = pl.BlockSpec((tm * ROW_TILES, LANES), lambda i, te, nt: (i, 0))
    grid_spec = pltpu.PrefetchScalarGridSpec(
        num_scalar_prefetch=2,
        grid=(nt_max,),
        in_specs=[tile_in,
                  pl.BlockSpec((1, D_MODEL, 2 * D_FF), lambda i, te, nt: (te[i], 0, 0)),
                  pl.BlockSpec((1, 1, 2 * D_FF), lambda i, te, nt: (te[i], 0, 0)),
                  pl.BlockSpec((1, D_FF, D_MODEL), lambda i, te, nt: (te[i], 0, 0)),
                  pl.BlockSpec((1, 1, D_MODEL), lambda i, te, nt: (te[i], 0, 0))],
        out_specs=tile,
        scratch_shapes=[pltpu.VMEM((D_MODEL, 2 * D_FF), BF16), pltpu.VMEM((D_FF, D_MODEL), BF16),
                        pltpu.VMEM((tm, D_FF), BF16)],
    )
    return pl.pallas_call(
        _moe_kernel,
        grid_spec=grid_spec,
        out_shape=jax.ShapeDtypeStruct((nt_max * tm * ROW_TILES, LANES), F32),
        compiler_params=pltpu.CompilerParams(dimension_semantics=("arbitrary",), vmem_limit_bytes=VMEM_LIMIT),
        name="experts",
    )(tile_expert, n_tiles, xs, wgu, bgu, wd, bd)


def _combine_kernel(pos_ref, pos_next_ref, ys_hbm, gate_ref, h_ref, g_ref, o_ref, buf0, buf1, acc_scr, sem):
    tm = h_ref.shape[0]
    n_rows = tm * TOP_K
    i = pl.program_id(0)
    bufs = (buf0, buf1)

    def gather(idx_ref, buf, s):
        def body(g, c):
            for u in range(DMA_UNROLL):
                f = g * DMA_UNROLL + u
                token = g * (DMA_UNROLL // TOP_K) + u // TOP_K
                pltpu.make_async_copy(_tile_rows(ys_hbm, idx_ref[0, 0, f]),
                                      _tile_rows(buf, (u % TOP_K) * tm + token), sem.at[s]).start(priority=u % 2)
            return c
        lax.fori_loop(0, n_rows // DMA_UNROLL, body, 0)

    @pl.when(i == 0)
    def _():
        gather(pos_ref, buf0, 0)

    for p in (0, 1):
        @pl.when(i % 2 == p)
        def _(p=p):
            @pl.when(i + 1 < pl.num_programs(0))
            def _():
                gather(pos_next_ref, bufs[1 - p], 1 - p)

            buf = bufs[p]
            pltpu.make_async_copy(ys_hbm.at[pl.ds(0, n_rows * ROW_TILES), :], buf, sem.at[p]).wait()
            gates = gate_ref[...]
            gk = [jnp.broadcast_to(gates[:, kk:kk + 1], (tm, LANES)) for kk in range(TOP_K)]
            ss = jnp.zeros((tm, 1), F32)
            for j in range(ROW_TILES):
                cols = slice(j * LANES, (j + 1) * LANES)
                moe = gk[0] * buf[pl.ds(j, tm, stride=ROW_TILES), :]
                for kk in range(1, TOP_K):
                    moe = moe + gk[kk] * buf[pl.ds(kk * tm * ROW_TILES + j, tm, stride=ROW_TILES), :]
                acc = h_ref[:, cols] + moe
                acc_scr[:, cols] = acc
                ss = ss + jnp.sum(acc * acc, axis=-1, keepdims=True)
            inv = lax.rsqrt(ss / D_MODEL + NORM_EPS)
            o_ref[...] = acc_scr[...] * inv * g_ref[...]


def _combine(pos, ys, gates, h, g):
    t = h.shape[0]
    tm = TM_COMB
    steps = t // tm
    pos_blocks = pos.T.reshape(steps, 1, tm * TOP_K)
    smem = lambda f: pl.BlockSpec((1, 1, tm * TOP_K), f, memory_space=pltpu.SMEM)
    row_buf = pltpu.VMEM((TOP_K * tm * ROW_TILES, LANES), F32)
    return pl.pallas_call(
        _combine_kernel,
        grid=(steps,),
        in_specs=[smem(lambda i: (i, 0, 0)),
                  smem(lambda i: (jnp.minimum(i + 1, steps - 1), 0, 0)),
                  pl.BlockSpec(memory_space=pl.ANY),
                  pl.BlockSpec((tm, LANES), lambda i: (i, 0)),
                  pl.BlockSpec((tm, D_MODEL), lambda i: (i, 0)),
                  pl.BlockSpec((1, D_MODEL), lambda i: (0, 0))],
        out_specs=pl.BlockSpec((tm, D_MODEL), lambda i: (i, 0)),
        out_shape=jax.ShapeDtypeStruct((t, D_MODEL), F32),
        scratch_shapes=[row_buf, row_buf, pltpu.VMEM((tm, D_MODEL), F32), pltpu.SemaphoreType.DMA((2,))],
        compiler_params=pltpu.CompilerParams(dimension_semantics=("arbitrary",), vmem_limit_bytes=VMEM_LIMIT),
        name="combine",
    )(pos_blocks, pos_blocks, ys, gates, h, g)


def _routing_plan(idx_rank, counts_f, n_tokens):
    tm = TM_MOE
    nt_max = n_tokens * TOP_K // tm + N_EXPERTS
    experts = jnp.arange(N_EXPERTS, dtype=I32)
    counts = counts_f[0, :N_EXPERTS].astype(I32)
    tiles_e = (counts + tm - 1) // tm
    tile_end = jnp.cumsum(tiles_e)
    first_slot = (tile_end - tiles_e) * tm
    n_tiles = tile_end[-1]
    tile_ids = jnp.arange(nt_max, dtype=I32)
    te = jnp.minimum(jnp.sum((tile_ids[:, None] >= tile_end[None, :]).astype(I32), axis=1), N_EXPERTS - 1)
    last_e = jnp.take(te, jnp.maximum(n_tiles - 1, 0))
    te = jnp.where(tile_ids < n_tiles, te, last_e).astype(I32)
    expert = idx_rank[:TOP_K]
    rank = idx_rank[TOP_K:2 * TOP_K]
    pos = jnp.sum(jnp.where(expert[:, :, None] == experts[None, None, :], first_slot[None, None, :], 0), axis=-1) + rank
    return (te, n_tiles.reshape(1).astype(I32), pos.astype(I32), (first_slot + counts).astype(I32),
            (tiles_e * tm - counts).astype(I32), nt_max)


def kernel(x, norm1_g, w_in, a_ln_g, a_ln_b, a_w_s, a_b_s, a_out_g, b_out_g, w_out, norm2_g, w_router,
           b_router, w_gate_up, b_gate_up, w_down, b_down, normf_g):
    batch, seq, _ = x.shape
    t = batch * seq
    assert seq % (TM_PROJ) == 0 and t % TM_MOE == 0 and seq // DILATIONS[-1] == BAND
    h = x.reshape(t, D_MODEL)

    pos = jnp.arange(seq, dtype=F32)
    inv = ROPE_THETA ** (-jnp.arange(0, HEAD_DIM, 2, dtype=F32) / HEAD_DIM)
    ang = pos[:, None] * inv[None, :]
    cos = jnp.tile(jnp.cos(ang), (1, 2 * LANES // HEAD_DIM))
    sin = jnp.tile(jnp.concatenate([-jnp.sin(ang), jnp.sin(ang)], axis=1), (1, LANES // HEAD_DIM))
    head_of_lane = np.arange(A_WIDTH) // HEAD_DIM
    avg = jnp.asarray((head_of_lane[:, None] == head_of_lane[None, :]).astype(np.float32) / HEAD_DIM, dtype=BF16)
    row2 = lambda v: v.reshape(1, -1).astype(F32)

    for layer in range(norm1_g.shape[0]):
        causal = np.tril(np.ones((CHUNK, CHUNK), dtype=bool))
        ws = jnp.where(causal[None], a_w_s[layer], 0.0).astype(BF16)
        bs = jnp.repeat(a_b_s[layer].astype(F32).T, HEAD_DIM, axis=1)
        a_out, *qkv = _inproj(h, row2(norm1_g[layer]), w_in[layer].astype(BF16), avg,
                              row2(a_ln_g[layer]), row2(a_ln_b[layer]), ws, bs, row2(a_out_g[layer]),
                              cos, sin, seq)
        b_mix = _attention(qkv, batch, seq)
        w_r = jnp.pad(w_router[layer], ((0, 0), (0, LANES - N_EXPERTS))).astype(BF16)
        b_r = jnp.concatenate([b_router[layer].astype(F32), jnp.full((LANES - N_EXPERTS,), NEG_INF, F32)])
        h_mid, hn, idx_rank, gates, counts = _mix(a_out, b_mix, h, row2(b_out_g[layer]), w_out[layer].astype(BF16),
                                                  row2(norm2_g[layer]), w_r, b_r.reshape(1, LANES))
        te, n_tiles, pos, pad_start, pad_count, nt_max = _routing_plan(idx_rank, counts, t)
        xs = _dispatch(pos, pad_start, pad_count, n_tiles, hn, nt_max * TM_MOE)
        ys = _experts(te, n_tiles, xs,
                      w_gate_up[layer], b_gate_up[layer].reshape(N_EXPERTS, 1, 2 * D_FF),
                      w_down[layer], b_down[layer].reshape(N_EXPERTS, 1, D_MODEL), nt_max)
        last = layer == norm1_g.shape[0] - 1
        assert last, "the combine kernel fuses the final norm; depth > 1 is not supported"
        h = _combine(pos, ys, gates, h_mid, row2(normf_g))
    return h.reshape(batch, seq, D_MODEL)
```

```python
import math

import numpy as np
import jax
import jax.numpy as jnp
from jax import lax
from jax.experimental import pallas as pl
from jax.experimental.pallas import tpu as pltpu

F32 = jnp.float32
BF16 = jnp.bfloat16
I32 = jnp.int32

D_MODEL = 1024
HEAD_DIM = 64
A_WIDTH = 512
B_WIDTH = 512
CHUNK = 128
BAND = 128
DILATIONS = (1, 4, 16)
ROPE_THETA = 10000.0
N_EXPERTS = 32
TOP_K = 4
D_FF = 1024
SWIGLU_ALPHA = 1.702
SWIGLU_LIMIT = 7.0
NORM_EPS = 1e-5
NEG_INF = -1e30

LANES = 128
SUBLANES = 8
ROW_TILES = D_MODEL // LANES

TM_PROJ = 512
PROJ_SUBTILES = 1
TM_DISPATCH = 2048
TM_MOE = 512
TM_COMB = 512
DMA_UNROLL = 16
MOE_FF_CHUNKS = 2
VMEM_LIMIT = 56 * 1024 * 1024


def _dot(a, b):
    return jnp.dot(a, b, preferred_element_type=F32)


def _gelu_tanh(x):
    c = math.sqrt(2.0 / math.pi)
    cdf = 0.5 * (1.0 + jnp.tanh(c * (x + 0.044715 * (x * x * x))))
    return x * cdf


def _rms(x, g):
    return x * lax.rsqrt(jnp.mean(x * x, axis=-1, keepdims=True) + NORM_EPS) * g


def _inproj_kernel(x_ref, g1_ref, w_ref, avg_ref, lng_ref, lnb_ref, ws_ref, bs_ref, aog_ref,
                   cos_ref, sin_ref, a_ref, q_ref, k_ref, v_ref, q4_ref, k4_ref, v4_ref,
                   q16_ref, k16_ref, v16_ref, a_scr, qkv_scr, cls_scr):
    tm = x_ref.shape[0]
    n = tm // PROJ_SUBTILES
    nlt = B_WIDTH // LANES
    d4, d16 = DILATIONS[1], DILATIONS[2]
    first_head = lax.broadcasted_iota(I32, (CHUNK, LANES), 1) < HEAD_DIM
    first_half = (lax.broadcasted_iota(I32, (n, LANES), 1) % HEAD_DIM) < (HEAD_DIM // 2)
    avg = avg_ref[...]

    for sub in range(PROJ_SUBTILES):
        r0 = sub * n
        rows_n = slice(r0, r0 + n)
        xn = _rms(x_ref[rows_n, :], g1_ref[...]).astype(BF16)

        ug = _gelu_tanh(_dot(xn, w_ref[:, 0:A_WIDTH]))
        vg = _gelu_tanh(_dot(xn, w_ref[:, A_WIDTH:2 * A_WIDTH]))
        mu = _dot(vg.astype(BF16), avg)
        d = vg - mu
        var = _dot((d * d).astype(BF16), avg)
        vn = (d * lax.rsqrt(var + NORM_EPS) * lng_ref[...] + lnb_ref[...]).astype(BF16)
        for c in range(n // CHUNK):
            rows = slice(c * CHUNK, (c + 1) * CHUNK)
            for p in range(A_WIDTH // LANES):
                cols = slice(p * LANES, (p + 1) * LANES)
                slab = vn[rows, cols]
                g = jnp.where(first_head, _dot(ws_ref[2 * p], slab), _dot(ws_ref[2 * p + 1], slab))
                a_scr[r0 + c * CHUNK:r0 + (c + 1) * CHUNK, cols] = ug[rows, cols] * (g + bs_ref[:, cols])
        a_ref[rows_n, :] = _rms(a_scr[rows_n, :], aog_ref[...]).astype(BF16)

        cos = cos_ref[rows_n, :]
        sin = sin_ref[rows_n, :]

        def rope(t):
            rot = jnp.where(first_half, pltpu.roll(t, LANES - HEAD_DIM // 2, 1), pltpu.roll(t, HEAD_DIM // 2, 1))
            return t * cos + rot * sin

        off = 2 * A_WIDTH
        for p in range(nlt):
            q = _dot(xn, w_ref[:, off + p * LANES: off + (p + 1) * LANES])
            qkv_scr[p, rows_n, :] = rope(q) * (HEAD_DIM ** -0.5)
            k = _dot(xn, w_ref[:, off + B_WIDTH + p * LANES: off + B_WIDTH + (p + 1) * LANES])
            qkv_scr[nlt + p, rows_n, :] = rope(k)
            qkv_scr[2 * nlt + p, rows_n, :] = _dot(xn, w_ref[:, off + 2 * B_WIDTH + p * LANES:
                                                             off + 2 * B_WIDTH + (p + 1) * LANES])

        rows4 = slice(r0 // d4, (r0 + n) // d4)
        rows16 = slice(r0 // d16, (r0 + n) // d16)
        for i, (nat_ref, c4_ref, c16_ref) in enumerate(((q_ref, q4_ref, q16_ref), (k_ref, k4_ref, k16_ref),
                                                        (v_ref, v4_ref, v16_ref))):
            for p in range(nlt):
                cols = slice(p * LANES, (p + 1) * LANES)
                nat_ref[rows_n, cols] = qkv_scr[i * nlt + p, rows_n, :].astype(BF16)
                for r4 in range(d4):
                    cls = qkv_scr[i * nlt + p, pl.ds(r0 + r4, n // d4, stride=d4), :]
                    c4_ref[0, r4, rows4, cols] = cls.astype(BF16)
                    cls_scr[i * nlt + p, r4, rows4, :] = cls
                for r4 in range(d4):
                    for a in range(d16 // d4):
                        c16_ref[0, r4 + d4 * a, rows16, cols] = cls_scr[
                            i * nlt + p, r4, pl.ds(r0 // d4 + a, n // d16, stride=d16 // d4), :].astype(BF16)


def _inproj(x2, g1, w_in, avg, lng, lnb, ws, bs, aog, cos, sin, seq):
    t = x2.shape[0]
    tm = TM_PROJ
    nseq = seq // tm
    full = lambda shape: pl.BlockSpec(shape, lambda i: (0,) * len(shape))
    rows = lambda w: pl.BlockSpec((tm, w), lambda i: (i, 0))
    classes = lambda dil: pl.BlockSpec((1, dil, tm // dil, B_WIDTH), lambda i: (i // nseq, 0, i % nseq, 0))
    class_shape = lambda dil: jax.ShapeDtypeStruct((t // seq, dil, seq // dil, B_WIDTH), BF16)
    return pl.pallas_call(
        _inproj_kernel,
        grid=(t // tm,),
        in_specs=[rows(D_MODEL), full((1, D_MODEL)), full(w_in.shape), full(avg.shape),
                  full((1, A_WIDTH)), full((1, A_WIDTH)), full(ws.shape), full(bs.shape),
                  full((1, A_WIDTH)),
                  pl.BlockSpec((tm, LANES), lambda i: (i % nseq, 0)),
                  pl.BlockSpec((tm, LANES), lambda i: (i % nseq, 0))],
        out_specs=[rows(A_WIDTH)] + [rows(B_WIDTH)] * 3 + [classes(DILATIONS[1])] * 3 + [classes(DILATIONS[2])] * 3,
        out_shape=([jax.ShapeDtypeStruct((t, A_WIDTH), BF16)] + [jax.ShapeDtypeStruct((t, B_WIDTH), BF16)] * 3
                   + [class_shape(DILATIONS[1])] * 3 + [class_shape(DILATIONS[2])] * 3),
        scratch_shapes=[pltpu.VMEM((tm, A_WIDTH), F32), pltpu.VMEM((3 * B_WIDTH // LANES, tm, LANES), F32),
                        pltpu.VMEM((3 * B_WIDTH // LANES, DILATIONS[1], tm // DILATIONS[1], LANES), F32)],
        compiler_params=pltpu.CompilerParams(dimension_semantics=("arbitrary",), vmem_limit_bytes=VMEM_LIMIT),
        name="inproj",
    )(x2, g1, w_in, avg, lng, lnb, ws, bs, aog, cos, sin)


def _attn_block(qb, kw, vw, bias, first_head):
    zero = jnp.zeros_like(qb)
    q2 = jnp.concatenate([jnp.where(first_head, qb, zero), jnp.where(first_head, zero, qb)], axis=0)
    s = lax.dot_general(q2, kw, (((1,), (1,)), ((), ())), preferred_element_type=F32) + bias
    m = jnp.max(s, axis=-1, keepdims=True)
    p = jnp.exp(s - m)
    l = jnp.sum(p, axis=-1, keepdims=True)
    o = _dot(p.astype(BF16), vw) / l
    lse = jnp.broadcast_to(m + jnp.log(l), o.shape)
    return (jnp.where(first_head, o[:BAND], o[BAND:]),
            jnp.where(first_head, lse[:BAND], lse[BAND:]))


def _attn_kernel(q1, k1, v1, q4, k4, v4, q16, k16, v16, bias_band_ref, bias_first_ref, o_ref, o_scr, l_scr):
    seq = q1.shape[0]
    first_head = lax.broadcasted_iota(I32, (BAND, LANES), 1) < HEAD_DIM
    branches = ((q1, k1, v1), (q4, k4, v4), (q16, k16, v16))
    for bi, dil in enumerate(DILATIONS):
        q_r, k_r, v_r = branches[bi]
        length = seq // dil
        for r in range(dil):
            for n in range(length // BAND):
                lo = n * BAND
                if bi == 0:
                    ref_slice = lambda ref, a, b: ref[a:b, :]
                else:
                    ref_slice = lambda ref, a, b, r=r: ref[0, r, a:b, :]
                qb = ref_slice(q_r, lo, lo + BAND)
                if n == 0:
                    kw, vw, bias = ref_slice(k_r, 0, BAND), ref_slice(v_r, 0, BAND), bias_first_ref[...]
                else:
                    kw, vw = ref_slice(k_r, lo - BAND, lo + BAND), ref_slice(v_r, lo - BAND, lo + BAND)
                    bias = bias_band_ref[...]
                o, lse = _attn_block(qb, kw, vw, bias, first_head)
                if dil == 1:
                    o_scr[bi, lo:lo + BAND, :] = o
                    l_scr[bi, lo:lo + BAND, :] = lse
                else:
                    dst = pl.ds(r + dil * lo, BAND, stride=dil)
                    o_scr[bi, dst, :] = o
                    l_scr[bi, dst, :] = lse
    lses = [l_scr[i] for i in range(3)]
    m = jnp.maximum(jnp.maximum(lses[0], lses[1]), lses[2])
    es = [jnp.exp(l - m) for l in lses]
    den = es[0] + es[1] + es[2]
    o_ref[...] = (es[0] / den) * o_scr[0] + (es[1] / den) * o_scr[1] + (es[2] / den) * o_scr[2]


def _attention(qkv, batch, seq):
    rel = (np.arange(BAND)[:, None] + BAND) - np.arange(2 * BAND)[None, :]
    band = np.where((rel >= 0) & (rel <= BAND), 0.0, NEG_INF).astype(np.float32)
    bias_band = jnp.asarray(np.concatenate([band, band], axis=0))
    bias_first = jnp.asarray(np.concatenate([band[:, BAND:], band[:, BAND:]], axis=0))
    nat = pl.BlockSpec((seq, LANES), lambda b, p: (b, p))
    cls = lambda dil: pl.BlockSpec((1, dil, seq // dil, LANES), lambda b, p: (b, 0, 0, p))
    full = lambda a: pl.BlockSpec(a.shape, lambda b, p: (0, 0))
    return pl.pallas_call(
        _attn_kernel,
        grid=(batch, B_WIDTH // LANES),
        in_specs=[nat] * 3 + [cls(4)] * 3 + [cls(16)] * 3 + [full(bias_band), full(bias_first)],
        out_specs=pl.BlockSpec((seq, LANES), lambda b, p: (b, p)),
        out_shape=jax.ShapeDtypeStruct((batch * seq, B_WIDTH), F32),
        scratch_shapes=[pltpu.VMEM((3, seq, LANES), F32), pltpu.VMEM((3, seq, LANES), F32)],
        compiler_params=pltpu.CompilerParams(dimension_semantics=("arbitrary", "arbitrary"),
                                             vmem_limit_bytes=VMEM_LIMIT),
        name="dilated_attention",
    )(*qkv, bias_band, bias_first)


def _mix_kernel(a_ref, bm_ref, x_ref, bog_ref, wout_ref, n2g_ref, wr_ref, br_ref, tri_ref,
                h_ref, hn_ref, idx_ref, gate_ref, cnt_ref, cnt_scr):
    tm = x_ref.shape[0]
    bn = _rms(bm_ref[...], bog_ref[...]).astype(BF16)
    mixed = jnp.concatenate([a_ref[...], bn], axis=1)
    h = x_ref[...] + _dot(mixed, wout_ref[...])
    h_ref[...] = h
    hn = _rms(h, n2g_ref[...])
    for j in range(ROW_TILES):
        hn_ref[pl.ds(j, tm, stride=ROW_TILES), :] = hn[:, j * LANES:(j + 1) * LANES]
    logits = _dot(hn.astype(BF16), wr_ref[...]) + br_ref[...]
    lane = lax.broadcasted_iota(I32, (tm, LANES), 1)
    vals, idxs = [], []
    for _ in range(TOP_K):
        m = jnp.max(logits, axis=-1, keepdims=True)
        am = jnp.min(jnp.where(logits == m, lane, LANES), axis=-1, keepdims=True)
        vals.append(m)
        idxs.append(am)
        logits = jnp.where(lane == am, -jnp.inf, logits)
    es = [jnp.exp(v - vals[0]) for v in vals]
    den = es[0] + es[1] + es[2] + es[3]
    @pl.when(pl.program_id(0) == 0)
    def _():
        cnt_scr[...] = jnp.zeros_like(cnt_scr)

    chosen = jnp.zeros((tm, LANES), F32)
    for kk in range(TOP_K):
        chosen = chosen + (lane == idxs[kk]).astype(F32)
    before = _dot(tri_ref[...], chosen.astype(BF16)) + cnt_scr[...]
    cnt_scr[...] = cnt_scr[...] + jnp.sum(chosen, axis=0, keepdims=True)
    cnt_ref[...] = jnp.broadcast_to(cnt_scr[...], cnt_ref.shape)

    idx_out = jnp.zeros((tm, LANES), I32)
    gate_out = jnp.zeros((tm, LANES), F32)
    for kk in range(TOP_K):
        rank = jnp.sum(jnp.where(lane == idxs[kk], before, 0.0), axis=-1, keepdims=True).astype(I32)
        idx_out = jnp.where(lane == kk, idxs[kk], idx_out)
        idx_out = jnp.where(lane == TOP_K + kk, rank, idx_out)
        gate_out = jnp.where(lane == kk, es[kk] / den, gate_out)
    idx_ref[...] = idx_out.T[:SUBLANES, :]
    gate_ref[...] = gate_out


def _mix(a_out, b_mix, x2, bog, w_out, n2g, w_r, b_r):
    t = x2.shape[0]
    tm = TM_PROJ
    full = lambda shape: pl.BlockSpec(shape, lambda i: (0,) * len(shape))
    rows = lambda w: pl.BlockSpec((tm, w), lambda i: (i, 0))
    tri = jnp.asarray(np.tril(np.ones((tm, tm), np.float32), -1), dtype=BF16)
    return pl.pallas_call(
        _mix_kernel,
        grid=(t // tm,),
        in_specs=[rows(A_WIDTH), rows(B_WIDTH), rows(D_MODEL), full((1, B_WIDTH)), full(w_out.shape),
                  full((1, D_MODEL)), full(w_r.shape), full((1, LANES)), full((tm, tm))],
        out_specs=[rows(D_MODEL), pl.BlockSpec((tm * ROW_TILES, LANES), lambda i: (i, 0)),
                   pl.BlockSpec((SUBLANES, tm), lambda i: (0, i)), rows(LANES), full((SUBLANES, LANES))],
        out_shape=[jax.ShapeDtypeStruct((t, D_MODEL), F32), jax.ShapeDtypeStruct((t * ROW_TILES, LANES), F32),
                   jax.ShapeDtypeStruct((SUBLANES, t), I32), jax.ShapeDtypeStruct((t, LANES), F32),
                   jax.ShapeDtypeStruct((SUBLANES, LANES), F32)],
        scratch_shapes=[pltpu.VMEM((1, LANES), F32)],
        compiler_params=pltpu.CompilerParams(dimension_semantics=("arbitrary",), vmem_limit_bytes=VMEM_LIMIT),
        name="mix_router",
    )(a_out, b_mix, x2, bog, w_out, n2g, w_r, b_r, tri)


def _tile_rows(ref, row):
    return ref.at[pl.ds(pl.multiple_of(row * ROW_TILES, SUBLANES), ROW_TILES), :]


def _zero_runs(pad_start_ref, pad_count_ref, nt_ref, xs_hbm, zero_scr, sem, wait):
    tile_rows = TM_MOE * ROW_TILES

    def copy(rows, first_slot):
        c = pltpu.make_async_copy(zero_scr.at[pl.ds(0, rows * ROW_TILES), :],
                                  xs_hbm.at[pl.ds(pl.multiple_of(first_slot * ROW_TILES, SUBLANES), rows * ROW_TILES), :],
                                  sem)
        c.wait() if wait else c.start()

    def expert_body(e, c):
        count = pad_count_ref[e]
        slot = pad_start_ref[e]
        bit = TM_MOE // 2
        while bit:
            @pl.when((count & bit) != 0)
            def _(bit=bit, slot=slot):
                copy(bit, slot)
            slot = slot + (count & bit)
            bit //= 2
        return c
    lax.fori_loop(0, N_EXPERTS, expert_body, 0)

    def idle_body(tile, c):
        copy(TM_MOE, tile * TM_MOE)
        return c
    lax.fori_loop(nt_ref[0], xs_hbm.shape[0] // tile_rows, idle_body, 0)


def _dispatch_kernel(pad_start_ref, pad_count_ref, nt_ref, pos_ref, hn_ref, xs_hbm, zero_scr, sem, zero_sem):
    n_rows = pos_ref.shape[-1] * TOP_K
    i = pl.program_id(0)

    @pl.when(i == 0)
    def _():
        zero_scr[...] = jnp.zeros_like(zero_scr)
        _zero_runs(pad_start_ref, pad_count_ref, nt_ref, xs_hbm, zero_scr, zero_sem, wait=False)

    def body(g, c):
        for u in range(DMA_UNROLL):
            token = g * (DMA_UNROLL // TOP_K) + u // TOP_K
            pltpu.make_async_copy(_tile_rows(hn_ref, token), _tile_rows(xs_hbm, pos_ref[u % TOP_K, token]),
                                  sem).start(priority=u % 2)
        return c
    lax.fori_loop(0, n_rows // DMA_UNROLL, body, 0)

    total = n_rows * ROW_TILES
    pltpu.make_async_copy(xs_hbm.at[pl.ds(0, total), :], xs_hbm.at[pl.ds(0, total), :], sem).wait()

    @pl.when(i == pl.num_programs(0) - 1)
    def _():
        _zero_runs(pad_start_ref, pad_count_ref, nt_ref, xs_hbm, zero_scr, zero_sem, wait=True)


def _dispatch(pos, pad_start, pad_count, n_tiles, hn_tiles, n_slots):
    t = hn_tiles.shape[0] // ROW_TILES
    tm = TM_DISPATCH
    steps = t // tm
    assert t % tm == 0
    grid_spec = pltpu.PrefetchScalarGridSpec(
        num_scalar_prefetch=3,
        grid=(steps,),
        in_specs=[pl.BlockSpec((TOP_K, tm), lambda i, *_: (0, i), memory_space=pltpu.SMEM),
                  pl.BlockSpec((tm * ROW_TILES, LANES), lambda i, *_: (i, 0))],
        out_specs=pl.BlockSpec(memory_space=pl.ANY),
        scratch_shapes=[pltpu.VMEM((TM_MOE * ROW_TILES, LANES), F32), pltpu.SemaphoreType.DMA,
                        pltpu.SemaphoreType.DMA],
    )
    return pl.pallas_call(
        _dispatch_kernel,
        grid_spec=grid_spec,
        out_shape=jax.ShapeDtypeStruct((n_slots * ROW_TILES, LANES), F32),
        compiler_params=pltpu.CompilerParams(dimension_semantics=("arbitrary",), vmem_limit_bytes=VMEM_LIMIT),
        name="dispatch",
    )(pad_start, pad_count, n_tiles, pos, hn_tiles)


def _moe_kernel(te_ref, nt_ref, x_ref, wgu_ref, bgu_ref, wd_ref, bd_ref, y_ref, wgu_bf, wd_bf, act_scr):
    tm = TM_MOE
    i = pl.program_id(0)

    @pl.when(i < nt_ref[0])
    def _():
        @pl.when((i == 0) | (te_ref[i] != te_ref[jnp.maximum(i - 1, 0)]))
        def _():
            wgu_bf[...] = wgu_ref[0].astype(BF16)
            wd_bf[...] = wd_ref[0].astype(BF16)

        x = jnp.concatenate([x_ref[pl.ds(j, tm, stride=ROW_TILES), :] for j in range(ROW_TILES)],
                            axis=1).astype(BF16)
        width = D_FF // MOE_FF_CHUNKS
        for c in range(MOE_FF_CHUNKS):
            lo, hi = c * width, (c + 1) * width
            gate = _dot(x, wgu_bf[:, lo:hi]) + bgu_ref[0, :, lo:hi]
            up = _dot(x, wgu_bf[:, D_FF + lo:D_FF + hi]) + bgu_ref[0, :, D_FF + lo:D_FF + hi]
            gate = jnp.minimum(gate, SWIGLU_LIMIT)
            up = jnp.clip(up, -SWIGLU_LIMIT, SWIGLU_LIMIT)
            act_scr[:, lo:hi] = ((up + 1.0) * (gate * jax.nn.sigmoid(gate * SWIGLU_ALPHA))).astype(BF16)
        y = _dot(act_scr[...], wd_bf[...]) + bd_ref[0]
        for j in range(ROW_TILES):
            y_ref[pl.ds(j, tm, stride=ROW_TILES), :] = y[:, j * LANES:(j + 1) * LANES]

    @pl.when(i >= nt_ref[0])
    def _():
        y_ref[...] = jnp.zeros_like(y_ref)


def _experts(tile_expert, n_tiles, xs, wgu, bgu, wd, bd, nt_max):
    tm = TM_MOE
    tile_in = pl.BlockSpec((tm * ROW_TILES, LANES), lambda i, te, nt: (jnp.minimum(i, nt[0] - 1), 0))
    tile = pl.BlockSpec((tm * ROW_TILES, LANES), lambda i, te, nt: (i, 0))
    grid_spec = pltpu.PrefetchScalarGridSpec(
        num_scalar_prefetch=2,
        grid=(nt_max,),
        in_specs=[tile_in,
                  pl.BlockSpec((1, D_MODEL, 2 * D_FF), lambda i, te, nt: (te[i], 0, 0)),
                  pl.BlockSpec((1, 1, 2 * D_FF), lambda i, te, nt: (te[i], 0, 0)),
                  pl.BlockSpec((1, D_FF, D_MODEL), lambda i, te, nt: (te[i], 0, 0)),
                  pl.BlockSpec((1, 1, D_MODEL), lambda i, te, nt: (te[i], 0, 0))],
        out_specs=tile,
        scratch_shapes=[pltpu.VMEM((D_MODEL, 2 * D_FF), BF16), pltpu.VMEM((D_FF, D_MODEL), BF16),
                        pltpu.VMEM((tm, D_FF), BF16)],
    )
    return pl.pallas_call(
        _moe_kernel,
        grid_spec=grid_spec,
        out_shape=jax.ShapeDtypeStruct((nt_max * tm * ROW_TILES, LANES), F32),
        compiler_params=pltpu.CompilerParams(dimension_semantics=("arbitrary",), vmem_limit_bytes=VMEM_LIMIT),
        name="experts",
    )(tile_expert, n_tiles, xs, wgu, bgu, wd, bd)


def _combine_kernel(*refs):
    pos_refs, pos_next_refs = refs[:TOP_K], refs[TOP_K:2 * TOP_K]
    ys_hbm, gate_ref, h_ref, g_ref, o_ref, buf0, buf1, acc_scr, sem = refs[2 * TOP_K:]
    tm = h_ref.shape[0]
    n_rows = tm * TOP_K
    i = pl.program_id(0)
    bufs = (buf0, buf1)

    def gather(idx_refs, buf, s):
        for kk in range(TOP_K):
            def body(g, c, kk=kk):
                for u in range(DMA_UNROLL):
                    token = g * DMA_UNROLL + u
                    pltpu.make_async_copy(_tile_rows(ys_hbm, idx_refs[kk][0, 0, token]),
                                          _tile_rows(buf, kk * tm + token), sem.at[s]).start(priority=u % 2)
                return c
            lax.fori_loop(0, tm // DMA_UNROLL, body, 0)

    @pl.when(i == 0)
    def _():
        gather(pos_refs, buf0, 0)

    for p in (0, 1):
        @pl.when(i % 2 == p)
        def _(p=p):
            @pl.when(i + 1 < pl.num_programs(0))
            def _():
                gather(pos_next_refs, bufs[1 - p], 1 - p)

            buf = bufs[p]
            pltpu.make_async_copy(ys_hbm.at[pl.ds(0, n_rows * ROW_TILES), :], buf, sem.at[p]).wait()
            gates = gate_ref[...]
            gk = [jnp.broadcast_to(gates[:, kk:kk + 1], (tm, LANES)) for kk in range(TOP_K)]
            ss = jnp.zeros((tm, 1), F32)
            for j in range(ROW_TILES):
                cols = slice(j * LANES, (j + 1) * LANES)
                moe = gk[0] * buf[pl.ds(j, tm, stride=ROW_TILES), :]
                for kk in range(1, TOP_K):
                    moe = moe + gk[kk] * buf[pl.ds(kk * tm * ROW_TILES + j, tm, stride=ROW_TILES), :]
                acc = h_ref[:, cols] + moe
                acc_scr[:, cols] = acc
                ss = ss + jnp.sum(acc * acc, axis=-1, keepdims=True)
            inv = lax.rsqrt(ss / D_MODEL + NORM_EPS)
            o_ref[...] = acc_scr[...] * inv * g_ref[...]


def _combine(pos, ys, gates, h, g):
    t = h.shape[0]
    tm = TM_COMB
    steps = t // tm
    pos_k = [pos[kk].reshape(steps, 1, tm) for kk in range(TOP_K)]
    smem = lambda f: pl.BlockSpec((1, 1, tm), f, memory_space=pltpu.SMEM)
    row_buf = pltpu.VMEM((TOP_K * tm * ROW_TILES, LANES), F32)
    return pl.pallas_call(
        _combine_kernel,
        grid=(steps,),
        in_specs=[smem(lambda i: (i, 0, 0))] * TOP_K
                 + [smem(lambda i: (jnp.minimum(i + 1, steps - 1), 0, 0))] * TOP_K
                 + [pl.BlockSpec(memory_space=pl.ANY),
                  pl.BlockSpec((tm, LANES), lambda i: (i, 0)),
                  pl.BlockSpec((tm, D_MODEL), lambda i: (i, 0)),
                  pl.BlockSpec((1, D_MODEL), lambda i: (0, 0))],
        out_specs=pl.BlockSpec((tm, D_MODEL), lambda i: (i, 0)),
        out_shape=jax.ShapeDtypeStruct((t, D_MODEL), F32),
        scratch_shapes=[row_buf, row_buf, pltpu.VMEM((tm, D_MODEL), F32), pltpu.SemaphoreType.DMA((2,))],
        compiler_params=pltpu.CompilerParams(dimension_semantics=("arbitrary",), vmem_limit_bytes=VMEM_LIMIT),
        name="combine",
    )(*pos_k, *pos_k, ys, gates, h, g)


def _routing_plan(idx_rank, counts_f, n_tokens):
    tm = TM_MOE
    nt_max = n_tokens * TOP_K // tm + N_EXPERTS
    experts = jnp.arange(N_EXPERTS, dtype=I32)
    counts = counts_f[0, :N_EXPERTS].astype(I32)
    tiles_e = (counts + tm - 1) // tm
    tile_end = jnp.cumsum(tiles_e)
    first_slot = (tile_end - tiles_e) * tm
    n_tiles = tile_end[-1]
    tile_ids = jnp.arange(nt_max, dtype=I32)
    te = jnp.minimum(jnp.sum((tile_ids[:, None] >= tile_end[None, :]).astype(I32), axis=1), N_EXPERTS - 1)
    last_e = jnp.take(te, jnp.maximum(n_tiles - 1, 0))
    te = jnp.where(tile_ids < n_tiles, te, last_e).astype(I32)
    expert = idx_rank[:TOP_K]
    rank = idx_rank[TOP_K:2 * TOP_K]
    pos = jnp.sum(jnp.where(expert[:, :, None] == experts[None, None, :], first_slot[None, None, :], 0), axis=-1) + rank
    return (te, n_tiles.reshape(1).astype(I32), pos.astype(I32), (first_slot + counts).astype(I32),
            (tiles_e * tm - counts).astype(I32), nt_max)


def kernel(x, norm1_g, w_in, a_ln_g, a_ln_b, a_w_s, a_b_s, a_out_g, b_out_g, w_out, norm2_g, w_router,
           b_router, w_gate_up, b_gate_up, w_down, b_down, normf_g):
    batch, seq, _ = x.shape
    t = batch * seq
    assert seq % (TM_PROJ) == 0 and t % TM_MOE == 0 and seq // DILATIONS[-1] == BAND
    h = x.reshape(t, D_MODEL)

    pos = jnp.arange(seq, dtype=F32)
    inv = ROPE_THETA ** (-jnp.arange(0, HEAD_DIM, 2, dtype=F32) / HEAD_DIM)
    ang = pos[:, None] * inv[None, :]
    cos = jnp.tile(jnp.cos(ang), (1, 2 * LANES // HEAD_DIM))
    sin = jnp.tile(jnp.concatenate([-jnp.sin(ang), jnp.sin(ang)], axis=1), (1, LANES // HEAD_DIM))
    head_of_lane = np.arange(A_WIDTH) // HEAD_DIM
    avg = jnp.asarray((head_of_lane[:, None] == head_of_lane[None, :]).astype(np.float32) / HEAD_DIM, dtype=BF16)
    row2 = lambda v: v.reshape(1, -1).astype(F32)

    for layer in range(norm1_g.shape[0]):
        causal = np.tril(np.ones((CHUNK, CHUNK), dtype=bool))
        ws = jnp.where(causal[None], a_w_s[layer], 0.0).astype(BF16)
        bs = jnp.repeat(a_b_s[layer].astype(F32).T, HEAD_DIM, axis=1)
        a_out, *qkv = _inproj(h, row2(norm1_g[layer]), w_in[layer].astype(BF16), avg,
                              row2(a_ln_g[layer]), row2(a_ln_b[layer]), ws, bs, row2(a_out_g[layer]),
                              cos, sin, seq)
        b_mix = _attention(qkv, batch, seq)
        w_r = jnp.pad(w_router[layer], ((0, 0), (0, LANES - N_EXPERTS))).astype(BF16)
        b_r = jnp.concatenate([b_router[layer].astype(F32), jnp.full((LANES - N_EXPERTS,), NEG_INF, F32)])
        h_mid, hn, idx_rank, gates, counts = _mix(a_out, b_mix, h, row2(b_out_g[layer]), w_out[layer].astype(BF16),
                                                  row2(norm2_g[layer]), w_r, b_r.reshape(1, LANES))
        te, n_tiles, pos, pad_start, pad_count, nt_max = _routing_plan(idx_rank, counts, t)
        xs = _dispatch(pos, pad_start, pad_count, n_tiles, hn, nt_max * TM_MOE)
        ys = _experts(te, n_tiles, xs,
                      w_gate_up[layer], b_gate_up[layer].reshape(N_EXPERTS, 1, 2 * D_FF),
                      w_down[layer], b_down[layer].reshape(N_EXPERTS, 1, D_MODEL), nt_max)
        last = layer == norm1_g.shape[0] - 1
        assert last, "the combine kernel fuses the final norm; depth > 1 is not supported"
        h = _combine(pos, ys, gates, h_mid, row2(normf_g))
    return h.reshape(batch, seq, D_MODEL)
```

```python
import math

import numpy as np
import jax
import jax.numpy as jnp
from jax import lax
from jax.experimental import pallas as pl
from jax.experimental.pallas import tpu as pltpu

F32 = jnp.float32
BF16 = jnp.bfloat16
I32 = jnp.int32

D_MODEL = 1024
HEAD_DIM = 64
A_WIDTH = 512
B_WIDTH = 512
CHUNK = 128
BAND = 128
DILATIONS = (1, 4, 16)
ROPE_THETA = 10000.0
N_EXPERTS = 32
TOP_K = 4
D_FF = 1024
SWIGLU_ALPHA = 1.702
SWIGLU_LIMIT = 7.0
NORM_EPS = 1e-5
NEG_INF = -1e30

LANES = 128
SUBLANES = 8
ROW_TILES = D_MODEL // LANES

TM_PROJ = 512
PROJ_SUBTILES = 1
TM_MIX = 1024
TM_DISPATCH = 2048
TM_MOE = 512
TM_COMB = 512
DMA_UNROLL = 16
MOE_FF_CHUNKS = 2
VMEM_LIMIT = 56 * 1024 * 1024


def _dot(a, b):
    return jnp.dot(a, b, preferred_element_type=F32)


def _gelu_tanh(x):
    c = math.sqrt(2.0 / math.pi)
    cdf = 0.5 * (1.0 + jnp.tanh(c * (x + 0.044715 * (x * x * x))))
    return x * cdf


def _rms(x, g):
    return x * lax.rsqrt(jnp.mean(x * x, axis=-1, keepdims=True) + NORM_EPS) * g


def _inproj_kernel(x_ref, g1_ref, w_ref, avg_ref, lng_ref, lnb_ref, ws_ref, bs_ref, aog_ref,
                   cos_ref, sin_ref, a_ref, q_ref, k_ref, v_ref, q4_ref, k4_ref, v4_ref,
                   q16_ref, k16_ref, v16_ref, a_scr, qkv_scr, cls_scr):
    tm = x_ref.shape[0]
    n = tm // PROJ_SUBTILES
    nlt = B_WIDTH // LANES
    d4, d16 = DILATIONS[1], DILATIONS[2]
    first_head = lax.broadcasted_iota(I32, (CHUNK, LANES), 1) < HEAD_DIM
    first_half = (lax.broadcasted_iota(I32, (n, LANES), 1) % HEAD_DIM) < (HEAD_DIM // 2)
    avg = avg_ref[...]

    for sub in range(PROJ_SUBTILES):
        r0 = sub * n
        rows_n = slice(r0, r0 + n)
        xn = _rms(x_ref[rows_n, :], g1_ref[...]).astype(BF16)

        ug = _gelu_tanh(_dot(xn, w_ref[:, 0:A_WIDTH]))
        vg = _gelu_tanh(_dot(xn, w_ref[:, A_WIDTH:2 * A_WIDTH]))
        mu = _dot(vg.astype(BF16), avg)
        d = vg - mu
        var = _dot((d * d).astype(BF16), avg)
        vn = (d * lax.rsqrt(var + NORM_EPS) * lng_ref[...] + lnb_ref[...]).astype(BF16)
        for c in range(n // CHUNK):
            rows = slice(c * CHUNK, (c + 1) * CHUNK)
            for p in range(A_WIDTH // LANES):
                cols = slice(p * LANES, (p + 1) * LANES)
                slab = vn[rows, cols]
                g = jnp.where(first_head, _dot(ws_ref[2 * p], slab), _dot(ws_ref[2 * p + 1], slab))
                a_scr[r0 + c * CHUNK:r0 + (c + 1) * CHUNK, cols] = ug[rows, cols] * (g + bs_ref[:, cols])
        a_ref[rows_n, :] = _rms(a_scr[rows_n, :], aog_ref[...]).astype(BF16)

        cos = cos_ref[rows_n, :]
        sin = sin_ref[rows_n, :]

        def rope(t):
            rot = jnp.where(first_half, pltpu.roll(t, LANES - HEAD_DIM // 2, 1), pltpu.roll(t, HEAD_DIM // 2, 1))
            return t * cos + rot * sin

        off = 2 * A_WIDTH
        for p in range(nlt):
            q = _dot(xn, w_ref[:, off + p * LANES: off + (p + 1) * LANES])
            qkv_scr[p, rows_n, :] = rope(q) * (HEAD_DIM ** -0.5)
            k = _dot(xn, w_ref[:, off + B_WIDTH + p * LANES: off + B_WIDTH + (p + 1) * LANES])
            qkv_scr[nlt + p, rows_n, :] = rope(k)
            qkv_scr[2 * nlt + p, rows_n, :] = _dot(xn, w_ref[:, off + 2 * B_WIDTH + p * LANES:
                                                             off + 2 * B_WIDTH + (p + 1) * LANES])

        rows4 = slice(r0 // d4, (r0 + n) // d4)
        rows16 = slice(r0 // d16, (r0 + n) // d16)
        for i, (nat_ref, c4_ref, c16_ref) in enumerate(((q_ref, q4_ref, q16_ref), (k_ref, k4_ref, k16_ref),
                                                        (v_ref, v4_ref, v16_ref))):
            for p in range(nlt):
                cols = slice(p * LANES, (p + 1) * LANES)
                nat_ref[rows_n, cols] = qkv_scr[i * nlt + p, rows_n, :].astype(BF16)
                for r4 in range(d4):
                    cls = qkv_scr[i * nlt + p, pl.ds(r0 + r4, n // d4, stride=d4), :]
                    c4_ref[0, r4, rows4, cols] = cls.astype(BF16)
                    cls_scr[i * nlt + p, r4, rows4, :] = cls
                for r4 in range(d4):
                    for a in range(d16 // d4):
                        c16_ref[0, r4 + d4 * a, rows16, cols] = cls_scr[
                            i * nlt + p, r4, pl.ds(r0 // d4 + a, n // d16, stride=d16 // d4), :].astype(BF16)


def _inproj(x2, g1, w_in, avg, lng, lnb, ws, bs, aog, cos, sin, seq):
    t = x2.shape[0]
    tm = TM_PROJ
    nseq = seq // tm
    full = lambda shape: pl.BlockSpec(shape, lambda i: (0,) * len(shape))
    rows = lambda w: pl.BlockSpec((tm, w), lambda i: (i, 0))
    classes = lambda dil: pl.BlockSpec((1, dil, tm // dil, B_WIDTH), lambda i: (i // nseq, 0, i % nseq, 0))
    class_shape = lambda dil: jax.ShapeDtypeStruct((t // seq, dil, seq // dil, B_WIDTH), BF16)
    return pl.pallas_call(
        _inproj_kernel,
        grid=(t // tm,),
        in_specs=[rows(D_MODEL), full((1, D_MODEL)), full(w_in.shape), full(avg.shape),
                  full((1, A_WIDTH)), full((1, A_WIDTH)), full(ws.shape), full(bs.shape),
                  full((1, A_WIDTH)),
                  pl.BlockSpec((tm, LANES), lambda i: (i % nseq, 0)),
                  pl.BlockSpec((tm, LANES), lambda i: (i % nseq, 0))],
        out_specs=[rows(A_WIDTH)] + [rows(B_WIDTH)] * 3 + [classes(DILATIONS[1])] * 3 + [classes(DILATIONS[2])] * 3,
        out_shape=([jax.ShapeDtypeStruct((t, A_WIDTH), BF16)] + [jax.ShapeDtypeStruct((t, B_WIDTH), BF16)] * 3
                   + [class_shape(DILATIONS[1])] * 3 + [class_shape(DILATIONS[2])] * 3),
        scratch_shapes=[pltpu.VMEM((tm, A_WIDTH), F32), pltpu.VMEM((3 * B_WIDTH // LANES, tm, LANES), F32),
                        pltpu.VMEM((3 * B_WIDTH // LANES, DILATIONS[1], tm // DILATIONS[1], LANES), F32)],
        compiler_params=pltpu.CompilerParams(dimension_semantics=("arbitrary",), vmem_limit_bytes=VMEM_LIMIT),
        name="inproj",
    )(x2, g1, w_in, avg, lng, lnb, ws, bs, aog, cos, sin)


def _attn_block(qb, kw, vw, bias, first_head):
    zero = jnp.zeros_like(qb)
    q2 = jnp.concatenate([jnp.where(first_head, qb, zero), jnp.where(first_head, zero, qb)], axis=0)
    s = lax.dot_general(q2, kw, (((1,), (1,)), ((), ())), preferred_element_type=F32) + bias
    m = jnp.max(s, axis=-1, keepdims=True)
    p = jnp.exp(s - m)
    l = jnp.sum(p, axis=-1, keepdims=True)
    o = _dot(p.astype(BF16), vw) / l
    lse = jnp.broadcast_to(m + jnp.log(l), o.shape)
    return (jnp.where(first_head, o[:BAND], o[BAND:]),
            jnp.where(first_head, lse[:BAND], lse[BAND:]))


def _attn_kernel(q1, k1, v1, q4, k4, v4, q16, k16, v16, bias_band_ref, bias_first_ref, o_ref, o_scr, l_scr):
    seq = q1.shape[0]
    first_head = lax.broadcasted_iota(I32, (BAND, LANES), 1) < HEAD_DIM
    branches = ((q1, k1, v1), (q4, k4, v4), (q16, k16, v16))
    for bi, dil in enumerate(DILATIONS):
        q_r, k_r, v_r = branches[bi]
        length = seq // dil
        for r in range(dil):
            for n in range(length // BAND):
                lo = n * BAND
                if bi == 0:
                    ref_slice = lambda ref, a, b: ref[a:b, :]
                else:
                    ref_slice = lambda ref, a, b, r=r: ref[0, r, a:b, :]
                qb = ref_slice(q_r, lo, lo + BAND)
                if n == 0:
                    kw, vw, bias = ref_slice(k_r, 0, BAND), ref_slice(v_r, 0, BAND), bias_first_ref[...]
                else:
                    kw, vw = ref_slice(k_r, lo - BAND, lo + BAND), ref_slice(v_r, lo - BAND, lo + BAND)
                    bias = bias_band_ref[...]
                o, lse = _attn_block(qb, kw, vw, bias, first_head)
                if dil == 1:
                    o_scr[bi, lo:lo + BAND, :] = o
                    l_scr[bi, lo:lo + BAND, :] = lse
                else:
                    dst = pl.ds(r + dil * lo, BAND, stride=dil)
                    o_scr[bi, dst, :] = o
                    l_scr[bi, dst, :] = lse
    lses = [l_scr[i] for i in range(3)]
    m = jnp.maximum(jnp.maximum(lses[0], lses[1]), lses[2])
    es = [jnp.exp(l - m) for l in lses]
    den = es[0] + es[1] + es[2]
    o_ref[...] = (es[0] / den) * o_scr[0] + (es[1] / den) * o_scr[1] + (es[2] / den) * o_scr[2]


def _attention(qkv, batch, seq):
    rel = (np.arange(BAND)[:, None] + BAND) - np.arange(2 * BAND)[None, :]
    band = np.where((rel >= 0) & (rel <= BAND), 0.0, NEG_INF).astype(np.float32)
    bias_band = jnp.asarray(np.concatenate([band, band], axis=0))
    bias_first = jnp.asarray(np.concatenate([band[:, BAND:], band[:, BAND:]], axis=0))
    nat = pl.BlockSpec((seq, LANES), lambda b, p: (b, p))
    cls = lambda dil: pl.BlockSpec((1, dil, seq // dil, LANES), lambda b, p: (b, 0, 0, p))
    full = lambda a: pl.BlockSpec(a.shape, lambda b, p: (0, 0))
    return pl.pallas_call(
        _attn_kernel,
        grid=(batch, B_WIDTH // LANES),
        in_specs=[nat] * 3 + [cls(4)] * 3 + [cls(16)] * 3 + [full(bias_band), full(bias_first)],
        out_specs=pl.BlockSpec((seq, LANES), lambda b, p: (b, p)),
        out_shape=jax.ShapeDtypeStruct((batch * seq, B_WIDTH), F32),
        scratch_shapes=[pltpu.VMEM((3, seq, LANES), F32), pltpu.VMEM((3, seq, LANES), F32)],
        compiler_params=pltpu.CompilerParams(dimension_semantics=("arbitrary", "arbitrary"),
                                             vmem_limit_bytes=VMEM_LIMIT),
        name="dilated_attention",
    )(*qkv, bias_band, bias_first)


def _mix_kernel(a_ref, bm_ref, x_ref, bog_ref, wout_ref, n2g_ref, wr_ref, br_ref, tri_ref,
                h_ref, hn_ref, idx_ref, gate_ref, cnt_ref, cnt_scr):
    tm = x_ref.shape[0]
    bn = _rms(bm_ref[...], bog_ref[...]).astype(BF16)
    mixed = jnp.concatenate([a_ref[...], bn], axis=1)
    h = x_ref[...] + _dot(mixed, wout_ref[...])
    h_ref[...] = h
    hn = _rms(h, n2g_ref[...])
    for j in range(ROW_TILES):
        hn_ref[pl.ds(j, tm, stride=ROW_TILES), :] = hn[:, j * LANES:(j + 1) * LANES]
    logits = _dot(hn.astype(BF16), wr_ref[...]) + br_ref[...]
    lane = lax.broadcasted_iota(I32, (tm, LANES), 1)
    vals, idxs = [], []
    for _ in range(TOP_K):
        m = jnp.max(logits, axis=-1, keepdims=True)
        am = jnp.min(jnp.where(logits == m, lane, LANES), axis=-1, keepdims=True)
        vals.append(m)
        idxs.append(am)
        logits = jnp.where(lane == am, -jnp.inf, logits)
    es = [jnp.exp(v - vals[0]) for v in vals]
    den = es[0] + es[1] + es[2] + es[3]
    @pl.when(pl.program_id(0) == 0)
    def _():
        cnt_scr[...] = jnp.zeros_like(cnt_scr)

    chosen = jnp.zeros((tm, LANES), F32)
    for kk in range(TOP_K):
        chosen = chosen + (lane == idxs[kk]).astype(F32)
    before = _dot(tri_ref[...], chosen.astype(BF16)) + cnt_scr[...]
    cnt_scr[...] = cnt_scr[...] + jnp.sum(chosen, axis=0, keepdims=True)
    cnt_ref[...] = jnp.broadcast_to(cnt_scr[...], cnt_ref.shape)

    idx_out = jnp.zeros((tm, LANES), I32)
    gate_out = jnp.zeros((tm, LANES), F32)
    for kk in range(TOP_K):
        rank = jnp.sum(jnp.where(lane == idxs[kk], before, 0.0), axis=-1, keepdims=True).astype(I32)
        idx_out = jnp.where(lane == kk, idxs[kk], idx_out)
        idx_out = jnp.where(lane == TOP_K + kk, rank, idx_out)
        gate_out = jnp.where(lane == kk, es[kk] / den, gate_out)
    idx_ref[...] = idx_out.T[:SUBLANES, :]
    gate_ref[...] = gate_out


def _mix(a_out, b_mix, x2, bog, w_out, n2g, w_r, b_r):
    t = x2.shape[0]
    tm = TM_MIX
    full = lambda shape: pl.BlockSpec(shape, lambda i: (0,) * len(shape))
    rows = lambda w: pl.BlockSpec((tm, w), lambda i: (i, 0))
    tri = jnp.asarray(np.tril(np.ones((tm, tm), np.float32), -1), dtype=BF16)
    return pl.pallas_call(
        _mix_kernel,
        grid=(t // tm,),
        in_specs=[rows(A_WIDTH), rows(B_WIDTH), rows(D_MODEL), full((1, B_WIDTH)), full(w_out.shape),
                  full((1, D_MODEL)), full(w_r.shape), full((1, LANES)), full((tm, tm))],
        out_specs=[rows(D_MODEL), pl.BlockSpec((tm * ROW_TILES, LANES), lambda i: (i, 0)),
                   pl.BlockSpec((SUBLANES, tm), lambda i: (0, i)), rows(LANES), full((SUBLANES, LANES))],
        out_shape=[jax.ShapeDtypeStruct((t, D_MODEL), F32), jax.ShapeDtypeStruct((t * ROW_TILES, LANES), F32),
                   jax.ShapeDtypeStruct((SUBLANES, t), I32), jax.ShapeDtypeStruct((t, LANES), F32),
                   jax.ShapeDtypeStruct((SUBLANES, LANES), F32)],
        scratch_shapes=[pltpu.VMEM((1, LANES), F32)],
        compiler_params=pltpu.CompilerParams(dimension_semantics=("arbitrary",), vmem_limit_bytes=VMEM_LIMIT),
        name="mix_router",
    )(a_out, b_mix, x2, bog, w_out, n2g, w_r, b_r, tri)


def _tile_rows(ref, row):
    return ref.at[pl.ds(pl.multiple_of(row * ROW_TILES, SUBLANES), ROW_TILES), :]


def _zero_runs(pad_start_ref, pad_count_ref, nt_ref, xs_hbm, zero_scr, sem, wait):
    tile_rows = TM_MOE * ROW_TILES

    def copy(rows, first_slot):
        c = pltpu.make_async_copy(zero_scr.at[pl.ds(0, rows * ROW_TILES), :],
                                  xs_hbm.at[pl.ds(pl.multiple_of(first_slot * ROW_TILES, SUBLANES), rows * ROW_TILES), :],
                                  sem)
        c.wait() if wait else c.start()

    def expert_body(e, c):
        count = pad_count_ref[e]
        slot = pad_start_ref[e]
        bit = TM_MOE // 2
        while bit:
            @pl.when((count & bit) != 0)
            def _(bit=bit, slot=slot):
                copy(bit, slot)
            slot = slot + (count & bit)
            bit //= 2
        return c
    lax.fori_loop(0, N_EXPERTS, expert_body, 0)

    def idle_body(tile, c):
        copy(TM_MOE, tile * TM_MOE)
        return c
    lax.fori_loop(nt_ref[0], xs_hbm.shape[0] // tile_rows, idle_body, 0)


def _dispatch_kernel(pad_start_ref, pad_count_ref, nt_ref, pos_ref, hn_ref, xs_hbm, zero_scr, sem, zero_sem):
    n_rows = pos_ref.shape[-1] * TOP_K
    i = pl.program_id(0)

    @pl.when(i == 0)
    def _():
        zero_scr[...] = jnp.zeros_like(zero_scr)
        _zero_runs(pad_start_ref, pad_count_ref, nt_ref, xs_hbm, zero_scr, zero_sem, wait=False)

    def body(g, c):
        for u in range(DMA_UNROLL):
            token = g * (DMA_UNROLL // TOP_K) + u // TOP_K
            pltpu.make_async_copy(_tile_rows(hn_ref, token), _tile_rows(xs_hbm, pos_ref[u % TOP_K, token]),
                                  sem).start(priority=u % 2)
        return c
    lax.fori_loop(0, n_rows // DMA_UNROLL, body, 0)

    total = n_rows * ROW_TILES
    pltpu.make_async_copy(xs_hbm.at[pl.ds(0, total), :], xs_hbm.at[pl.ds(0, total), :], sem).wait()

    @pl.when(i == pl.num_programs(0) - 1)
    def _():
        _zero_runs(pad_start_ref, pad_count_ref, nt_ref, xs_hbm, zero_scr, zero_sem, wait=True)


def _dispatch(pos, pad_start, pad_count, n_tiles, hn_tiles, n_slots):
    t = hn_tiles.shape[0] // ROW_TILES
    tm = TM_DISPATCH
    steps = t // tm
    assert t % tm == 0
    grid_spec = pltpu.PrefetchScalarGridSpec(
        num_scalar_prefetch=3,
        grid=(steps,),
        in_specs=[pl.BlockSpec((TOP_K, tm), lambda i, *_: (0, i), memory_space=pltpu.SMEM),
                  pl.BlockSpec((tm * ROW_TILES, LANES), lambda i, *_: (i, 0))],
        out_specs=pl.BlockSpec(memory_space=pl.ANY),
        scratch_shapes=[pltpu.VMEM((TM_MOE * ROW_TILES, LANES), F32), pltpu.SemaphoreType.DMA,
                        pltpu.SemaphoreType.DMA],
    )
    return pl.pallas_call(
        _dispatch_kernel,
        grid_spec=grid_spec,
        out_shape=jax.ShapeDtypeStruct((n_slots * ROW_TILES, LANES), F32),
        compiler_params=pltpu.CompilerParams(dimension_semantics=("arbitrary",), vmem_limit_bytes=VMEM_LIMIT),
        name="dispatch",
    )(pad_start, pad_count, n_tiles, pos, hn_tiles)


def _moe_kernel(te_ref, nt_ref, x_ref, wgu_ref, bgu_ref, wd_ref, bd_ref, y_ref, wgu_bf, wd_bf, act_scr):
    tm = TM_MOE
    i = pl.program_id(0)

    @pl.when(i < nt_ref[0])
    def _():
        @pl.when((i == 0) | (te_ref[i] != te_ref[jnp.maximum(i - 1, 0)]))
        def _():
            wgu_bf[...] = wgu_ref[0].astype(BF16)
            wd_bf[...] = wd_ref[0].astype(BF16)

        x = jnp.concatenate([x_ref[pl.ds(j, tm, stride=ROW_TILES), :] for j in range(ROW_TILES)],
                            axis=1).astype(BF16)
        width = D_FF // MOE_FF_CHUNKS
        for c in range(MOE_FF_CHUNKS):
            lo, hi = c * width, (c + 1) * width
            gate = _dot(x, wgu_bf[:, lo:hi]) + bgu_ref[0, :, lo:hi]
            up = _dot(x, wgu_bf[:, D_FF + lo:D_FF + hi]) + bgu_ref[0, :, D_FF + lo:D_FF + hi]
            gate = jnp.minimum(gate, SWIGLU_LIMIT)
            up = jnp.clip(up, -SWIGLU_LIMIT, SWIGLU_LIMIT)
            act_scr[:, lo:hi] = ((up + 1.0) * (gate * jax.nn.sigmoid(gate * SWIGLU_ALPHA))).astype(BF16)
        y = _dot(act_scr[...], wd_bf[...]) + bd_ref[0]
        for j in range(ROW_TILES):
            y_ref[pl.ds(j, tm, stride=ROW_TILES), :] = y[:, j * LANES:(j + 1) * LANES]

    @pl.when(i >= nt_ref[0])
    def _():
        y_ref[...] = jnp.zeros_like(y_ref)


def _experts(tile_expert, n_tiles, xs, wgu, bgu, wd, bd, nt_max):
    tm = TM_MOE
    tile_in = pl.BlockSpec((tm * ROW_TILES, LANES), lambda i, te, nt: (jnp.minimum(i, nt[0] - 1), 0))
    tile = pl.BlockSpec((tm * ROW_TILES, LANES), lambda i, te, nt: (i, 0))
    grid_spec = pltpu.PrefetchScalarGridSpec(
        num_scalar_prefetch=2,
        grid=(nt_max,),
        in_specs=[tile_in,
                  pl.BlockSpec((1, D_MODEL, 2 * D_FF), lambda i, te, nt: (te[i], 0, 0)),
                  pl.BlockSpec((1, 1, 2 * D_FF), lambda i, te, nt: (te[i], 0, 0)),
                  pl.BlockSpec((1, D_FF, D_MODEL), lambda i, te, nt: (te[i], 0, 0)),
                  pl.BlockSpec((1, 1, D_MODEL), lambda i, te, nt: (te[i], 0, 0))],
        out_specs=tile,
        scratch_shapes=[pltpu.VMEM((D_MODEL, 2 * D_FF), BF16), pltpu.VMEM((D_FF, D_MODEL), BF16),
                        pltpu.VMEM((tm, D_FF), BF16)],
    )
    return pl.pallas_call(
        _moe_kernel,
        grid_spec=grid_spec,
        out_shape=jax.ShapeDtypeStruct((nt_max * tm * ROW_TILES, LANES), F32),
        compiler_params=pltpu.CompilerParams(dimension_semantics=("arbitrary",), vmem_limit_bytes=VMEM_LIMIT),
        name="experts",
    )(tile_expert, n_tiles, xs, wgu, bgu, wd, bd)


def _combine_kernel(*refs):
    pos_refs, pos_next_refs = refs[:TOP_K], refs[TOP_K:2 * TOP_K]
    ys_hbm, gate_ref, h_ref, g_ref, o_ref, buf0, buf1, acc_scr, sem = refs[2 * TOP_K:]
    tm = h_ref.shape[0]
    n_rows = tm * TOP_K
    i = pl.program_id(0)
    bufs = (buf0, buf1)

    def gather(idx_refs, buf, s):
        for kk in range(TOP_K):
            def body(g, c, kk=kk):
                for u in range(DMA_UNROLL):
                    token = g * DMA_UNROLL + u
                    pltpu.make_async_copy(_tile_rows(ys_hbm, idx_refs[kk][0, 0, token]),
                                          _tile_rows(buf, kk * tm + token), sem.at[s]).start(priority=u % 2)
                return c
            lax.fori_loop(0, tm // DMA_UNROLL, body, 0)

    @pl.when(i == 0)
    def _():
        gather(pos_refs, buf0, 0)

    for p in (0, 1):
        @pl.when(i % 2 == p)
        def _(p=p):
            @pl.when(i + 1 < pl.num_programs(0))
            def _():
                gather(pos_next_refs, bufs[1 - p], 1 - p)

            buf = bufs[p]
            pltpu.make_async_copy(ys_hbm.at[pl.ds(0, n_rows * ROW_TILES), :], buf, sem.at[p]).wait()
            gates = gate_ref[...]
            gk = [jnp.broadcast_to(gates[:, kk:kk + 1], (tm, LANES)) for kk in range(TOP_K)]
            ss = jnp.zeros((tm, 1), F32)
            for j in range(ROW_TILES):
                cols = slice(j * LANES, (j + 1) * LANES)
                moe = gk[0] * buf[pl.ds(j, tm, stride=ROW_TILES), :]
                for kk in range(1, TOP_K):
                    moe = moe + gk[kk] * buf[pl.ds(kk * tm * ROW_TILES + j, tm, stride=ROW_TILES), :]
                acc = h_ref[:, cols] + moe
                acc_scr[:, cols] = acc
                ss = ss + jnp.sum(acc * acc, axis=-1, keepdims=True)
            inv = lax.rsqrt(ss / D_MODEL + NORM_EPS)
            o_ref[...] = acc_scr[...] * inv * g_ref[...]


def _combine(pos, ys, gates, h, g):
    t = h.shape[0]
    tm = TM_COMB
    steps = t // tm
    pos_k = [pos[kk].reshape(steps, 1, tm) for kk in range(TOP_K)]
    smem = lambda f: pl.BlockSpec((1, 1, tm), f, memory_space=pltpu.SMEM)
    row_buf = pltpu.VMEM((TOP_K * tm * ROW_TILES, LANES), F32)
    return pl.pallas_call(
        _combine_kernel,
        grid=(steps,),
        in_specs=[smem(lambda i: (i, 0, 0))] * TOP_K
                 + [smem(lambda i: (jnp.minimum(i + 1, steps - 1), 0, 0))] * TOP_K
                 + [pl.BlockSpec(memory_space=pl.ANY),
                  pl.BlockSpec((tm, LANES), lambda i: (i, 0)),
                  pl.BlockSpec((tm, D_MODEL), lambda i: (i, 0)),
                  pl.BlockSpec((1, D_MODEL), lambda i: (0, 0))],
        out_specs=pl.BlockSpec((tm, D_MODEL), lambda i: (i, 0)),
        out_shape=jax.ShapeDtypeStruct((t, D_MODEL), F32),
        scratch_shapes=[row_buf, row_buf, pltpu.VMEM((tm, D_MODEL), F32), pltpu.SemaphoreType.DMA((2,))],
        compiler_params=pltpu.CompilerParams(dimension_semantics=("arbitrary",), vmem_limit_bytes=VMEM_LIMIT),
        name="combine",
    )(*pos_k, *pos_k, ys, gates, h, g)


def _routing_plan(idx_rank, counts_f, n_tokens):
    tm = TM_MOE
    nt_max = n_tokens * TOP_K // tm + N_EXPERTS
    experts = jnp.arange(N_EXPERTS, dtype=I32)
    counts = counts_f[0, :N_EXPERTS].astype(I32)
    tiles_e = (counts + tm - 1) // tm
    tile_end = jnp.cumsum(tiles_e)
    first_slot = (tile_end - tiles_e) * tm
    n_tiles = tile_end[-1]
    tile_ids = jnp.arange(nt_max, dtype=I32)
    te = jnp.minimum(jnp.sum((tile_ids[:, None] >= tile_end[None, :]).astype(I32), axis=1), N_EXPERTS - 1)
    last_e = jnp.take(te, jnp.maximum(n_tiles - 1, 0))
    te = jnp.where(tile_ids < n_tiles, te, last_e).astype(I32)
    expert = idx_rank[:TOP_K]
    rank = idx_rank[TOP_K:2 * TOP_K]
    pos = jnp.sum(jnp.where(expert[:, :, None] == experts[None, None, :], first_slot[None, None, :], 0), axis=-1) + rank
    return (te, n_tiles.reshape(1).astype(I32), pos.astype(I32), (first_slot + counts).astype(I32),
            (tiles_e * tm - counts).astype(I32), nt_max)


def kernel(x, norm1_g, w_in, a_ln_g, a_ln_b, a_w_s, a_b_s, a_out_g, b_out_g, w_out, norm2_g, w_router,
           b_router, w_gate_up, b_gate_up, w_down, b_down, normf_g):
    batch, seq, _ = x.shape
    t = batch * seq
    assert seq % (TM_PROJ) == 0 and t % TM_MOE == 0 and seq // DILATIONS[-1] == BAND
    h = x.reshape(t, D_MODEL)

    pos = jnp.arange(seq, dtype=F32)
    inv = ROPE_THETA ** (-jnp.arange(0, HEAD_DIM, 2, dtype=F32) / HEAD_DIM)
    ang = pos[:, None] * inv[None, :]
    cos = jnp.tile(jnp.cos(ang), (1, 2 * LANES // HEAD_DIM))
    sin = jnp.tile(jnp.concatenate([-jnp.sin(ang), jnp.sin(ang)], axis=1), (1, LANES // HEAD_DIM))
    head_of_lane = np.arange(A_WIDTH) // HEAD_DIM
    avg = jnp.asarray((head_of_lane[:, None] == head_of_lane[None, :]).astype(np.float32) / HEAD_DIM, dtype=BF16)
    row2 = lambda v: v.reshape(1, -1).astype(F32)

    for layer in range(norm1_g.shape[0]):
        causal = np.tril(np.ones((CHUNK, CHUNK), dtype=bool))
        ws = jnp.where(causal[None], a_w_s[layer], 0.0).astype(BF16)
        bs = jnp.repeat(a_b_s[layer].astype(F32).T, HEAD_DIM, axis=1)
        a_out, *qkv = _inproj(h, row2(norm1_g[layer]), w_in[layer].astype(BF16), avg,
                              row2(a_ln_g[layer]), row2(a_ln_b[layer]), ws, bs, row2(a_out_g[layer]),
                              cos, sin, seq)
        b_mix = _attention(qkv, batch, seq)
        w_r = jnp.pad(w_router[layer], ((0, 0), (0, LANES - N_EXPERTS))).astype(BF16)
        b_r = jnp.concatenate([b_router[layer].astype(F32), jnp.full((LANES - N_EXPERTS,), NEG_INF, F32)])
        h_mid, hn, idx_rank, gates, counts = _mix(a_out, b_mix, h, row2(b_out_g[layer]), w_out[layer].astype(BF16),
                                                  row2(norm2_g[layer]), w_r, b_r.reshape(1, LANES))
        te, n_tiles, pos, pad_start, pad_count, nt_max = _routing_plan(idx_rank, counts, t)
        xs = _dispatch(pos, pad_start, pad_count, n_tiles, hn, nt_max * TM_MOE)
        ys = _experts(te, n_tiles, xs,
                      w_gate_up[layer], b_gate_up[layer].reshape(N_EXPERTS, 1, 2 * D_FF),
                      w_down[layer], b_down[layer].reshape(N_EXPERTS, 1, D_MODEL), nt_max)
        last = layer == norm1_g.shape[0] - 1
        assert last, "the combine kernel fuses the final norm; depth > 1 is not supported"
        h = _combine(pos, ys, gates, h_mid, row2(normf_g))
    return h.reshape(batch, seq, D_MODEL)
```
